```python
import jax, jax.numpy as jnp
from jax import lax
import numpy as np

D_MODEL = 1024
BATCH = 8
SEQ = 4096
DEPTH = 2
DEC_BATCH = 16
DEC_SEQ = 2048
PAST_LEN = 128

GRID_W = 64
HEAD_DIM = 64
NA_HEADS = 8
NA_WIDTH = NA_HEADS * HEAD_DIM
NA_ROWS = 8
NA_COLS = 16
NA_QBLOCK = 16
NA_KBLOCK = 2 * NA_QBLOCK
SGU_GROUPS = 6
SGU_GROUP_CH = 128
SGU_WIDTH = SGU_GROUPS * SGU_GROUP_CH
SGU_CHUNK = 128
DIL_PATTERNS = ((128, 1), (512, 4), (2048, 16))
DIL_HEADS = 4
DIL_WIDTH = len(DIL_PATTERNS) * DIL_HEADS * HEAD_DIM
DIL_OUT_WIDTH = DIL_HEADS * HEAD_DIM
DIL_BLOCK = 64
ROPE_THETA = 10000.0
N_BRANCH = 3
OFF_A = 3 * NA_WIDTH
OFF_B = OFF_A + 2 * SGU_WIDTH
OFF_C = OFF_B + 3 * DIL_WIDTH
IN_COLS = OFF_C + N_BRANCH * D_MODEL
PEER_HEADS = 8
PEER_NKEYS = 128
PEER_N = PEER_NKEYS * PEER_NKEYS
PEER_DKEY = 256
PEER_TOPK = 16
PEER_BLOCK = 128
PLE_DIM = 256
LN_EPS = 1e-5
ALPHA = (2 * DEPTH) ** 0.25
BETA = (8 * DEPTH) ** -0.25

kernel_name = "hybrid_na_sgu_dilated_peer_encoder"


def layer_norm(x, g, b):
    xf = x.astype(jnp.float32)
    mu = jnp.mean(xf, axis=-1, keepdims=True)
    var = jnp.mean(jnp.square(xf - mu), axis=-1, keepdims=True)
    return ((xf - mu) * lax.rsqrt(var + LN_EPS) * g.astype(jnp.float32) + b.astype(jnp.float32)).astype(x.dtype)


def rope(x, pos):
    half = x.shape[-1] // 2
    inv = ROPE_THETA ** (-jnp.arange(half, dtype=jnp.float32) / half)
    ang = pos.astype(jnp.float32)[:, None] * inv[None, :]
    cos = jnp.cos(ang)[None, :, None, :]
    sin = jnp.sin(ang)[None, :, None, :]
    xf = x.astype(jnp.float32)
    x1, x2 = xf[..., :half], xf[..., half:]
    return jnp.concatenate([x1 * cos - x2 * sin, x2 * cos + x1 * sin], axis=-1).astype(x.dtype)


def neighborhood_attention(q, k, v, rpb):
    B, S, H, dh = q.shape
    rows = S // GRID_W
    kh = min(NA_ROWS, rows)
    nb = GRID_W // NA_QBLOCK
    qcol = np.arange(GRID_W).reshape(nb, NA_QBLOCK)
    kc0 = np.clip(np.arange(nb) * NA_QBLOCK - NA_QBLOCK // 2, 0, GRID_W - NA_KBLOCK)
    kcol = kc0[:, None] + np.arange(NA_KBLOCK)[None, :]
    cstart = np.clip(qcol - NA_COLS // 2, 0, GRID_W - NA_COLS)
    col_ok = (kcol[:, None, :] >= cstart[..., None]) & (kcol[:, None, :] < cstart[..., None] + NA_COLS)
    dc = np.clip(kcol[:, None, :] - qcol[..., None], -(NA_COLS - 1), NA_COLS - 1) + (NA_COLS - 1)
    bias_c = rpb[:, :, dc]
    scale = dh ** -0.5

    def row_block(r):
        rs = jnp.clip(r - kh // 2, 0, rows - kh)
        q_r = lax.dynamic_slice_in_dim(q, r * GRID_W, GRID_W, axis=1).reshape(B, nb, NA_QBLOCK, H, dh)
        k_r = lax.dynamic_slice_in_dim(k, rs * GRID_W, kh * GRID_W, axis=1).reshape(B, kh, GRID_W, H, dh)[:, :, kcol]
        v_r = lax.dynamic_slice_in_dim(v, rs * GRID_W, kh * GRID_W, axis=1).reshape(B, kh, GRID_W, H, dh)[:, :, kcol]
        s = jnp.einsum('bnqhd,bjnkhd->bhnqjk', q_r, k_r, preferred_element_type=jnp.float32) * scale
        dr = rs + jnp.arange(kh) - r + (NA_ROWS - 1)
        bias = jnp.take(bias_c, dr, axis=1).transpose(0, 2, 3, 1, 4)
        s = s + bias[None].astype(jnp.float32)
        s = jnp.where(col_ok[None, None, :, :, None, :], s, -jnp.inf)
        p = jax.nn.softmax(s.reshape(B, H, nb, NA_QBLOCK, kh * NA_KBLOCK), axis=-1).reshape(s.shape)
        o = jnp.einsum('bhnqjk,bjnkhd->bnqhd', p.astype(v.dtype), v_r)
        return o.reshape(B, GRID_W, H, dh)

    out = lax.map(row_block, jnp.arange(rows))
    return out.transpose(1, 0, 2, 3, 4).reshape(B, S, H, dh)


def spatial_gating(u, v, ln_g, ln_b, w_s, b_s):
    B, S, _ = u.shape
    u = jax.nn.gelu(u)
    v = layer_norm(jax.nn.gelu(v), ln_g, ln_b)
    n = S // SGU_CHUNK
    vc = v.reshape(B, n, SGU_CHUNK, SGU_GROUPS, SGU_GROUP_CH)
    s = jnp.einsum('gts,bnsgc->bntgc', w_s, vc) + b_s.T[None, None, :, :, None]
    return u * s.reshape(B, S, SGU_WIDTH)


def dilated_group(q, k, v, window, dil):
    B, S, H, dh = q.shape
    L = S // dil
    half = window // (2 * dil)
    QB = DIL_BLOCK
    nblk = -(-L // QB)
    Lp = nblk * QB

    def split(x):
        return x.reshape(B, L, dil, H, dh).transpose(0, 2, 1, 3, 4)

    def band(x):
        xp = jnp.pad(x, ((0, 0), (0, 0), (QB, Lp - L + QB), (0, 0), (0, 0))).reshape(B, dil, nblk + 2, QB, H, dh)
        return jnp.concatenate([xp[:, :, :-2], xp[:, :, 1:-1], xp[:, :, 2:]], axis=3)

    qd = jnp.pad(split(q), ((0, 0), (0, 0), (0, Lp - L), (0, 0), (0, 0))).reshape(B, dil, nblk, QB, H, dh)
    kb = band(split(k))
    vb = band(split(v))
    qn = np.arange(Lp).reshape(nblk, QB)
    kn = (np.arange(nblk)[:, None] - 1) * QB + np.arange(3 * QB)[None, :]
    rel = kn[:, None, :] - qn[:, :, None]
    ok = (np.abs(rel) <= half) & (kn[:, None, :] >= 0) & (kn[:, None, :] < L)
    s = jnp.einsum('brnqhd,brnkhd->brnhqk', qd, kb, preferred_element_type=jnp.float32) * (dh ** -0.5)
    s = jnp.where(ok[None, None, :, None], s, -jnp.inf)
    m = jnp.max(s, axis=-1, keepdims=True)
    e = jnp.exp(s - m)
    den = jnp.sum(e, axis=-1, keepdims=True)
    o = jnp.einsum('brnhqk,brnkhd->brnqhd', (e / den).astype(v.dtype), vb)
    lse = (m + jnp.log(den))[..., 0]
    o = o.reshape(B, dil, Lp, H, dh)[:, :, :L].transpose(0, 2, 1, 3, 4).reshape(B, S, H, dh)
    lse = lse.transpose(0, 1, 2, 4, 3).reshape(B, dil, Lp, H)[:, :, :L].transpose(0, 2, 1, 3).reshape(B, S, H)
    return o, lse


def dilated_attention(q, k, v):
    S = q.shape[1]
    pos = jnp.arange(S)
    q = rope(q, pos)
    k = rope(k, pos)
    outs, lses = [], []
    for g, (window, dil) in enumerate(DIL_PATTERNS):
        sl = slice(g * DIL_HEADS, (g + 1) * DIL_HEADS)
        o, lse = dilated_group(q[:, :, sl], k[:, :, sl], v[:, :, sl], window, dil)
        outs.append(o)
        lses.append(lse)
    w = jax.nn.softmax(jnp.stack(lses, axis=0), axis=0)
    return jnp.einsum('gbsh,gbshd->bshd', w.astype(q.dtype), jnp.stack(outs, axis=0))


def token_mixer(x, w_in, rpb, sgu_ln_g, sgu_ln_b, sgu_w, sgu_b, w_br_a, w_br_b, w_br_c, w_out):
    B, S, D = x.shape
    h = x @ w_in
    ha, hb, hc, hg = jnp.split(h, [OFF_A, OFF_B, OFF_C], axis=-1)
    qa, ka, va = [t.reshape(B, S, NA_HEADS, HEAD_DIM) for t in jnp.split(ha, 3, axis=-1)]
    u, vg = jnp.split(hb, 2, axis=-1)
    qc, kc, vc = [t.reshape(B, S, len(DIL_PATTERNS) * DIL_HEADS, HEAD_DIM) for t in jnp.split(hc, 3, axis=-1)]
    gates = jax.nn.sigmoid(hg).reshape(B, S, N_BRANCH, D)
    a = neighborhood_attention(qa, ka, va, rpb).reshape(B, S, NA_WIDTH) @ w_br_a
    b = spatial_gating(u, vg, sgu_ln_g, sgu_ln_b, sgu_w, sgu_b) @ w_br_b
    c = dilated_attention(qc, kc, vc).reshape(B, S, DIL_OUT_WIDTH) @ w_br_c
    merged = gates[:, :, 0] * a + gates[:, :, 1] * b + gates[:, :, 2] * c
    return merged @ w_out


def peer(x, w_q, k1, k2, u_tab, v_tab):
    B, S, D = x.shape
    T = B * S
    xt = x.reshape(T, D)
    q = (xt @ w_q).reshape(T, PEER_HEADS, 2, PEER_DKEY // 2)
    s1 = jnp.einsum('thd,hnd->thn', q[:, :, 0], k1, preferred_element_type=jnp.float32)
    s2 = jnp.einsum('thd,hnd->thn', q[:, :, 1], k2, preferred_element_type=jnp.float32)
    v1, i1 = lax.top_k(s1, PEER_TOPK)
    v2, i2 = lax.top_k(s2, PEER_TOPK)
    cand = (v1[..., :, None] + v2[..., None, :]).reshape(T, PEER_HEADS, PEER_TOPK * PEER_TOPK)
    cidx = (i1[..., :, None] * PEER_NKEYS + i2[..., None, :]).reshape(T, PEER_HEADS, PEER_TOPK * PEER_TOPK)
    top_s, sel = lax.top_k(cand, PEER_TOPK)
    idx = jnp.take_along_axis(cidx, sel, axis=-1)
    g = jax.nn.softmax(top_s, axis=-1)
    nblk = T // PEER_BLOCK

    def block(args):
        xb, ib, gb = args
        ue = u_tab[ib]
        act = jax.nn.gelu(jnp.einsum('phkd,pd->phk', ue, xb))
        ve = v_tab[ib]
        return jnp.einsum('phk,phkd->pd', (gb * act).astype(ve.dtype), ve)

    out = lax.map(block, (xt.reshape(nblk, PEER_BLOCK, D),
                          idx.reshape(nblk, PEER_BLOCK, PEER_HEADS, PEER_TOPK),
                          g.reshape(nblk, PEER_BLOCK, PEER_HEADS, PEER_TOPK)))
    return out.reshape(B, S, D)


def trunk(x, p, ln0_g, ln0_b, w_in, rpb, sgu_ln_g, sgu_ln_b, sgu_w, sgu_b, w_br_a, w_br_b, w_br_c,
          w_out, ln1_g, ln1_b, peer_wq, peer_k1, peer_k2, peer_u, peer_v, ple_w, ple_gate_w, ln2_g, ln2_b):
    x = layer_norm(x, ln0_g, ln0_b)
    for i in range(DEPTH):
        mix = token_mixer(x, w_in[i], rpb[i], sgu_ln_g[i], sgu_ln_b[i], sgu_w[i], sgu_b[i],
                          w_br_a[i], w_br_b[i], w_br_c[i], w_out[i])
        x = layer_norm(ALPHA * x + mix, ln1_g[i], ln1_b[i])
        ple = jax.nn.sigmoid(x @ ple_gate_w[i]) * (p[i] @ ple_w[i])
        ffn = peer(x, peer_wq[i], peer_k1[i], peer_k2[i], peer_u[i], peer_v[i])
        x = layer_norm(ALPHA * x + ffn + ple, ln2_g[i], ln2_b[i])
    return x


def setup_inputs(seed: int = 0) -> dict:
    key = jax.random.key(seed)
    ks = jax.random.split(key, 27)

    def nrm(k, shape, scale):
        return jax.random.normal(k, shape, jnp.float32) * scale

    D = D_MODEL
    return {
        "x_prompt": nrm(ks[0], (BATCH, SEQ, D), 1.0),
        "x_sample": nrm(ks[1], (DEC_BATCH, DEC_SEQ, D), 1.0),
        "p_prompt": nrm(ks[2], (DEPTH, BATCH, SEQ, PLE_DIM), 1.0),
        "p_sample": nrm(ks[3], (DEPTH, DEC_BATCH, DEC_SEQ, PLE_DIM), 1.0),
        "ln0_g": 1.0 + nrm(ks[4], (D,), 0.02),
        "ln0_b": nrm(ks[5], (D,), 0.02),
        "w_in": nrm(ks[6], (DEPTH, D, IN_COLS), D ** -0.5),
        "rpb": nrm(ks[7], (DEPTH, NA_HEADS, 2 * NA_ROWS - 1, 2 * NA_COLS - 1), 0.1),
        "sgu_ln_g": 1.0 + nrm(ks[8], (DEPTH, SGU_WIDTH), 0.02),
        "sgu_ln_b": nrm(ks[9], (DEPTH, SGU_WIDTH), 0.02),
        "sgu_w": nrm(ks[10], (DEPTH, SGU_GROUPS, SGU_CHUNK, SGU_CHUNK), SGU_CHUNK ** -0.5),
        "sgu_b": 1.0 + nrm(ks[11], (DEPTH, SGU_GROUPS, SGU_CHUNK), 0.01),
        "w_br_a": nrm(ks[12], (DEPTH, NA_WIDTH, D), NA_WIDTH ** -0.5),
        "w_br_b": nrm(ks[13], (DEPTH, SGU_WIDTH, D), SGU_WIDTH ** -0.5),
        "w_br_c": nrm(ks[14], (DEPTH, DIL_OUT_WIDTH, D), DIL_OUT_WIDTH ** -0.5),
        "w_out": nrm(ks[15], (DEPTH, D, D), BETA * D ** -0.5),
        "ln1_g": 1.0 + nrm(ks[16], (DEPTH, D), 0.02),
        "ln1_b": nrm(ks[17], (DEPTH, D), 0.02),
        "peer_wq": nrm(ks[18], (DEPTH, D, PEER_HEADS * PEER_DKEY), D ** -0.5),
        "peer_k1": nrm(ks[19], (DEPTH, PEER_HEADS, PEER_NKEYS, PEER_DKEY // 2), (PEER_DKEY // 2) ** -0.5),
        "peer_k2": nrm(ks[20], (DEPTH, PEER_HEADS, PEER_NKEYS, PEER_DKEY // 2), (PEER_DKEY // 2) ** -0.5),
        "peer_u": nrm(ks[21], (DEPTH, PEER_N, D), D ** -0.5),
        "peer_v": nrm(ks[22], (DEPTH, PEER_N, D), BETA),
        "ple_w": nrm(ks[23], (DEPTH, PLE_DIM, D), PLE_DIM ** -0.5),
        "ple_gate_w": nrm(ks[24], (DEPTH, D, D), D ** -0.5),
        "ln2_g": 1.0 + nrm(ks[25], (DEPTH, D), 0.02),
        "ln2_b": nrm(ks[26], (DEPTH, D), 0.02),
    }


def reference(x_prompt, x_sample, p_prompt, p_sample, ln0_g, ln0_b, w_in, rpb, sgu_ln_g, sgu_ln_b, sgu_w,
              sgu_b, w_br_a, w_br_b, w_br_c, w_out, ln1_g, ln1_b, peer_wq, peer_k1, peer_k2, peer_u, peer_v,
              ple_w, ple_gate_w, ln2_g, ln2_b):
    y_prompt = trunk(x_prompt, p_prompt, ln0_g, ln0_b, w_in, rpb, sgu_ln_g, sgu_ln_b, sgu_w, sgu_b,
                     w_br_a, w_br_b, w_br_c, w_out, ln1_g, ln1_b, peer_wq, peer_k1, peer_k2, peer_u, peer_v,
                     ple_w, ple_gate_w, ln2_g, ln2_b)
    y_sample = trunk(x_sample, p_sample, ln0_g, ln0_b, w_in, rpb, sgu_ln_g, sgu_ln_b, sgu_w, sgu_b,
                     w_br_a, w_br_b, w_br_c, w_out, ln1_g, ln1_b, peer_wq, peer_k1, peer_k2, peer_u, peer_v,
                     ple_w, ple_gate_w, ln2_g, ln2_b)
    return (y_prompt, y_sample)
```

```python
import functools

import numpy as np
import jax
import jax.numpy as jnp
from jax import lax
from jax.experimental import pallas as pl
from jax.experimental.pallas import tpu as pltpu

F32 = jnp.float32
BF16 = jnp.bfloat16

GRID_W = 64
HEAD_DIM = 64
NA_HEADS = 8
NA_WIDTH = NA_HEADS * HEAD_DIM
NA_ROWS = 8
NA_COLS = 16
SGU_GROUPS = 6
SGU_GROUP_CH = 128
SGU_WIDTH = SGU_GROUPS * SGU_GROUP_CH
SGU_CHUNK = 128
DIL_PATTERNS = ((128, 1), (512, 4), (2048, 16))
DIL_HEADS = 4
DIL_GW = DIL_HEADS * HEAD_DIM
DIL_WIDTH = len(DIL_PATTERNS) * DIL_GW
ROPE_THETA = 10000.0
N_BRANCH = 3
PEER_HEADS = 8
PEER_NKEYS = 128
PEER_DKEY = 256
PEER_TOPK = 16
LN_EPS = 1e-5
NEG_BIG = -1e30

LANES = 128
VMEM_LIMIT = 48 * 1024 * 1024


def _cparams(sem):
    return pltpu.CompilerParams(dimension_semantics=sem, vmem_limit_bytes=VMEM_LIMIT)


def _ln(x, g, b):
    mu = jnp.mean(x, axis=-1, keepdims=True)
    xc = x - mu
    var = jnp.mean(xc * xc, axis=-1, keepdims=True)
    return xc * lax.rsqrt(var + LN_EPS) * g + b


def _gelu(x):
    return x * (0.5 * (1.0 + jnp.tanh(0.7978845608028654 * (x + 0.044715 * (x * x * x)))))


def _sigmoid(x):
    return 1.0 / (1.0 + jnp.exp(-x))


def _dot_nt(a, b):
    return lax.dot_general(a, b, (((1,), (1,)), ((), ())), preferred_element_type=F32)


def _ln0_kernel(x_ref, g_ref, b_ref, xf_ref, xb_ref):
    y = _ln(x_ref[...], g_ref[...], b_ref[...])
    xf_ref[...] = y
    xb_ref[...] = y.astype(BF16)


def _ln0(x, g, b, tm=512):
    T, D = x.shape
    return pl.pallas_call(
        _ln0_kernel,
        grid=(T // tm,),
        in_specs=[pl.BlockSpec((tm, D), lambda i: (i, 0)),
                  pl.BlockSpec((1, D), lambda i: (0, 0)),
                  pl.BlockSpec((1, D), lambda i: (0, 0))],
        out_specs=[pl.BlockSpec((tm, D), lambda i: (i, 0)),
                   pl.BlockSpec((tm, D), lambda i: (i, 0))],
        out_shape=[jax.ShapeDtypeStruct((T, D), F32), jax.ShapeDtypeStruct((T, D), BF16)],
        compiler_params=_cparams(("parallel",)),
        name="ln0",
    )(x, g.reshape(1, D), b.reshape(1, D))


def _matmul_kernel(x_ref, w_ref, o_ref):
    o_ref[...] = jnp.dot(x_ref[...], w_ref[...], preferred_element_type=F32).astype(o_ref.dtype)


def _inproj(xb, w, tm=1024, tn=768):
    T, K = xb.shape
    N = w.shape[1]
    tm = min(tm, T)
    return pl.pallas_call(
        _matmul_kernel,
        grid=(T // tm, N // tn),
        in_specs=[pl.BlockSpec((tm, K), lambda i, j: (i, 0)),
                  pl.BlockSpec((K, tn), lambda i, j: (0, j))],
        out_specs=pl.BlockSpec((tm, tn), lambda i, j: (i, j)),
        out_shape=jax.ShapeDtypeStruct((T, N), BF16),
        compiler_params=_cparams(("parallel", "parallel")),
        name="inproj",
    )(xb, w)


def _na_kernel(q_ref, k_ref, v_ref, bias_ref, o_ref, *, rows):
    r = pl.program_id(1)
    rs = jnp.clip(r - NA_ROWS // 2, 0, rows - NA_ROWS)
    start = pl.multiple_of(rs * GRID_W, GRID_W)
    nk = NA_ROWS * GRID_W
    first = lax.broadcasted_iota(jnp.int32, (GRID_W, LANES), 1) < HEAD_DIM
    scale = HEAD_DIM ** -0.5
    for p in range(NA_WIDTH // LANES):
        cols = slice(p * LANES, (p + 1) * LANES)
        qp = q_ref[0, :, cols].astype(F32) * scale
        kp = k_ref[0, pl.ds(start, nk), cols]
        vp = v_ref[0, pl.ds(start, nk), cols]
        q2 = jnp.concatenate([jnp.where(first, qp, 0.0), jnp.where(first, 0.0, qp)], axis=0).astype(BF16)
        s = _dot_nt(q2, kp) + bias_ref[0, p * 2 * GRID_W:(p + 1) * 2 * GRID_W, :]
        m = jnp.max(s, axis=-1, keepdims=True)
        e = jnp.exp(s - m)
        l = jnp.sum(e, axis=-1, keepdims=True)
        o2 = jnp.dot(e.astype(BF16), vp, preferred_element_type=F32) * (1.0 / l)
        o_ref[0, :, cols] = jnp.where(first, o2[:GRID_W], o2[GRID_W:]).astype(o_ref.dtype)


def _na_bias_table(rpb):
    qcol = np.arange(GRID_W)
    kcol = np.arange(GRID_W)
    cstart = np.clip(qcol - NA_COLS // 2, 0, GRID_W - NA_COLS)
    ok = (kcol[None, :] >= cstart[:, None]) & (kcol[None, :] < cstart[:, None] + NA_COLS)
    dc = np.clip(kcol[None, :] - qcol[:, None], -(NA_COLS - 1), NA_COLS - 1) + (NA_COLS - 1)
    b = jnp.where(ok[None, None], rpb[:, :, dc], NEG_BIG)
    cases = []
    for c in range(NA_ROWS):
        dr = np.arange(NA_ROWS) - c + (NA_ROWS - 1)
        t = b[:, dr]
        cases.append(t.transpose(0, 2, 1, 3).reshape(NA_HEADS * GRID_W, NA_ROWS * GRID_W))
    return jnp.stack(cases, axis=0).astype(F32)


def _na_attention(h3, bias_tab, col0):
    B, S, _ = h3.shape
    rows = S // GRID_W
    cb = col0 // NA_WIDTH

    def case_of(r):
        return r - jnp.clip(r - NA_ROWS // 2, 0, rows - NA_ROWS)

    return pl.pallas_call(
        functools.partial(_na_kernel, rows=rows),
        grid=(B, rows),
        in_specs=[pl.BlockSpec((1, GRID_W, NA_WIDTH), lambda b, r: (b, r, cb)),
                  pl.BlockSpec((1, S, NA_WIDTH), lambda b, r: (b, 0, cb + 1)),
                  pl.BlockSpec((1, S, NA_WIDTH), lambda b, r: (b, 0, cb + 2)),
                  pl.BlockSpec((1, NA_HEADS * GRID_W, NA_ROWS * GRID_W), lambda b, r: (case_of(r), 0, 0))],
        out_specs=pl.BlockSpec((1, GRID_W, NA_WIDTH), lambda b, r: (b, r, 0)),
        out_shape=jax.ShapeDtypeStruct((B, S, NA_WIDTH), BF16),
        compiler_params=_cparams(("parallel", "arbitrary")),
        name="na_attn",
    )(h3, h3, h3, bias_tab)


def _sgu_kernel(u_ref, v_ref, g_ref, b_ref, ws_ref, bs_ref, o_ref, *, nchunk):
    for c in range(nchunk):
        rs = slice(c * SGU_CHUNK, (c + 1) * SGU_CHUNK)
        v = _ln(_gelu(v_ref[0, rs, :].astype(F32)), g_ref[...], b_ref[...]).astype(BF16)
        u = _gelu(u_ref[0, rs, :].astype(F32))
        for g in range(SGU_GROUPS):
            cs = slice(g * SGU_GROUP_CH, (g + 1) * SGU_GROUP_CH)
            s = jnp.dot(ws_ref[g], v[:, cs], preferred_element_type=F32) + bs_ref[g]
            o_ref[0, rs, cs] = (u[:, cs] * s).astype(o_ref.dtype)


def _sgu(h3, ln_g, ln_b, ws, bs, col0, tc=512):
    B, S, _ = h3.shape
    tc = min(tc, S)
    cb = col0 // SGU_WIDTH
    bs_b = jnp.broadcast_to(bs[:, :, None], (SGU_GROUPS, SGU_CHUNK, SGU_GROUP_CH)).astype(F32)
    return pl.pallas_call(
        functools.partial(_sgu_kernel, nchunk=tc // SGU_CHUNK),
        grid=(B, S // tc),
        in_specs=[pl.BlockSpec((1, tc, SGU_WIDTH), lambda b, i: (b, i, cb)),
                  pl.BlockSpec((1, tc, SGU_WIDTH), lambda b, i: (b, i, cb + 1)),
                  pl.BlockSpec((1, SGU_WIDTH), lambda b, i: (0, 0)),
                  pl.BlockSpec((1, SGU_WIDTH), lambda b, i: (0, 0)),
                  pl.BlockSpec((SGU_GROUPS, SGU_CHUNK, SGU_CHUNK), lambda b, i: (0, 0, 0)),
                  pl.BlockSpec((SGU_GROUPS, SGU_CHUNK, SGU_GROUP_CH), lambda b, i: (0, 0, 0))],
        out_specs=pl.BlockSpec((1, tc, SGU_WIDTH), lambda b, i: (b, i, 0)),
        out_shape=jax.ShapeDtypeStruct((B, S, SGU_WIDTH), BF16),
        compiler_params=_cparams(("parallel", "parallel")),
        name="sgu",
    )(h3, h3, ln_g.reshape(1, -1), ln_b.reshape(1, -1), ws.astype(BF16), bs_b)


def _rope_tile(x, c, sa, sb):
    return x * c + pltpu.roll(x, LANES - HEAD_DIM // 2, 1) * sa + pltpu.roll(x, HEAD_DIM // 2, 1) * sb


def _dil_kernel(q_ref, k_ref, v_ref, c_ref, sa_ref, sb_ref, o_ref, lse_ref, kr_ref, *, L, QB, KB, half):
    qi = pl.program_id(2)
    ntile = DIL_GW // LANES
    rchunk = min(L, 256)

    @pl.when(qi == 0)
    def _():
        for c0 in range(0, L, rchunk):
            rs = slice(c0, c0 + rchunk)
            for t in range(ntile):
                cs = slice(t * LANES, (t + 1) * LANES)
                kr_ref[rs, cs] = _rope_tile(k_ref[0, rs, cs].astype(F32), c_ref[rs, cs], sa_ref[rs, cs],
                                            sb_ref[rs, cs]).astype(BF16)

    n0 = pl.multiple_of(qi * QB, QB)
    start = pl.multiple_of(jnp.clip(n0 - half, 0, L - KB), 16)
    qn = n0 + lax.broadcasted_iota(jnp.int32, (QB, KB), 0)
    kn = start + lax.broadcasted_iota(jnp.int32, (QB, KB), 1)
    ok = jnp.abs(kn - qn) <= half
    first = lax.broadcasted_iota(jnp.int32, (QB, LANES), 1) < HEAD_DIM
    scale = HEAD_DIM ** -0.5
    for t in range(ntile):
        cs = slice(t * LANES, (t + 1) * LANES)
        qr = _rope_tile(q_ref[0, :, cs].astype(F32), c_ref[pl.ds(n0, QB), cs], sa_ref[pl.ds(n0, QB), cs],
                        sb_ref[pl.ds(n0, QB), cs]) * scale
        kt = kr_ref[pl.ds(start, KB), cs]
        vt = v_ref[0, pl.ds(start, KB), cs]
        outs, lses = [], []
        for sub in range(2):
            qm = jnp.where(first, qr, 0.0) if sub == 0 else jnp.where(first, 0.0, qr)
            s = jnp.where(ok, _dot_nt(qm.astype(BF16), kt), NEG_BIG)
            m = jnp.max(s, axis=-1, keepdims=True)
            e = jnp.exp(s - m)
            den = jnp.sum(e, axis=-1, keepdims=True)
            outs.append(jnp.dot(e.astype(BF16), vt, preferred_element_type=F32) * (1.0 / den))
            lses.append(jnp.broadcast_to(m + jnp.log(den), (QB, LANES)))
        o_ref[0, :, cs] = jnp.where(first, outs[0], outs[1])
        lse_ref[0, :, cs] = jnp.where(first, lses[0], lses[1])


def _rope_tables(S):
    half = HEAD_DIM // 2
    inv = ROPE_THETA ** (-jnp.arange(half, dtype=F32) / half)
    ang = jnp.arange(S).astype(F32)[:, None] * inv[None, :]
    cos, sin = jnp.cos(ang), jnp.sin(ang)
    z = jnp.zeros_like(sin)
    c = jnp.tile(jnp.concatenate([cos, cos], axis=-1), (1, DIL_HEADS))
    sa = jnp.tile(jnp.concatenate([-sin, z], axis=-1), (1, DIL_HEADS))
    sb = jnp.tile(jnp.concatenate([z, sin], axis=-1), (1, DIL_HEADS))
    return c, sa, sb


def _dil_attention(h3, tabs, g, col0):
    B, S, C = h3.shape
    window, dil = DIL_PATTERNS[g]
    L = S // dil
    half = window // (2 * dil)
    QB = min(L, 256)
    KB = min(L, QB + 2 * half)
    cpb = C // DIL_GW
    qb = col0 // DIL_GW + g
    kb = qb + DIL_WIDTH // DIL_GW
    vb = kb + DIL_WIDTH // DIL_GW
    hv = h3.reshape(B, L, dil * C)
    c, sa, sb = [t.reshape(L, dil * DIL_GW) for t in tabs]
    tab_spec = pl.BlockSpec((L, DIL_GW), lambda b, r, i: (0, r))
    o, lse = pl.pallas_call(
        functools.partial(_dil_kernel, L=L, QB=QB, KB=KB, half=half),
        grid=(B, dil, L // QB),
        in_specs=[pl.BlockSpec((1, QB, DIL_GW), lambda b, r, i: (b, i, r * cpb + qb)),
                  pl.BlockSpec((1, L, DIL_GW), lambda b, r, i: (b, 0, r * cpb + kb)),
                  pl.BlockSpec((1, L, DIL_GW), lambda b, r, i: (b, 0, r * cpb + vb)),
                  tab_spec, tab_spec, tab_spec],
        out_specs=[pl.BlockSpec((1, QB, DIL_GW), lambda b, r, i: (b, i, r)),
                   pl.BlockSpec((1, QB, DIL_GW), lambda b, r, i: (b, i, r))],
        out_shape=[jax.ShapeDtypeStruct((B, L, dil * DIL_GW), F32),
                   jax.ShapeDtypeStruct((B, L, dil * DIL_GW), F32)],
        scratch_shapes=[pltpu.VMEM((L, DIL_GW), BF16)],
        compiler_params=_cparams(("parallel", "parallel", "arbitrary")),
        name=f"dil_attn_g{g}",
    )(hv, hv, hv, c, sa, sb)
    return o.reshape(B * S, DIL_GW), lse.reshape(B * S, DIL_GW)


def _merge_kernel(x_ref, gate_ref, a_ref, b_ref, o0_ref, o1_ref, o2_ref, l0_ref, l1_ref, l2_ref, p_ref,
                  wa_ref, wb_ref, wc_ref, wo_ref, g1_ref, b1_ref, wpg_ref, wp_ref,
                  xb_ref, base_ref, *, alpha, D):
    a = jnp.dot(a_ref[...], wa_ref[...], preferred_element_type=F32)
    b = jnp.dot(b_ref[...], wb_ref[...], preferred_element_type=F32)
    l0, l1, l2 = l0_ref[...], l1_ref[...], l2_ref[...]
    m = jnp.maximum(jnp.maximum(l0, l1), l2)
    e0, e1, e2 = jnp.exp(l0 - m), jnp.exp(l1 - m), jnp.exp(l2 - m)
    cin = (e0 * o0_ref[...] + e1 * o1_ref[...] + e2 * o2_ref[...]) * (1.0 / (e0 + e1 + e2))
    c = jnp.dot(cin.astype(BF16), wc_ref[...], preferred_element_type=F32)
    merged = (_sigmoid(gate_ref[:, 0:D].astype(F32)) * a
              + _sigmoid(gate_ref[:, D:2 * D].astype(F32)) * b
              + _sigmoid(gate_ref[:, 2 * D:3 * D].astype(F32)) * c)
    mix = jnp.dot(merged.astype(BF16), wo_ref[...], preferred_element_type=F32)
    x1 = _ln(alpha * x_ref[...] + mix, g1_ref[...], b1_ref[...])
    x1b = x1.astype(BF16)
    xb_ref[...] = x1b
    ple = (_sigmoid(jnp.dot(x1b, wpg_ref[...], preferred_element_type=F32))
           * jnp.dot(p_ref[...].astype(BF16), wp_ref[...], preferred_element_type=F32))
    base_ref[...] = alpha * x1 + ple


def _merge(x, h, na, sgu, dil_o, dil_l, p, wa, wb, wc, wo, g1, b1, wpg, wp, alpha, tm=256):
    T, D = x.shape
    tm = min(tm, T)
    row = lambda w: pl.BlockSpec((tm, w), lambda i: (i, 0))
    full = lambda arr: pl.BlockSpec(arr.shape, lambda i: (0,) * arr.ndim)
    g1, b1 = g1.reshape(1, D), b1.reshape(1, D)
    return pl.pallas_call(
        functools.partial(_merge_kernel, alpha=alpha, D=D),
        grid=(T // tm,),
        in_specs=[row(D), row(N_BRANCH * D), row(NA_WIDTH), row(SGU_WIDTH),
                  row(DIL_GW), row(DIL_GW), row(DIL_GW), row(DIL_GW), row(DIL_GW), row(DIL_GW),
                  row(p.shape[1]),
                  full(wa), full(wb), full(wc), full(wo), full(g1), full(b1), full(wpg), full(wp)],
        out_specs=[row(D), row(D)],
        out_shape=[jax.ShapeDtypeStruct((T, D), BF16), jax.ShapeDtypeStruct((T, D), F32)],
        compiler_params=_cparams(("parallel",)),
        name="merge",
    )(x, h, na, sgu, dil_o[0], dil_o[1], dil_o[2], dil_l[0], dil_l[1], dil_l[2], p,
      wa, wb, wc, wo, g1, b1, wpg, wp)


_CAND_ROWS = 16 + 7 * 8 + 8


def _cand_index():
    idx = [float(j) for j in range(16)]
    for i in range(1, 8):
        idx += [float(i * 16 + j) for j in range(8)]
    idx += [float(i * 16) for i in range(8, 16)]
    return np.broadcast_to(np.asarray(idx, np.float32)[:, None], (_CAND_ROWS, LANES)).copy()


def _extract_top(s, order, n):
    rank = jnp.full(s.shape, 127.0, F32)
    vals = []
    for i in range(n):
        m = jnp.max(s, axis=0, keepdims=True)
        pick = jnp.min(jnp.where(s == m, order, 1e9), axis=0, keepdims=True)
        sel = order == pick
        rank = jnp.where(sel, float(i), rank)
        s = jnp.where(sel, -jnp.inf, s)
        vals.append(m)
    return rank, vals


def _route_kernel(x_ref, wq_ref, k1_ref, k2_ref, cidx_ref, a1_ref, nx1_ref, e2_ref, x2_ref, s1_ref, s2_ref,
                  *, tb):
    hk = PEER_DKEY // 2
    q = jnp.dot(x_ref[...], wq_ref[...], preferred_element_type=F32).astype(BF16)
    s1_ref[...] = _dot_nt(k1_ref[0], q[:, :hk])
    s2_ref[...] = _dot_nt(k2_ref[0], q[:, hk:])

    def chunk(c, carry):
        ls = pl.ds(pl.multiple_of(c * LANES, LANES), LANES)
        s1 = s1_ref[:, ls]
        s2 = s2_ref[:, ls]
        key_order = lax.broadcasted_iota(jnp.int32, (PEER_NKEYS, LANES), 0).astype(F32)
        rank1, v1 = _extract_top(s1, key_order, PEER_TOPK)
        rank2, v2 = _extract_top(s2, key_order, PEER_TOPK)
        v1a = jnp.concatenate(v1, axis=0)
        v2a = jnp.concatenate(v2, axis=0)
        e1r = jnp.exp(v1a - v1[0])
        e2r = jnp.exp(v2a - v2[0])

        def pairs(r1, r2, op):
            blocks = [op(r1[0:1], r2)]
            blocks += [op(r1[i:i + 1], r2[0:8]) for i in range(1, 8)]
            blocks += [op(r1[8:16], r2[0:1])]
            return jnp.concatenate(blocks, axis=0)

        cand = pairs(v1a, v2a, jnp.add)
        ecand = pairs(e1r, e2r, jnp.multiply)
        crank, _ = _extract_top(cand, cidx_ref[...], PEER_TOPK)
        sel = jnp.where(crank < 100.0, 1.0, 0.0)
        z = jnp.sum(sel * ecand, axis=0, keepdims=True)
        cnt = [jnp.sum(sel[0:16], axis=0, keepdims=True)]
        cnt += [jnp.sum(sel[16 + 8 * (i - 1):16 + 8 * i], axis=0, keepdims=True) for i in range(1, 8)]
        cnt += [sel[72 + i:73 + i] for i in range(8)]
        nx1 = jnp.full(s1.shape, 1e9, F32)
        for i in range(PEER_TOPK):
            nx1 = jnp.where(rank1 == float(i), 0.5 - cnt[i], nx1)
        a1_ref[0, :, ls] = jnp.where(rank1 < 100.0, jnp.exp(s1 - v1[0]), 0.0) * (1.0 / z)
        nx1_ref[0, :, ls] = nx1
        e2_ref[0, :, ls] = jnp.exp(s2 - v2[0])
        x2_ref[0, :, ls] = -rank2
        return carry

    lax.fori_loop(0, tb // LANES, chunk, 0)


def _peer_route(xb, wq, k1, k2, tb):
    T, D = xb.shape
    H = PEER_HEADS
    cidx = jnp.asarray(_cand_index())
    out_spec = pl.BlockSpec((1, PEER_NKEYS, tb), lambda i, h: (h, 0, i))
    out_sds = jax.ShapeDtypeStruct((H, PEER_NKEYS, T), F32)
    return pl.pallas_call(
        functools.partial(_route_kernel, tb=tb),
        grid=(T // tb, H),
        in_specs=[pl.BlockSpec((tb, D), lambda i, h: (i, 0)),
                  pl.BlockSpec((D, PEER_DKEY), lambda i, h: (0, h)),
                  pl.BlockSpec((1, PEER_NKEYS, PEER_DKEY // 2), lambda i, h: (h, 0, 0)),
                  pl.BlockSpec((1, PEER_NKEYS, PEER_DKEY // 2), lambda i, h: (h, 0, 0)),
                  pl.BlockSpec((_CAND_ROWS, LANES), lambda i, h: (0, 0))],
        out_specs=[out_spec] * 4,
        out_shape=[out_sds] * 4,
        scratch_shapes=[pltpu.VMEM((PEER_NKEYS, tb), F32), pltpu.VMEM((PEER_NKEYS, tb), F32)],
        compiler_params=_cparams(("parallel", "arbitrary")),
        name="peer_route",
    )(xb, wq, k1, k2, cidx)


def _dense_kernel(x_ref, u_ref, vt_ref, a1_ref, nx1_ref, e2_ref, x2_ref, base_ref, g_ref, b_ref,
                  yf_ref, yb_ref, acc_ref, *, eb):
    e = pl.program_id(1)

    @pl.when(e == 0)
    def _():
        acc_ref[...] = jnp.zeros_like(acc_ref)

    act = _gelu(_dot_nt(u_ref[...], x_ref[...]))
    parts = []
    for al in range(eb // PEER_NKEYS):
        a = e * (eb // PEER_NKEYS) + al
        w = None
        for h in range(PEER_HEADS):
            hit = jnp.where(x2_ref[h] >= nx1_ref[h, pl.ds(a, 1), :], e2_ref[h], 0.0)
            term = hit * a1_ref[h, pl.ds(a, 1), :]
            w = term if w is None else w + term
        parts.append((w * act[al * PEER_NKEYS:(al + 1) * PEER_NKEYS]).astype(BF16))
    pw = jnp.concatenate(parts, axis=0)
    acc_ref[...] += jnp.dot(vt_ref[...], pw, preferred_element_type=F32)

    @pl.when(e == pl.num_programs(1) - 1)
    def _():
        y = _ln(base_ref[...] + acc_ref[...].T, g_ref[...], b_ref[...])
        yf_ref[...] = y
        yb_ref[...] = y.astype(BF16)


def _peer_dense(xb, u, vt, route, base, g2, b2, tb, eb=512):
    T, D = xb.shape
    NE = u.shape[0]
    H = PEER_HEADS
    rspec = pl.BlockSpec((H, PEER_NKEYS, tb), lambda i, e: (0, 0, i))
    return pl.pallas_call(
        functools.partial(_dense_kernel, eb=eb),
        grid=(T // tb, NE // eb),
        in_specs=[pl.BlockSpec((tb, D), lambda i, e: (i, 0)),
                  pl.BlockSpec((eb, D), lambda i, e: (e, 0)),
                  pl.BlockSpec((D, eb), lambda i, e: (0, e)),
                  rspec, rspec, rspec, rspec,
                  pl.BlockSpec((tb, D), lambda i, e: (i, 0)),
                  pl.BlockSpec((1, D), lambda i, e: (0, 0)),
                  pl.BlockSpec((1, D), lambda i, e: (0, 0))],
        out_specs=[pl.BlockSpec((tb, D), lambda i, e: (i, 0)),
                   pl.BlockSpec((tb, D), lambda i, e: (i, 0))],
        out_shape=[jax.ShapeDtypeStruct((T, D), F32), jax.ShapeDtypeStruct((T, D), BF16)],
        scratch_shapes=[pltpu.VMEM((D, tb), F32)],
        compiler_params=_cparams(("parallel", "arbitrary")),
        name="peer_dense",
    )(xb, u, vt, route[0], route[1], route[2], route[3], base, g2.reshape(1, D), b2.reshape(1, D))


def _layer_weights(i, w_in, rpb, sgu_ln_g, sgu_ln_b, sgu_w, sgu_b, w_br_a, w_br_b, w_br_c, w_out, ln1_g, ln1_b,
                   peer_wq, peer_k1, peer_k2, peer_u, peer_v, ple_w, ple_gate_w, ln2_g, ln2_b):
    D = w_in.shape[1]
    off_c = 3 * NA_WIDTH + 2 * SGU_WIDTH + 3 * DIL_WIDTH
    w_perm = jnp.concatenate([w_in[i][:, off_c:], w_in[i][:, :off_c]], axis=1).astype(BF16)
    return dict(
        w_in=w_perm, col_a=N_BRANCH * D, col_b=N_BRANCH * D + 3 * NA_WIDTH,
        col_c=N_BRANCH * D + 3 * NA_WIDTH + 2 * SGU_WIDTH,
        na_bias=_na_bias_table(rpb[i]),
        sgu_ln_g=sgu_ln_g[i], sgu_ln_b=sgu_ln_b[i], sgu_w=sgu_w[i], sgu_b=sgu_b[i],
        wa=w_br_a[i].astype(BF16), wb=w_br_b[i].astype(BF16), wc=w_br_c[i].astype(BF16),
        wo=w_out[i].astype(BF16), ln1_g=ln1_g[i], ln1_b=ln1_b[i],
        wq=peer_wq[i].astype(BF16), k1=peer_k1[i].astype(BF16), k2=peer_k2[i].astype(BF16),
        u=peer_u[i].astype(BF16), vt=peer_v[i].astype(BF16).T,
        wp=ple_w[i].astype(BF16), wpg=ple_gate_w[i].astype(BF16), ln2_g=ln2_g[i], ln2_b=ln2_b[i])


def _trunk(x, p, ln0_g, ln0_b, layers, alpha):
    B, S, D = x.shape
    T = B * S
    tb = min(512, T)
    tabs = _rope_tables(S)
    xf, xb = _ln0(x.reshape(T, D), ln0_g, ln0_b)
    for i, lw in enumerate(layers):
        h = _inproj(xb, lw["w_in"])
        h3 = h.reshape(B, S, -1)
        na = _na_attention(h3, lw["na_bias"], lw["col_a"]).reshape(T, NA_WIDTH)
        sg = _sgu(h3, lw["sgu_ln_g"], lw["sgu_ln_b"], lw["sgu_w"], lw["sgu_b"], lw["col_b"]).reshape(T, SGU_WIDTH)
        dil = [_dil_attention(h3, tabs, g, lw["col_c"]) for g in range(len(DIL_PATTERNS))]
        x1b, base = _merge(xf, h, na, sg, [d[0] for d in dil], [d[1] for d in dil], p[i].reshape(T, -1),
                           lw["wa"], lw["wb"], lw["wc"], lw["wo"], lw["ln1_g"], lw["ln1_b"],
                           lw["wpg"], lw["wp"], alpha)
        route = _peer_route(x1b, lw["wq"], lw["k1"], lw["k2"], tb)
        xf, xb = _peer_dense(x1b, lw["u"], lw["vt"], route, base, lw["ln2_g"], lw["ln2_b"], tb)
    return xf.reshape(B, S, D)


def kernel(x_prompt, x_sample, p_prompt, p_sample, ln0_g, ln0_b, w_in, rpb, sgu_ln_g, sgu_ln_b, sgu_w, sgu_b, w_br_a, w_br_b, w_br_c, w_out, ln1_g, ln1_b, peer_wq, peer_k1, peer_k2, peer_u, peer_v, ple_w, ple_gate_w, ln2_g, ln2_b):
    depth = w_in.shape[0]
    alpha = (2 * depth) ** 0.25
    layers = [_layer_weights(i, w_in, rpb, sgu_ln_g, sgu_ln_b, sgu_w, sgu_b, w_br_a, w_br_b, w_br_c, w_out,
                             ln1_g, ln1_b, peer_wq, peer_k1, peer_k2, peer_u, peer_v, ple_w, ple_gate_w,
                             ln2_g, ln2_b) for i in range(depth)]
    y_prompt = _trunk(x_prompt, p_prompt, ln0_g, ln0_b, layers, alpha)
    y_sample = _trunk(x_sample, p_sample, ln0_g, ln0_b, layers, alpha)
    return (y_prompt, y_sample)
```

```python
import functools

import numpy as np
import jax
import jax.numpy as jnp
from jax import lax
from jax.experimental import pallas as pl
from jax.experimental.pallas import tpu as pltpu

F32 = jnp.float32
BF16 = jnp.bfloat16

GRID_W = 64
HEAD_DIM = 64
NA_HEADS = 8
NA_WIDTH = NA_HEADS * HEAD_DIM
NA_ROWS = 8
NA_COLS = 16
SGU_GROUPS = 6
SGU_GROUP_CH = 128
SGU_WIDTH = SGU_GROUPS * SGU_GROUP_CH
SGU_CHUNK = 128
DIL_PATTERNS = ((128, 1), (512, 4), (2048, 16))
DIL_HEADS = 4
DIL_GW = DIL_HEADS * HEAD_DIM
DIL_WIDTH = len(DIL_PATTERNS) * DIL_GW
ROPE_THETA = 10000.0
N_BRANCH = 3
PEER_HEADS = 8
PEER_NKEYS = 128
PEER_DKEY = 256
PEER_TOPK = 16
LN_EPS = 1e-5
NEG_BIG = -1e30

LANES = 128
BF16_ROWS = 16
VMEM_LIMIT = 48 * 1024 * 1024


def _cparams(sem):
    return pltpu.CompilerParams(dimension_semantics=sem, vmem_limit_bytes=VMEM_LIMIT)


def _ln(x, g, b):
    mu = jnp.mean(x, axis=-1, keepdims=True)
    xc = x - mu
    var = jnp.mean(xc * xc, axis=-1, keepdims=True)
    return xc * lax.rsqrt(var + LN_EPS) * g + b


def _gelu(x):
    return x * (0.5 * (1.0 + jnp.tanh(0.7978845608028654 * (x + 0.044715 * (x * x * x)))))


def _sigmoid(x):
    return 1.0 / (1.0 + jnp.exp(-x))


def _dot_nt(a, b):
    return lax.dot_general(a, b, (((1,), (1,)), ((), ())), preferred_element_type=F32)


def _ln0_kernel(x_ref, g_ref, b_ref, xf_ref, xb_ref):
    y = _ln(x_ref[...], g_ref[...], b_ref[...])
    xf_ref[...] = y
    xb_ref[...] = y.astype(BF16)


def _ln0(x, g, b, tm=512):
    T, D = x.shape
    return pl.pallas_call(
        _ln0_kernel,
        grid=(T // tm,),
        in_specs=[pl.BlockSpec((tm, D), lambda i: (i, 0)),
                  pl.BlockSpec((1, D), lambda i: (0, 0)),
                  pl.BlockSpec((1, D), lambda i: (0, 0))],
        out_specs=[pl.BlockSpec((tm, D), lambda i: (i, 0)),
                   pl.BlockSpec((tm, D), lambda i: (i, 0))],
        out_shape=[jax.ShapeDtypeStruct((T, D), F32), jax.ShapeDtypeStruct((T, D), BF16)],
        compiler_params=_cparams(("parallel",)),
        name="ln0",
    )(x, g.reshape(1, D), b.reshape(1, D))


def _matmul_kernel(x_ref, w_ref, o_ref):
    o_ref[...] = jnp.dot(x_ref[...], w_ref[...], preferred_element_type=F32).astype(o_ref.dtype)


def _inproj(xb, w, tm=1024, tn=768):
    T, K = xb.shape
    N = w.shape[1]
    tm = min(tm, T)
    return pl.pallas_call(
        _matmul_kernel,
        grid=(T // tm, N // tn),
        in_specs=[pl.BlockSpec((tm, K), lambda i, j: (i, 0)),
                  pl.BlockSpec((K, tn), lambda i, j: (0, j))],
        out_specs=pl.BlockSpec((tm, tn), lambda i, j: (i, j)),
        out_shape=jax.ShapeDtypeStruct((T, N), BF16),
        compiler_params=_cparams(("parallel", "parallel")),
        name="inproj",
    )(xb, w)


def _na_kernel(q_ref, k_ref, v_ref, bias_ref, o_ref, *, rows):
    r = pl.program_id(1)
    rs = jnp.clip(r - NA_ROWS // 2, 0, rows - NA_ROWS)
    start = pl.multiple_of(rs * GRID_W, GRID_W)
    nk = NA_ROWS * GRID_W
    first = lax.broadcasted_iota(jnp.int32, (GRID_W, LANES), 1) < HEAD_DIM
    scale = HEAD_DIM ** -0.5
    for p in range(NA_WIDTH // LANES):
        cols = slice(p * LANES, (p + 1) * LANES)
        qp = q_ref[0, :, cols].astype(F32) * scale
        kp = k_ref[0, pl.ds(start, nk), cols]
        vp = v_ref[0, pl.ds(start, nk), cols]
        q2 = jnp.concatenate([jnp.where(first, qp, 0.0), jnp.where(first, 0.0, qp)], axis=0).astype(BF16)
        s = _dot_nt(q2, kp) + bias_ref[0, p * 2 * GRID_W:(p + 1) * 2 * GRID_W, :]
        m = jnp.max(s, axis=-1, keepdims=True)
        e = jnp.exp(s - m)
        l = jnp.sum(e, axis=-1, keepdims=True)
        o2 = jnp.dot(e.astype(BF16), vp, preferred_element_type=F32) * (1.0 / l)
        o_ref[0, :, cols] = jnp.where(first, o2[:GRID_W], o2[GRID_W:]).astype(o_ref.dtype)


def _na_bias_table(rpb):
    qcol = np.arange(GRID_W)
    kcol = np.arange(GRID_W)
    cstart = np.clip(qcol - NA_COLS // 2, 0, GRID_W - NA_COLS)
    ok = (kcol[None, :] >= cstart[:, None]) & (kcol[None, :] < cstart[:, None] + NA_COLS)
    dc = np.clip(kcol[None, :] - qcol[:, None], -(NA_COLS - 1), NA_COLS - 1) + (NA_COLS - 1)
    b = jnp.where(ok[None, None], rpb[:, :, dc], NEG_BIG)
    cases = []
    for c in range(NA_ROWS):
        dr = np.arange(NA_ROWS) - c + (NA_ROWS - 1)
        t = b[:, dr]
        cases.append(t.transpose(0, 2, 1, 3).reshape(NA_HEADS * GRID_W, NA_ROWS * GRID_W))
    return jnp.stack(cases, axis=0).astype(F32)


def _na_attention(h3, bias_tab, col0):
    B, S, _ = h3.shape
    rows = S // GRID_W
    cb = col0 // NA_WIDTH

    def case_of(r):
        return r - jnp.clip(r - NA_ROWS // 2, 0, rows - NA_ROWS)

    return pl.pallas_call(
        functools.partial(_na_kernel, rows=rows),
        grid=(B, rows),
        in_specs=[pl.BlockSpec((1, GRID_W, NA_WIDTH), lambda b, r: (b, r, cb)),
                  pl.BlockSpec((1, S, NA_WIDTH), lambda b, r: (b, 0, cb + 1)),
                  pl.BlockSpec((1, S, NA_WIDTH), lambda b, r: (b, 0, cb + 2)),
                  pl.BlockSpec((1, NA_HEADS * GRID_W, NA_ROWS * GRID_W), lambda b, r: (case_of(r), 0, 0))],
        out_specs=pl.BlockSpec((1, GRID_W, NA_WIDTH), lambda b, r: (b, r, 0)),
        out_shape=jax.ShapeDtypeStruct((B, S, NA_WIDTH), BF16),
        compiler_params=_cparams(("parallel", "arbitrary")),
        name="na_attn",
    )(h3, h3, h3, bias_tab)


def _sgu_kernel(u_ref, v_ref, g_ref, b_ref, ws_ref, bs_ref, o_ref, *, nchunk):
    for c in range(nchunk):
        rs = slice(c * SGU_CHUNK, (c + 1) * SGU_CHUNK)
        v = _ln(_gelu(v_ref[0, rs, :].astype(F32)), g_ref[...], b_ref[...]).astype(BF16)
        u = _gelu(u_ref[0, rs, :].astype(F32))
        for g in range(SGU_GROUPS):
            cs = slice(g * SGU_GROUP_CH, (g + 1) * SGU_GROUP_CH)
            s = jnp.dot(ws_ref[g], v[:, cs], preferred_element_type=F32) + bs_ref[g]
            o_ref[0, rs, cs] = (u[:, cs] * s).astype(o_ref.dtype)


def _sgu(h3, ln_g, ln_b, ws, bs, col0, tc=512):
    B, S, _ = h3.shape
    tc = min(tc, S)
    cb = col0 // SGU_WIDTH
    bs_b = jnp.broadcast_to(bs[:, :, None], (SGU_GROUPS, SGU_CHUNK, SGU_GROUP_CH)).astype(F32)
    return pl.pallas_call(
        functools.partial(_sgu_kernel, nchunk=tc // SGU_CHUNK),
        grid=(B, S // tc),
        in_specs=[pl.BlockSpec((1, tc, SGU_WIDTH), lambda b, i: (b, i, cb)),
                  pl.BlockSpec((1, tc, SGU_WIDTH), lambda b, i: (b, i, cb + 1)),
                  pl.BlockSpec((1, SGU_WIDTH), lambda b, i: (0, 0)),
                  pl.BlockSpec((1, SGU_WIDTH), lambda b, i: (0, 0)),
                  pl.BlockSpec((SGU_GROUPS, SGU_CHUNK, SGU_CHUNK), lambda b, i: (0, 0, 0)),
                  pl.BlockSpec((SGU_GROUPS, SGU_CHUNK, SGU_GROUP_CH), lambda b, i: (0, 0, 0))],
        out_specs=pl.BlockSpec((1, tc, SGU_WIDTH), lambda b, i: (b, i, 0)),
        out_shape=jax.ShapeDtypeStruct((B, S, SGU_WIDTH), BF16),
        compiler_params=_cparams(("parallel", "parallel")),
        name="sgu",
    )(h3, h3, ln_g.reshape(1, -1), ln_b.reshape(1, -1), ws.astype(BF16), bs_b)


def _rope_tile(x, c, sa, sb):
    return x * c + pltpu.roll(x, LANES - HEAD_DIM // 2, 1) * sa + pltpu.roll(x, HEAD_DIM // 2, 1) * sb


def _dil_kernel(q_ref, k_ref, v_ref, c_ref, sa_ref, sb_ref, o_ref, lse_ref, kr_ref, *, L, QB, KB, half):
    qi = pl.program_id(2)
    ntile = DIL_GW // LANES
    rchunk = min(L, 256)

    @pl.when(qi == 0)
    def _():
        for c0 in range(0, L, rchunk):
            rs = slice(c0, c0 + rchunk)
            for t in range(ntile):
                cs = slice(t * LANES, (t + 1) * LANES)
                kr_ref[rs, cs] = _rope_tile(k_ref[0, rs, cs].astype(F32), c_ref[rs, cs], sa_ref[rs, cs],
                                            sb_ref[rs, cs]).astype(BF16)

    n0 = pl.multiple_of(qi * QB, QB)
    start = pl.multiple_of(jnp.clip(n0 - half, 0, L - KB), 16)
    qn = n0 + lax.broadcasted_iota(jnp.int32, (QB, KB), 0)
    kn = start + lax.broadcasted_iota(jnp.int32, (QB, KB), 1)
    ok = jnp.abs(kn - qn) <= half
    first = lax.broadcasted_iota(jnp.int32, (QB, LANES), 1) < HEAD_DIM
    scale = HEAD_DIM ** -0.5
    for t in range(ntile):
        cs = slice(t * LANES, (t + 1) * LANES)
        qr = _rope_tile(q_ref[0, :, cs].astype(F32), c_ref[pl.ds(n0, QB), cs], sa_ref[pl.ds(n0, QB), cs],
                        sb_ref[pl.ds(n0, QB), cs]) * scale
        kt = kr_ref[pl.ds(start, KB), cs]
        vt = v_ref[0, pl.ds(start, KB), cs]
        outs, lses = [], []
        for sub in range(2):
            qm = jnp.where(first, qr, 0.0) if sub == 0 else jnp.where(first, 0.0, qr)
            s = jnp.where(ok, _dot_nt(qm.astype(BF16), kt), NEG_BIG)
            m = jnp.max(s, axis=-1, keepdims=True)
            e = jnp.exp(s - m)
            den = jnp.sum(e, axis=-1, keepdims=True)
            outs.append(jnp.dot(e.astype(BF16), vt, preferred_element_type=F32) * (1.0 / den))
            lses.append(jnp.broadcast_to(m + jnp.log(den), (QB, LANES)))
        o_ref[0, :, cs] = jnp.where(first, outs[0], outs[1])
        lse_ref[0, :, cs] = jnp.where(first, lses[0], lses[1])


def _rope_tables(S):
    half = HEAD_DIM // 2
    inv = ROPE_THETA ** (-jnp.arange(half, dtype=F32) / half)
    ang = jnp.arange(S).astype(F32)[:, None] * inv[None, :]
    cos, sin = jnp.cos(ang), jnp.sin(ang)
    z = jnp.zeros_like(sin)
    c = jnp.tile(jnp.concatenate([cos, cos], axis=-1), (1, DIL_HEADS))
    sa = jnp.tile(jnp.concatenate([-sin, z], axis=-1), (1, DIL_HEADS))
    sb = jnp.tile(jnp.concatenate([z, sin], axis=-1), (1, DIL_HEADS))
    return c, sa, sb


def _dil_attention(h3, tabs, g, col0):
    B, S, C = h3.shape
    window, dil = DIL_PATTERNS[g]
    L = S // dil
    half = window // (2 * dil)
    QB = min(L, 256)
    KB = min(L, QB + 2 * half)
    gcol = col0 + g * 3 * DIL_GW
    if dil == 1:
        hv, cpb, qb = h3, C // DIL_GW, gcol // DIL_GW
    else:
        hv, cpb, qb = h3[:, :, gcol:gcol + 3 * DIL_GW].reshape(B, L, dil * 3 * DIL_GW), 3, 0
    c, sa, sb = [t.reshape(L, dil * DIL_GW) for t in tabs]
    tab_spec = pl.BlockSpec((L, DIL_GW), lambda b, r, i: (0, r))
    o, lse = pl.pallas_call(
        functools.partial(_dil_kernel, L=L, QB=QB, KB=KB, half=half),
        grid=(B, dil, L // QB),
        in_specs=[pl.BlockSpec((1, QB, DIL_GW), lambda b, r, i: (b, i, r * cpb + qb)),
                  pl.BlockSpec((1, L, DIL_GW), lambda b, r, i: (b, 0, r * cpb + qb + 1)),
                  pl.BlockSpec((1, L, DIL_GW), lambda b, r, i: (b, 0, r * cpb + qb + 2)),
                  tab_spec, tab_spec, tab_spec],
        out_specs=[pl.BlockSpec((1, QB, DIL_GW), lambda b, r, i: (b, i, r)),
                   pl.BlockSpec((1, QB, DIL_GW), lambda b, r, i: (b, i, r))],
        out_shape=[jax.ShapeDtypeStruct((B, L, dil * DIL_GW), F32),
                   jax.ShapeDtypeStruct((B, L, dil * DIL_GW), F32)],
        scratch_shapes=[pltpu.VMEM((L, DIL_GW), BF16)],
        compiler_params=_cparams(("parallel", "parallel", "arbitrary")),
        name=f"dil_attn_g{g}",
    )(hv, hv, hv, c, sa, sb)
    return o.reshape(B * S, DIL_GW), lse.reshape(B * S, DIL_GW)


def _merge_kernel(x_ref, gate_ref, a_ref, b_ref, o0_ref, o1_ref, o2_ref, l0_ref, l1_ref, l2_ref, p_ref,
                  wa_ref, wb_ref, wc_ref, wo_ref, g1_ref, b1_ref, wpg_ref, wp_ref,
                  xb_ref, base_ref, *, alpha, D):
    a = jnp.dot(a_ref[...], wa_ref[...], preferred_element_type=F32)
    b = jnp.dot(b_ref[...], wb_ref[...], preferred_element_type=F32)
    l0, l1, l2 = l0_ref[...], l1_ref[...], l2_ref[...]
    m = jnp.maximum(jnp.maximum(l0, l1), l2)
    e0, e1, e2 = jnp.exp(l0 - m), jnp.exp(l1 - m), jnp.exp(l2 - m)
    cin = (e0 * o0_ref[...] + e1 * o1_ref[...] + e2 * o2_ref[...]) * (1.0 / (e0 + e1 + e2))
    c = jnp.dot(cin.astype(BF16), wc_ref[...], preferred_element_type=F32)
    merged = (_sigmoid(gate_ref[:, 0:D].astype(F32)) * a
              + _sigmoid(gate_ref[:, D:2 * D].astype(F32)) * b
              + _sigmoid(gate_ref[:, 2 * D:3 * D].astype(F32)) * c)
    mix = jnp.dot(merged.astype(BF16), wo_ref[...], preferred_element_type=F32)
    x1 = _ln(alpha * x_ref[...] + mix, g1_ref[...], b1_ref[...])
    x1b = x1.astype(BF16)
    xb_ref[...] = x1b
    ple = (_sigmoid(jnp.dot(x1b, wpg_ref[...], preferred_element_type=F32))
           * jnp.dot(p_ref[...].astype(BF16), wp_ref[...], preferred_element_type=F32))
    base_ref[...] = alpha * x1 + ple


def _merge(x, h, na, sgu, dil_o, dil_l, p, wa, wb, wc, wo, g1, b1, wpg, wp, alpha, tm=256):
    T, D = x.shape
    tm = min(tm, T)
    row = lambda w: pl.BlockSpec((tm, w), lambda i: (i, 0))
    full = lambda arr: pl.BlockSpec(arr.shape, lambda i: (0,) * arr.ndim)
    g1, b1 = g1.reshape(1, D), b1.reshape(1, D)
    return pl.pallas_call(
        functools.partial(_merge_kernel, alpha=alpha, D=D),
        grid=(T // tm,),
        in_specs=[row(D), row(N_BRANCH * D), row(NA_WIDTH), row(SGU_WIDTH),
                  row(DIL_GW), row(DIL_GW), row(DIL_GW), row(DIL_GW), row(DIL_GW), row(DIL_GW),
                  row(p.shape[1]),
                  full(wa), full(wb), full(wc), full(wo), full(g1), full(b1), full(wpg), full(wp)],
        out_specs=[row(D), row(D)],
        out_shape=[jax.ShapeDtypeStruct((T, D), BF16), jax.ShapeDtypeStruct((T, D), F32)],
        compiler_params=_cparams(("parallel",)),
        name="merge",
    )(x, h, na, sgu, dil_o[0], dil_o[1], dil_o[2], dil_l[0], dil_l[1], dil_l[2], p,
      wa, wb, wc, wo, g1, b1, wpg, wp)


_CAND_ROWS = 16 + 7 * 8 + 8


def _cand_index():
    idx = [float(j) for j in range(16)]
    for i in range(1, 8):
        idx += [float(i * 16 + j) for j in range(8)]
    idx += [float(i * 16) for i in range(8, 16)]
    return np.broadcast_to(np.asarray(idx, np.float32)[:, None], (_CAND_ROWS, LANES)).copy()


def _extract_top(s, order, n, tie_safe):
    rank = jnp.full(s.shape, 127.0, F32)
    vals = []
    for i in range(n):
        m = jnp.max(s, axis=0, keepdims=True)
        if tie_safe:
            pick = jnp.min(jnp.where(s == m, order, 1e9), axis=0, keepdims=True)
            sel = order == pick
        else:
            sel = s == m
        rank = jnp.where(sel, float(i), rank)
        s = jnp.where(sel, -jnp.inf, s)
        vals.append(m)
    return rank, vals


def _count_taken(rank):
    return jnp.sum(jnp.where(rank < 100.0, 1.0, 0.0), axis=0, keepdims=True)


def _route_chunk(s1, s2, cidx, tie_safe):
    key_order = lax.broadcasted_iota(jnp.int32, (PEER_NKEYS, LANES), 0).astype(F32)
    rank1, v1 = _extract_top(s1, key_order, PEER_TOPK, tie_safe)
    rank2, v2 = _extract_top(s2, key_order, PEER_TOPK, tie_safe)
    v1a = jnp.concatenate(v1, axis=0)
    v2a = jnp.concatenate(v2, axis=0)
    e1r = jnp.exp(v1a - v1[0])
    e2r = jnp.exp(v2a - v2[0])

    def pairs(r1, r2, op):
        blocks = [op(r1[0:1], r2)]
        blocks += [op(r1[i:i + 1], r2[0:8]) for i in range(1, 8)]
        blocks += [op(r1[8:16], r2[0:1])]
        return jnp.concatenate(blocks, axis=0)

    cand = pairs(v1a, v2a, jnp.add)
    ecand = pairs(e1r, e2r, jnp.multiply)
    crank, _ = _extract_top(cand, cidx, PEER_TOPK, tie_safe)
    sel = jnp.where(crank < 100.0, 1.0, 0.0)
    z = jnp.sum(sel * ecand, axis=0, keepdims=True)
    cnt = [jnp.sum(sel[0:16], axis=0, keepdims=True)]
    cnt += [jnp.sum(sel[16 + 8 * (i - 1):16 + 8 * i], axis=0, keepdims=True) for i in range(1, 8)]
    cnt += [sel[72 + i:73 + i] for i in range(8)]
    nx1 = jnp.full(s1.shape, 1e9, F32)
    for i in range(PEER_TOPK):
        nx1 = jnp.where(rank1 == float(i), 0.5 - cnt[i], nx1)
    a1 = jnp.where(rank1 < 100.0, jnp.exp(s1 - v1[0]), 0.0) * (1.0 / z)
    e2 = jnp.exp(s2 - v2[0])
    excess = (_count_taken(rank1) + _count_taken(rank2) + _count_taken(crank)) - 3.0 * PEER_TOPK
    return a1, nx1, e2, -rank2, excess


def _route_kernel(x_ref, wq_ref, k1_ref, k2_ref, cidx_ref, a1_ref, nx1_ref, e2_ref, x2_ref, s1_ref, s2_ref,
                  *, tb):
    hk = PEER_DKEY // 2
    q = jnp.dot(x_ref[...], wq_ref[...], preferred_element_type=F32).astype(BF16)
    s1_ref[...] = _dot_nt(k1_ref[0], q[:, :hk])
    s2_ref[...] = _dot_nt(k2_ref[0], q[:, hk:])

    def chunk(c, carry):
        ls = pl.ds(pl.multiple_of(c * LANES, LANES), LANES)

        def run(tie_safe):
            a1, nx1, e2, x2, excess = _route_chunk(s1_ref[:, ls], s2_ref[:, ls], cidx_ref[...], tie_safe)
            a1_ref[0, :, ls] = a1
            nx1_ref[0, :, ls] = nx1
            e2_ref[0, :, ls] = e2.astype(e2_ref.dtype)
            x2_ref[0, :, ls] = x2.astype(x2_ref.dtype)
            return excess

        tied = jnp.max(run(False)) > 0.0

        @pl.when(tied)
        def _():
            run(True)

        return carry

    lax.fori_loop(0, tb // LANES, chunk, 0)


def _peer_route(xb, wq, k1, k2, tb):
    T, D = xb.shape
    H = PEER_HEADS
    cidx = jnp.asarray(_cand_index())
    out_spec = pl.BlockSpec((1, PEER_NKEYS, tb), lambda i, h: (h, 0, i))
    out_sds = lambda dt: jax.ShapeDtypeStruct((H, PEER_NKEYS, T), dt)
    return pl.pallas_call(
        functools.partial(_route_kernel, tb=tb),
        grid=(T // tb, H),
        in_specs=[pl.BlockSpec((tb, D), lambda i, h: (i, 0)),
                  pl.BlockSpec((D, PEER_DKEY), lambda i, h: (0, h)),
                  pl.BlockSpec((1, PEER_NKEYS, PEER_DKEY // 2), lambda i, h: (h, 0, 0)),
                  pl.BlockSpec((1, PEER_NKEYS, PEER_DKEY // 2), lambda i, h: (h, 0, 0)),
                  pl.BlockSpec((_CAND_ROWS, LANES), lambda i, h: (0, 0))],
        out_specs=[out_spec] * 4,
        out_shape=[out_sds(F32), out_sds(F32), out_sds(BF16), out_sds(BF16)],
        scratch_shapes=[pltpu.VMEM((PEER_NKEYS, tb), F32), pltpu.VMEM((PEER_NKEYS, tb), F32)],
        compiler_params=_cparams(("parallel", "arbitrary")),
        name="peer_route",
    )(xb, wq, k1, k2, cidx)


def _dense_kernel(x_ref, u_ref, vt_ref, a1_ref, nx1_ref, e2_ref, x2_ref, base_ref, g_ref, b_ref,
                  yf_ref, yb_ref, acc_ref, *, eb):
    e = pl.program_id(1)

    @pl.when(e == 0)
    def _():
        acc_ref[...] = jnp.zeros_like(acc_ref)

    act = _gelu(_dot_nt(u_ref[...], x_ref[...])).astype(BF16)
    tb = act.shape[1]
    zero = jnp.zeros((), BF16)
    parts = []
    for al in range(eb // PEER_NKEYS):
        a = e * (eb // PEER_NKEYS) + al
        nxb = [jnp.broadcast_to(nx1_ref[h, pl.ds(a, 1), :], (BF16_ROWS, tb)).astype(BF16)
               for h in range(PEER_HEADS)]
        a1b = [jnp.broadcast_to(a1_ref[h, pl.ds(a, 1), :], (BF16_ROWS, tb)).astype(BF16)
               for h in range(PEER_HEADS)]
        for k in range(PEER_NKEYS // BF16_ROWS):
            rows = slice(k * BF16_ROWS, (k + 1) * BF16_ROWS)
            w = None
            for h in range(PEER_HEADS):
                term = jnp.where(x2_ref[h, rows, :] >= nxb[h], e2_ref[h, rows, :], zero) * a1b[h]
                w = term if w is None else w + term
            r0 = al * PEER_NKEYS + k * BF16_ROWS
            parts.append(w * act[r0:r0 + BF16_ROWS])
    pw = jnp.concatenate(parts, axis=0)
    acc_ref[...] += jnp.dot(vt_ref[...], pw, preferred_element_type=F32)

    @pl.when(e == pl.num_programs(1) - 1)
    def _():
        y = _ln(base_ref[...] + acc_ref[...].T, g_ref[...], b_ref[...])
        yf_ref[...] = y
        yb_ref[...] = y.astype(BF16)


def _peer_dense(xb, u, vt, route, base, g2, b2, tb, eb=512):
    T, D = xb.shape
    NE = u.shape[0]
    H = PEER_HEADS
    rspec = pl.BlockSpec((H, PEER_NKEYS, tb), lambda i, e: (0, 0, i))
    return pl.pallas_call(
        functools.partial(_dense_kernel, eb=eb),
        grid=(T // tb, NE // eb),
        in_specs=[pl.BlockSpec((tb, D), lambda i, e: (i, 0)),
                  pl.BlockSpec((eb, D), lambda i, e: (e, 0)),
                  pl.BlockSpec((D, eb), lambda i, e: (0, e)),
                  rspec, rspec, rspec, rspec,
                  pl.BlockSpec((tb, D), lambda i, e: (i, 0)),
                  pl.BlockSpec((1, D), lambda i, e: (0, 0)),
                  pl.BlockSpec((1, D), lambda i, e: (0, 0))],
        out_specs=[pl.BlockSpec((tb, D), lambda i, e: (i, 0)),
                   pl.BlockSpec((tb, D), lambda i, e: (i, 0))],
        out_shape=[jax.ShapeDtypeStruct((T, D), F32), jax.ShapeDtypeStruct((T, D), BF16)],
        scratch_shapes=[pltpu.VMEM((D, tb), F32)],
        compiler_params=_cparams(("parallel", "arbitrary")),
        name="peer_dense",
    )(xb, u, vt, route[0], route[1], route[2], route[3], base, g2.reshape(1, D), b2.reshape(1, D))


def _layer_weights(i, w_in, rpb, sgu_ln_g, sgu_ln_b, sgu_w, sgu_b, w_br_a, w_br_b, w_br_c, w_out, ln1_g, ln1_b,
                   peer_wq, peer_k1, peer_k2, peer_u, peer_v, ple_w, ple_gate_w, ln2_g, ln2_b):
    D = w_in.shape[1]
    off_c = 3 * NA_WIDTH + 2 * SGU_WIDTH + 3 * DIL_WIDTH
    off_b = 3 * NA_WIDTH + 2 * SGU_WIDTH
    wc_cols = [w_in[i][:, off_b + part * DIL_WIDTH + g * DIL_GW: off_b + part * DIL_WIDTH + (g + 1) * DIL_GW]
               for g in range(len(DIL_PATTERNS)) for part in range(3)]
    w_perm = jnp.concatenate([w_in[i][:, off_c:], w_in[i][:, :off_b]] + wc_cols, axis=1).astype(BF16)
    return dict(
        w_in=w_perm, col_a=N_BRANCH * D, col_b=N_BRANCH * D + 3 * NA_WIDTH,
        col_c=N_BRANCH * D + 3 * NA_WIDTH + 2 * SGU_WIDTH,
        na_bias=_na_bias_table(rpb[i]),
        sgu_ln_g=sgu_ln_g[i], sgu_ln_b=sgu_ln_b[i], sgu_w=sgu_w[i], sgu_b=sgu_b[i],
        wa=w_br_a[i].astype(BF16), wb=w_br_b[i].astype(BF16), wc=w_br_c[i].astype(BF16),
        wo=w_out[i].astype(BF16), ln1_g=ln1_g[i], ln1_b=ln1_b[i],
        wq=peer_wq[i].astype(BF16), k1=peer_k1[i].astype(BF16), k2=peer_k2[i].astype(BF16),
        u=peer_u[i].astype(BF16), vt=peer_v[i].astype(BF16).T,
        wp=ple_w[i].astype(BF16), wpg=ple_gate_w[i].astype(BF16), ln2_g=ln2_g[i], ln2_b=ln2_b[i])


def _trunk(x, p, ln0_g, ln0_b, layers, alpha):
    B, S, D = x.shape
    T = B * S
    tb = min(512, T)
    tabs = _rope_tables(S)
    xf, xb = _ln0(x.reshape(T, D), ln0_g, ln0_b)
    for i, lw in enumerate(layers):
        h = _inproj(xb, lw["w_in"])
        h3 = h.reshape(B, S, -1)
        na = _na_attention(h3, lw["na_bias"], lw["col_a"]).reshape(T, NA_WIDTH)
        sg = _sgu(h3, lw["sgu_ln_g"], lw["sgu_ln_b"], lw["sgu_w"], lw["sgu_b"], lw["col_b"]).reshape(T, SGU_WIDTH)
        dil = [_dil_attention(h3, tabs, g, lw["col_c"]) for g in range(len(DIL_PATTERNS))]
        x1b, base = _merge(xf, h, na, sg, [d[0] for d in dil], [d[1] for d in dil], p[i].reshape(T, -1),
                           lw["wa"], lw["wb"], lw["wc"], lw["wo"], lw["ln1_g"], lw["ln1_b"],
                           lw["wpg"], lw["wp"], alpha)
        route = _peer_route(x1b, lw["wq"], lw["k1"], lw["k2"], tb)
        xf, xb = _peer_dense(x1b, lw["u"], lw["vt"], route, base, lw["ln2_g"], lw["ln2_b"], tb)
    return xf.reshape(B, S, D)


def kernel(x_prompt, x_sample, p_prompt, p_sample, ln0_g, ln0_b, w_in, rpb, sgu_ln_g, sgu_ln_b, sgu_w, sgu_b, w_br_a, w_br_b, w_br_c, w_out, ln1_g, ln1_b, peer_wq, peer_k1, peer_k2, peer_u, peer_v, ple_w, ple_gate_w, ln2_g, ln2_b):
    depth = w_in.shape[0]
    alpha = (2 * depth) ** 0.25
    layers = [_layer_weights(i, w_in, rpb, sgu_ln_g, sgu_ln_b, sgu_w, sgu_b, w_br_a, w_br_b, w_br_c, w_out,
                             ln1_g, ln1_b, peer_wq, peer_k1, peer_k2, peer_u, peer_v, ple_w, ple_gate_w,
                             ln2_g, ln2_b) for i in range(depth)]
    y_prompt = _trunk(x_prompt, p_prompt, ln0_g, ln0_b, layers, alpha)
    y_sample = _trunk(x_sample, p_sample, ln0_g, ln0_b, layers, alpha)
    return (y_prompt, y_sample)
```

```python
import functools

import numpy as np
import jax
import jax.numpy as jnp
from jax import lax
from jax.experimental import pallas as pl
from jax.experimental.pallas import tpu as pltpu

F32 = jnp.float32
BF16 = jnp.bfloat16

GRID_W = 64
HEAD_DIM = 64
NA_HEADS = 8
NA_WIDTH = NA_HEADS * HEAD_DIM
NA_ROWS = 8
NA_COLS = 16
SGU_GROUPS = 6
SGU_GROUP_CH = 128
SGU_WIDTH = SGU_GROUPS * SGU_GROUP_CH
SGU_CHUNK = 128
DIL_PATTERNS = ((128, 1), (512, 4), (2048, 16))
DIL_HEADS = 4
DIL_GW = DIL_HEADS * HEAD_DIM
DIL_WIDTH = len(DIL_PATTERNS) * DIL_GW
ROPE_THETA = 10000.0
N_BRANCH = 3
PEER_HEADS = 8
PEER_NKEYS = 128
PEER_DKEY = 256
PEER_TOPK = 16
LN_EPS = 1e-5
NEG_BIG = -1e30

LANES = 128
BF16_ROWS = 16
DENSE_SUB = 512
DENSE_LW = 256
ROUTE_LW = 512
VMEM_LIMIT = 48 * 1024 * 1024


def _cparams(sem):
    return pltpu.CompilerParams(dimension_semantics=sem, vmem_limit_bytes=VMEM_LIMIT)


def _ln(x, g, b):
    mu = jnp.mean(x, axis=-1, keepdims=True)
    xc = x - mu
    var = jnp.mean(xc * xc, axis=-1, keepdims=True)
    return xc * lax.rsqrt(var + LN_EPS) * g + b


def _gelu(x):
    return x * (0.5 * (1.0 + jnp.tanh(0.7978845608028654 * (x + 0.044715 * (x * x * x)))))


def _sigmoid(x):
    return 1.0 / (1.0 + jnp.exp(-x))


def _dot_nt(a, b):
    return lax.dot_general(a, b, (((1,), (1,)), ((), ())), preferred_element_type=F32)


def _ln0_kernel(x_ref, g_ref, b_ref, xf_ref, xb_ref):
    y = _ln(x_ref[...], g_ref[...], b_ref[...])
    xf_ref[...] = y
    xb_ref[...] = y.astype(BF16)


def _ln0(x, g, b, tm=512):
    T, D = x.shape
    return pl.pallas_call(
        _ln0_kernel,
        grid=(T // tm,),
        in_specs=[pl.BlockSpec((tm, D), lambda i: (i, 0)),
                  pl.BlockSpec((1, D), lambda i: (0, 0)),
                  pl.BlockSpec((1, D), lambda i: (0, 0))],
        out_specs=[pl.BlockSpec((tm, D), lambda i: (i, 0)),
                   pl.BlockSpec((tm, D), lambda i: (i, 0))],
        out_shape=[jax.ShapeDtypeStruct((T, D), F32), jax.ShapeDtypeStruct((T, D), BF16)],
        compiler_params=_cparams(("parallel",)),
        name="ln0",
    )(x, g.reshape(1, D), b.reshape(1, D))


def _matmul_kernel(x_ref, w_ref, o_ref):
    o_ref[...] = jnp.dot(x_ref[...], w_ref[...], preferred_element_type=F32).astype(o_ref.dtype)


def _inproj(xb, w, tm=1024, tn=768):
    T, K = xb.shape
    N = w.shape[1]
    tm = min(tm, T)
    return pl.pallas_call(
        _matmul_kernel,
        grid=(T // tm, N // tn),
        in_specs=[pl.BlockSpec((tm, K), lambda i, j: (i, 0)),
                  pl.BlockSpec((K, tn), lambda i, j: (0, j))],
        out_specs=pl.BlockSpec((tm, tn), lambda i, j: (i, j)),
        out_shape=jax.ShapeDtypeStruct((T, N), BF16),
        compiler_params=_cparams(("parallel", "parallel")),
        name="inproj",
    )(xb, w)


def _na_kernel(q_ref, k_ref, v_ref, bias_ref, o_ref, *, rows):
    r = pl.program_id(1)
    rs = jnp.clip(r - NA_ROWS // 2, 0, rows - NA_ROWS)
    start = pl.multiple_of(rs * GRID_W, GRID_W)
    nk = NA_ROWS * GRID_W
    first = lax.broadcasted_iota(jnp.int32, (GRID_W, LANES), 1) < HEAD_DIM
    scale = HEAD_DIM ** -0.5
    for p in range(NA_WIDTH // LANES):
        cols = slice(p * LANES, (p + 1) * LANES)
        qp = q_ref[0, :, cols].astype(F32) * scale
        kp = k_ref[0, pl.ds(start, nk), cols]
        vp = v_ref[0, pl.ds(start, nk), cols]
        q2 = jnp.concatenate([jnp.where(first, qp, 0.0), jnp.where(first, 0.0, qp)], axis=0).astype(BF16)
        s = _dot_nt(q2, kp) + bias_ref[0, p * 2 * GRID_W:(p + 1) * 2 * GRID_W, :]
        m = jnp.max(s, axis=-1, keepdims=True)
        e = jnp.exp(s - m)
        l = jnp.sum(e, axis=-1, keepdims=True)
        o2 = jnp.dot(e.astype(BF16), vp, preferred_element_type=F32) * (1.0 / l)
        o_ref[0, :, cols] = jnp.where(first, o2[:GRID_W], o2[GRID_W:]).astype(o_ref.dtype)


def _na_bias_table(rpb):
    qcol = np.arange(GRID_W)
    kcol = np.arange(GRID_W)
    cstart = np.clip(qcol - NA_COLS // 2, 0, GRID_W - NA_COLS)
    ok = (kcol[None, :] >= cstart[:, None]) & (kcol[None, :] < cstart[:, None] + NA_COLS)
    dc = np.clip(kcol[None, :] - qcol[:, None], -(NA_COLS - 1), NA_COLS - 1) + (NA_COLS - 1)
    b = jnp.where(ok[None, None], rpb[:, :, dc], NEG_BIG)
    cases = []
    for c in range(NA_ROWS):
        dr = np.arange(NA_ROWS) - c + (NA_ROWS - 1)
        t = b[:, dr]
        cases.append(t.transpose(0, 2, 1, 3).reshape(NA_HEADS * GRID_W, NA_ROWS * GRID_W))
    return jnp.stack(cases, axis=0).astype(F32)


def _na_attention(h3, bias_tab, col0):
    B, S, _ = h3.shape
    rows = S // GRID_W
    cb = col0 // NA_WIDTH

    def case_of(r):
        return r - jnp.clip(r - NA_ROWS // 2, 0, rows - NA_ROWS)

    return pl.pallas_call(
        functools.partial(_na_kernel, rows=rows),
        grid=(B, rows),
        in_specs=[pl.BlockSpec((1, GRID_W, NA_WIDTH), lambda b, r: (b, r, cb)),
                  pl.BlockSpec((1, S, NA_WIDTH), lambda b, r: (b, 0, cb + 1)),
                  pl.BlockSpec((1, S, NA_WIDTH), lambda b, r: (b, 0, cb + 2)),
                  pl.BlockSpec((1, NA_HEADS * GRID_W, NA_ROWS * GRID_W), lambda b, r: (case_of(r), 0, 0))],
        out_specs=pl.BlockSpec((1, GRID_W, NA_WIDTH), lambda b, r: (b, r, 0)),
        out_shape=jax.ShapeDtypeStruct((B, S, NA_WIDTH), BF16),
        compiler_params=_cparams(("parallel", "arbitrary")),
        name="na_attn",
    )(h3, h3, h3, bias_tab)


def _sgu_kernel(u_ref, v_ref, g_ref, b_ref, ws_ref, bs_ref, o_ref, *, nchunk):
    for c in range(nchunk):
        rs = slice(c * SGU_CHUNK, (c + 1) * SGU_CHUNK)
        v = _ln(_gelu(v_ref[0, rs, :].astype(F32)), g_ref[...], b_ref[...]).astype(BF16)
        u = _gelu(u_ref[0, rs, :].astype(F32))
        for g in range(SGU_GROUPS):
            cs = slice(g * SGU_GROUP_CH, (g + 1) * SGU_GROUP_CH)
            s = jnp.dot(ws_ref[g], v[:, cs], preferred_element_type=F32) + bs_ref[g]
            o_ref[0, rs, cs] = (u[:, cs] * s).astype(o_ref.dtype)


def _sgu(h3, ln_g, ln_b, ws, bs, col0, tc=512):
    B, S, _ = h3.shape
    tc = min(tc, S)
    cb = col0 // SGU_WIDTH
    bs_b = jnp.broadcast_to(bs[:, :, None], (SGU_GROUPS, SGU_CHUNK, SGU_GROUP_CH)).astype(F32)
    return pl.pallas_call(
        functools.partial(_sgu_kernel, nchunk=tc // SGU_CHUNK),
        grid=(B, S // tc),
        in_specs=[pl.BlockSpec((1, tc, SGU_WIDTH), lambda b, i: (b, i, cb)),
                  pl.BlockSpec((1, tc, SGU_WIDTH), lambda b, i: (b, i, cb + 1)),
                  pl.BlockSpec((1, SGU_WIDTH), lambda b, i: (0, 0)),
                  pl.BlockSpec((1, SGU_WIDTH), lambda b, i: (0, 0)),
                  pl.BlockSpec((SGU_GROUPS, SGU_CHUNK, SGU_CHUNK), lambda b, i: (0, 0, 0)),
                  pl.BlockSpec((SGU_GROUPS, SGU_CHUNK, SGU_GROUP_CH), lambda b, i: (0, 0, 0))],
        out_specs=pl.BlockSpec((1, tc, SGU_WIDTH), lambda b, i: (b, i, 0)),
        out_shape=jax.ShapeDtypeStruct((B, S, SGU_WIDTH), BF16),
        compiler_params=_cparams(("parallel", "parallel")),
        name="sgu",
    )(h3, h3, ln_g.reshape(1, -1), ln_b.reshape(1, -1), ws.astype(BF16), bs_b)


def _rope_tile(x, c, sa, sb):
    return x * c + pltpu.roll(x, LANES - HEAD_DIM // 2, 1) * sa + pltpu.roll(x, HEAD_DIM // 2, 1) * sb


def _dil_kernel(q_ref, k_ref, v_ref, c_ref, sa_ref, sb_ref, o_ref, lse_ref, kr_ref, *, L, QB, KB, half):
    qi = pl.program_id(2)
    ntile = DIL_GW // LANES
    rchunk = min(L, 256)

    @pl.when(qi == 0)
    def _():
        for c0 in range(0, L, rchunk):
            rs = slice(c0, c0 + rchunk)
            for t in range(ntile):
                cs = slice(t * LANES, (t + 1) * LANES)
                kr_ref[rs, cs] = _rope_tile(k_ref[0, rs, cs].astype(F32), c_ref[rs, cs], sa_ref[rs, cs],
                                            sb_ref[rs, cs]).astype(BF16)

    n0 = pl.multiple_of(qi * QB, QB)
    start = pl.multiple_of(jnp.clip(n0 - half, 0, L - KB), 16)
    qn = n0 + lax.broadcasted_iota(jnp.int32, (QB, KB), 0)
    kn = start + lax.broadcasted_iota(jnp.int32, (QB, KB), 1)
    ok = jnp.abs(kn - qn) <= half
    first = lax.broadcasted_iota(jnp.int32, (QB, LANES), 1) < HEAD_DIM
    scale = HEAD_DIM ** -0.5
    for t in range(ntile):
        cs = slice(t * LANES, (t + 1) * LANES)
        qr = _rope_tile(q_ref[0, :, cs].astype(F32), c_ref[pl.ds(n0, QB), cs], sa_ref[pl.ds(n0, QB), cs],
                        sb_ref[pl.ds(n0, QB), cs]) * scale
        kt = kr_ref[pl.ds(start, KB), cs]
        vt = v_ref[0, pl.ds(start, KB), cs]
        outs, lses = [], []
        for sub in range(2):
            qm = jnp.where(first, qr, 0.0) if sub == 0 else jnp.where(first, 0.0, qr)
            s = jnp.where(ok, _dot_nt(qm.astype(BF16), kt), NEG_BIG)
            m = jnp.max(s, axis=-1, keepdims=True)
            e = jnp.exp(s - m)
            den = jnp.sum(e, axis=-1, keepdims=True)
            outs.append(jnp.dot(e.astype(BF16), vt, preferred_element_type=F32) * (1.0 / den))
            lses.append(jnp.broadcast_to(m + jnp.log(den), (QB, LANES)))
        o_ref[0, :, cs] = jnp.where(first, outs[0], outs[1])
        lse_ref[0, :, cs] = jnp.where(first, lses[0], lses[1])


def _rope_tables(S):
    half = HEAD_DIM // 2
    inv = ROPE_THETA ** (-jnp.arange(half, dtype=F32) / half)
    ang = jnp.arange(S).astype(F32)[:, None] * inv[None, :]
    cos, sin = jnp.cos(ang), jnp.sin(ang)
    z = jnp.zeros_like(sin)
    c = jnp.tile(jnp.concatenate([cos, cos], axis=-1), (1, DIL_HEADS))
    sa = jnp.tile(jnp.concatenate([-sin, z], axis=-1), (1, DIL_HEADS))
    sb = jnp.tile(jnp.concatenate([z, sin], axis=-1), (1, DIL_HEADS))
    return c, sa, sb


def _dil_attention(h3, tabs, g, col0):
    B, S, C = h3.shape
    window, dil = DIL_PATTERNS[g]
    L = S // dil
    half = window // (2 * dil)
    QB = min(L, 256)
    KB = min(L, QB + 2 * half)
    gcol = col0 + g * 3 * DIL_GW
    if dil == 1:
        hv, cpb, qb = h3, C // DIL_GW, gcol // DIL_GW
    else:
        hv, cpb, qb = h3[:, :, gcol:gcol + 3 * DIL_GW].reshape(B, L, dil * 3 * DIL_GW), 3, 0
    c, sa, sb = [t.reshape(L, dil * DIL_GW) for t in tabs]
    tab_spec = pl.BlockSpec((L, DIL_GW), lambda b, r, i: (0, r))
    o, lse = pl.pallas_call(
        functools.partial(_dil_kernel, L=L, QB=QB, KB=KB, half=half),
        grid=(B, dil, L // QB),
        in_specs=[pl.BlockSpec((1, QB, DIL_GW), lambda b, r, i: (b, i, r * cpb + qb)),
                  pl.BlockSpec((1, L, DIL_GW), lambda b, r, i: (b, 0, r * cpb + qb + 1)),
                  pl.BlockSpec((1, L, DIL_GW), lambda b, r, i: (b, 0, r * cpb + qb + 2)),
                  tab_spec, tab_spec, tab_spec],
        out_specs=[pl.BlockSpec((1, QB, DIL_GW), lambda b, r, i: (b, i, r)),
                   pl.BlockSpec((1, QB, DIL_GW), lambda b, r, i: (b, i, r))],
        out_shape=[jax.ShapeDtypeStruct((B, L, dil * DIL_GW), F32),
                   jax.ShapeDtypeStruct((B, L, dil * DIL_GW), F32)],
        scratch_shapes=[pltpu.VMEM((L, DIL_GW), BF16)],
        compiler_params=_cparams(("parallel", "parallel", "arbitrary")),
        name=f"dil_attn_g{g}",
    )(hv, hv, hv, c, sa, sb)
    return o.reshape(B * S, DIL_GW), lse.reshape(B * S, DIL_GW)


def _merge_kernel(x_ref, gate_ref, a_ref, b_ref, o0_ref, o1_ref, o2_ref, l0_ref, l1_ref, l2_ref, p_ref,
                  wa_ref, wb_ref, wc_ref, wo_ref, g1_ref, b1_ref, wpg_ref, wp_ref,
                  xb_ref, base_ref, *, alpha, D):
    a = jnp.dot(a_ref[...], wa_ref[...], preferred_element_type=F32)
    b = jnp.dot(b_ref[...], wb_ref[...], preferred_element_type=F32)
    l0, l1, l2 = l0_ref[...], l1_ref[...], l2_ref[...]
    m = jnp.maximum(jnp.maximum(l0, l1), l2)
    e0, e1, e2 = jnp.exp(l0 - m), jnp.exp(l1 - m), jnp.exp(l2 - m)
    cin = (e0 * o0_ref[...] + e1 * o1_ref[...] + e2 * o2_ref[...]) * (1.0 / (e0 + e1 + e2))
    c = jnp.dot(cin.astype(BF16), wc_ref[...], preferred_element_type=F32)
    merged = (_sigmoid(gate_ref[:, 0:D].astype(F32)) * a
              + _sigmoid(gate_ref[:, D:2 * D].astype(F32)) * b
              + _sigmoid(gate_ref[:, 2 * D:3 * D].astype(F32)) * c)
    mix = jnp.dot(merged.astype(BF16), wo_ref[...], preferred_element_type=F32)
    x1 = _ln(alpha * x_ref[...] + mix, g1_ref[...], b1_ref[...])
    x1b = x1.astype(BF16)
    xb_ref[...] = x1b
    ple = (_sigmoid(jnp.dot(x1b, wpg_ref[...], preferred_element_type=F32))
           * jnp.dot(p_ref[...].astype(BF16), wp_ref[...], preferred_element_type=F32))
    base_ref[...] = alpha * x1 + ple


def _merge(x, h, na, sgu, dil_o, dil_l, p, wa, wb, wc, wo, g1, b1, wpg, wp, alpha, tm=256):
    T, D = x.shape
    tm = min(tm, T)
    row = lambda w: pl.BlockSpec((tm, w), lambda i: (i, 0))
    full = lambda arr: pl.BlockSpec(arr.shape, lambda i: (0,) * arr.ndim)
    g1, b1 = g1.reshape(1, D), b1.reshape(1, D)
    return pl.pallas_call(
        functools.partial(_merge_kernel, alpha=alpha, D=D),
        grid=(T // tm,),
        in_specs=[row(D), row(N_BRANCH * D), row(NA_WIDTH), row(SGU_WIDTH),
                  row(DIL_GW), row(DIL_GW), row(DIL_GW), row(DIL_GW), row(DIL_GW), row(DIL_GW),
                  row(p.shape[1]),
                  full(wa), full(wb), full(wc), full(wo), full(g1), full(b1), full(wpg), full(wp)],
        out_specs=[row(D), row(D)],
        out_shape=[jax.ShapeDtypeStruct((T, D), BF16), jax.ShapeDtypeStruct((T, D), F32)],
        compiler_params=_cparams(("parallel",)),
        name="merge",
    )(x, h, na, sgu, dil_o[0], dil_o[1], dil_o[2], dil_l[0], dil_l[1], dil_l[2], p,
      wa, wb, wc, wo, g1, b1, wpg, wp)


_CAND_ROWS = 16 + 7 * 8 + 8


def _cand_index():
    idx = [float(j) for j in range(16)]
    for i in range(1, 8):
        idx += [float(i * 16 + j) for j in range(8)]
    idx += [float(i * 16) for i in range(8, 16)]
    return np.broadcast_to(np.asarray(idx, np.float32)[:, None], (_CAND_ROWS, ROUTE_LW)).copy()


def _extract_top(s, order, n, tie_safe):
    rank = jnp.full(s.shape, 127.0, F32)
    vals = []
    for i in range(n):
        m = jnp.max(s, axis=0, keepdims=True)
        if tie_safe:
            pick = jnp.min(jnp.where(s == m, order, 1e9), axis=0, keepdims=True)
            sel = order == pick
        else:
            sel = s == m
        rank = jnp.where(sel, float(i), rank)
        s = jnp.where(sel, -jnp.inf, s)
        vals.append(m)
    return rank, vals


def _row_tile(ref, h, a, ls):
    row = ref[h, a:a + 1, ls]
    return jnp.broadcast_to(row, (BF16_ROWS, row.shape[1])).astype(BF16)


def _count_taken(rank):
    return jnp.sum(jnp.where(rank < 100.0, 1.0, 0.0), axis=0, keepdims=True)


def _route_chunk(s1, s2, cidx, tie_safe):
    key_order = lax.broadcasted_iota(jnp.int32, s1.shape, 0).astype(F32)
    rank1, v1 = _extract_top(s1, key_order, PEER_TOPK, tie_safe)
    rank2, v2 = _extract_top(s2, key_order, PEER_TOPK, tie_safe)
    v1a = jnp.concatenate(v1, axis=0)
    v2a = jnp.concatenate(v2, axis=0)
    e1r = jnp.exp(v1a - v1[0])
    e2r = jnp.exp(v2a - v2[0])

    def pairs(r1, r2, op):
        blocks = [op(r1[0:1], r2)]
        blocks += [op(r1[i:i + 1], r2[0:8]) for i in range(1, 8)]
        blocks += [op(r1[8:16], r2[0:1])]
        return jnp.concatenate(blocks, axis=0)

    cand = pairs(v1a, v2a, jnp.add)
    ecand = pairs(e1r, e2r, jnp.multiply)
    crank, _ = _extract_top(cand, cidx, PEER_TOPK, tie_safe)
    sel = jnp.where(crank < 100.0, 1.0, 0.0)
    z = jnp.sum(sel * ecand, axis=0, keepdims=True)
    cnt = [jnp.sum(sel[0:16], axis=0, keepdims=True)]
    cnt += [jnp.sum(sel[16 + 8 * (i - 1):16 + 8 * i], axis=0, keepdims=True) for i in range(1, 8)]
    cnt += [sel[72 + i:73 + i] for i in range(8)]
    nx1 = jnp.full(s1.shape, 1e9, F32)
    for i in range(PEER_TOPK):
        nx1 = jnp.where(rank1 == float(i), 0.5 - cnt[i], nx1)
    a1 = jnp.where(rank1 < 100.0, jnp.exp(s1 - v1[0]), 0.0) * (1.0 / z)
    e2 = jnp.exp(s2 - v2[0])
    excess = (_count_taken(rank1) + _count_taken(rank2) + _count_taken(crank)) - 3.0 * PEER_TOPK
    return a1, nx1, e2, -rank2, excess


def _route_kernel(x_ref, wq_ref, k1_ref, k2_ref, cidx_ref, a1_ref, nx1_ref, e2_ref, x2_ref, s1_ref, s2_ref,
                  *, tb):
    hk = PEER_DKEY // 2
    q = jnp.dot(x_ref[...], wq_ref[...], preferred_element_type=F32).astype(BF16)
    s1_ref[...] = _dot_nt(k1_ref[0], q[:, :hk])
    s2_ref[...] = _dot_nt(k2_ref[0], q[:, hk:])

    def chunk(c, carry):
        ls = pl.ds(pl.multiple_of(c * ROUTE_LW, ROUTE_LW), ROUTE_LW)

        def run(tie_safe):
            a1, nx1, e2, x2, excess = _route_chunk(s1_ref[:, ls], s2_ref[:, ls], cidx_ref[...], tie_safe)
            a1_ref[0, :, ls] = a1
            nx1_ref[0, :, ls] = nx1
            e2_ref[0, :, ls] = e2.astype(e2_ref.dtype)
            x2_ref[0, :, ls] = x2.astype(x2_ref.dtype)
            return excess

        tied = jnp.max(run(False)) > 0.0

        @pl.when(tied)
        def _():
            run(True)

        return carry

    lax.fori_loop(0, tb // ROUTE_LW, chunk, 0)


def _peer_route(xb, wq, k1, k2, tb):
    T, D = xb.shape
    H = PEER_HEADS
    cidx = jnp.asarray(_cand_index())
    out_spec = pl.BlockSpec((1, PEER_NKEYS, tb), lambda i, h: (h, 0, i))
    out_sds = lambda dt: jax.ShapeDtypeStruct((H, PEER_NKEYS, T), dt)
    return pl.pallas_call(
        functools.partial(_route_kernel, tb=tb),
        grid=(T // tb, H),
        in_specs=[pl.BlockSpec((tb, D), lambda i, h: (i, 0)),
                  pl.BlockSpec((D, PEER_DKEY), lambda i, h: (0, h)),
                  pl.BlockSpec((1, PEER_NKEYS, PEER_DKEY // 2), lambda i, h: (h, 0, 0)),
                  pl.BlockSpec((1, PEER_NKEYS, PEER_DKEY // 2), lambda i, h: (h, 0, 0)),
                  pl.BlockSpec((_CAND_ROWS, ROUTE_LW), lambda i, h: (0, 0))],
        out_specs=[out_spec] * 4,
        out_shape=[out_sds(F32), out_sds(F32), out_sds(BF16), out_sds(BF16)],
        scratch_shapes=[pltpu.VMEM((PEER_NKEYS, tb), F32), pltpu.VMEM((PEER_NKEYS, tb), F32)],
        compiler_params=_cparams(("parallel", "arbitrary")),
        name="peer_route",
    )(xb, wq, k1, k2, cidx)


def _dense_kernel(x_ref, u_ref, vt_ref, a1_ref, nx1_ref, e2_ref, x2_ref, base_ref, g_ref, b_ref,
                  yf_ref, yb_ref, acc_ref, act_ref, pw_ref, *, eb):
    e = pl.program_id(1)
    last = pl.num_programs(1) - 1
    tb = x_ref.shape[0]
    zero = jnp.zeros((), BF16)

    for sb in range(eb // DENSE_SUB):
        rows_e = slice(sb * DENSE_SUB, (sb + 1) * DENSE_SUB)
        act_ref[rows_e, :] = _gelu(_dot_nt(u_ref[rows_e, :], x_ref[...]).astype(BF16))
    nk = PEER_NKEYS // BF16_ROWS

    for al in range(eb // PEER_NKEYS):
        for lt in range(tb // DENSE_LW):
            ls = slice(lt * DENSE_LW, (lt + 1) * DENSE_LW)
            w = [None] * nk
            for h in range(PEER_HEADS):
                nxb = _row_tile(nx1_ref, h, al, ls)
                a1b = _row_tile(a1_ref, h, al, ls)
                for k in range(nk):
                    rows = slice(k * BF16_ROWS, (k + 1) * BF16_ROWS)
                    term = jnp.where(x2_ref[h, rows, ls] >= nxb, e2_ref[h, rows, ls], zero) * a1b
                    w[k] = term if w[k] is None else w[k] + term
            for k in range(nk):
                r0 = al * PEER_NKEYS + k * BF16_ROWS
                pw_ref[r0:r0 + BF16_ROWS, ls] = w[k] * act_ref[r0:r0 + BF16_ROWS, ls]

    d = jnp.dot(vt_ref[...], pw_ref[...], preferred_element_type=F32)

    @pl.when(e == 0)
    def _():
        acc_ref[...] = d

    @pl.when(e > 0)
    def _():
        acc_ref[...] += d

    @pl.when(e == last)
    def _():
        y = _ln(base_ref[...] + acc_ref[...].T, g_ref[...], b_ref[...])
        yf_ref[...] = y
        yb_ref[...] = y.astype(BF16)


def _peer_dense(xb, u, vt, route, base, g2, b2, tb, eb=1024):
    T, D = xb.shape
    NE = u.shape[0]
    H = PEER_HEADS
    rspec = pl.BlockSpec((H, PEER_NKEYS, tb), lambda i, e: (0, 0, i))
    aspec = pl.BlockSpec((H, eb // PEER_NKEYS, tb), lambda i, e: (0, e, i))
    return pl.pallas_call(
        functools.partial(_dense_kernel, eb=eb),
        grid=(T // tb, NE // eb),
        in_specs=[pl.BlockSpec((tb, D), lambda i, e: (i, 0)),
                  pl.BlockSpec((eb, D), lambda i, e: (e, 0)),
                  pl.BlockSpec((D, eb), lambda i, e: (0, e)),
                  aspec, aspec, rspec, rspec,
                  pl.BlockSpec((tb, D), lambda i, e: (i, 0)),
                  pl.BlockSpec((1, D), lambda i, e: (0, 0)),
                  pl.BlockSpec((1, D), lambda i, e: (0, 0))],
        out_specs=[pl.BlockSpec((tb, D), lambda i, e: (i, 0)),
                   pl.BlockSpec((tb, D), lambda i, e: (i, 0))],
        out_shape=[jax.ShapeDtypeStruct((T, D), F32), jax.ShapeDtypeStruct((T, D), BF16)],
        scratch_shapes=[pltpu.VMEM((D, tb), F32), pltpu.VMEM((eb, tb), BF16), pltpu.VMEM((eb, tb), BF16)],
        compiler_params=_cparams(("parallel", "arbitrary")),
        name="peer_dense",
    )(xb, u, vt, route[0], route[1], route[2], route[3], base, g2.reshape(1, D), b2.reshape(1, D))


def _layer_weights(i, w_in, rpb, sgu_ln_g, sgu_ln_b, sgu_w, sgu_b, w_br_a, w_br_b, w_br_c, w_out, ln1_g, ln1_b,
                   peer_wq, peer_k1, peer_k2, peer_u, peer_v, ple_w, ple_gate_w, ln2_g, ln2_b):
    D = w_in.shape[1]
    off_c = 3 * NA_WIDTH + 2 * SGU_WIDTH + 3 * DIL_WIDTH
    off_b = 3 * NA_WIDTH + 2 * SGU_WIDTH
    wc_cols = [w_in[i][:, off_b + part * DIL_WIDTH + g * DIL_GW: off_b + part * DIL_WIDTH + (g + 1) * DIL_GW]
               for g in range(len(DIL_PATTERNS)) for part in range(3)]
    w_perm = jnp.concatenate([w_in[i][:, off_c:], w_in[i][:, :off_b]] + wc_cols, axis=1).astype(BF16)
    return dict(
        w_in=w_perm, col_a=N_BRANCH * D, col_b=N_BRANCH * D + 3 * NA_WIDTH,
        col_c=N_BRANCH * D + 3 * NA_WIDTH + 2 * SGU_WIDTH,
        na_bias=_na_bias_table(rpb[i]),
        sgu_ln_g=sgu_ln_g[i], sgu_ln_b=sgu_ln_b[i], sgu_w=sgu_w[i], sgu_b=sgu_b[i],
        wa=w_br_a[i].astype(BF16), wb=w_br_b[i].astype(BF16), wc=w_br_c[i].astype(BF16),
        wo=w_out[i].astype(BF16), ln1_g=ln1_g[i], ln1_b=ln1_b[i],
        wq=peer_wq[i].astype(BF16), k1=peer_k1[i].astype(BF16), k2=peer_k2[i].astype(BF16),
        u=peer_u[i].astype(BF16), vt=peer_v[i].astype(BF16).T,
        wp=ple_w[i].astype(BF16), wpg=ple_gate_w[i].astype(BF16), ln2_g=ln2_g[i], ln2_b=ln2_b[i])


def _trunk(x, p, ln0_g, ln0_b, layers, alpha):
    B, S, D = x.shape
    T = B * S
    tb = min(512, T)
    tabs = _rope_tables(S)
    xf, xb = _ln0(x.reshape(T, D), ln0_g, ln0_b)
    for i, lw in enumerate(layers):
        h = _inproj(xb, lw["w_in"])
        h3 = h.reshape(B, S, -1)
        na = _na_attention(h3, lw["na_bias"], lw["col_a"]).reshape(T, NA_WIDTH)
        sg = _sgu(h3, lw["sgu_ln_g"], lw["sgu_ln_b"], lw["sgu_w"], lw["sgu_b"], lw["col_b"]).reshape(T, SGU_WIDTH)
        dil = [_dil_attention(h3, tabs, g, lw["col_c"]) for g in range(len(DIL_PATTERNS))]
        x1b, base = _merge(xf, h, na, sg, [d[0] for d in dil], [d[1] for d in dil], p[i].reshape(T, -1),
                           lw["wa"], lw["wb"], lw["wc"], lw["wo"], lw["ln1_g"], lw["ln1_b"],
                           lw["wpg"], lw["wp"], alpha)
        route = _peer_route(x1b, lw["wq"], lw["k1"], lw["k2"], tb)
        xf, xb = _peer_dense(x1b, lw["u"], lw["vt"], route, base, lw["ln2_g"], lw["ln2_b"], tb)
    return xf.reshape(B, S, D)


def kernel(x_prompt, x_sample, p_prompt, p_sample, ln0_g, ln0_b, w_in, rpb, sgu_ln_g, sgu_ln_b, sgu_w, sgu_b, w_br_a, w_br_b, w_br_c, w_out, ln1_g, ln1_b, peer_wq, peer_k1, peer_k2, peer_u, peer_v, ple_w, ple_gate_w, ln2_g, ln2_b):
    depth = w_in.shape[0]
    alpha = (2 * depth) ** 0.25
    layers = [_layer_weights(i, w_in, rpb, sgu_ln_g, sgu_ln_b, sgu_w, sgu_b, w_br_a, w_br_b, w_br_c, w_out,
                             ln1_g, ln1_b, peer_wq, peer_k1, peer_k2, peer_u, peer_v, ple_w, ple_gate_w,
                             ln2_g, ln2_b) for i in range(depth)]
    y_prompt = _trunk(x_prompt, p_prompt, ln0_g, ln0_b, layers, alpha)
    y_sample = _trunk(x_sample, p_sample, ln0_g, ln0_b, layers, alpha)
    return (y_prompt, y_sample)
```

```python
import functools

import numpy as np
import jax
import jax.numpy as jnp
from jax import lax
from jax.experimental import pallas as pl
from jax.experimental.pallas import tpu as pltpu

F32 = jnp.float32
BF16 = jnp.bfloat16

GRID_W = 64
HEAD_DIM = 64
NA_HEADS = 8
NA_WIDTH = NA_HEADS * HEAD_DIM
NA_ROWS = 8
NA_COLS = 16
SGU_GROUPS = 6
SGU_GROUP_CH = 128
SGU_WIDTH = SGU_GROUPS * SGU_GROUP_CH
SGU_CHUNK = 128
DIL_PATTERNS = ((128, 1), (512, 4), (2048, 16))
DIL_HEADS = 4
DIL_GW = DIL_HEADS * HEAD_DIM
DIL_WIDTH = len(DIL_PATTERNS) * DIL_GW
ROPE_THETA = 10000.0
N_BRANCH = 3
PEER_HEADS = 8
PEER_NKEYS = 128
PEER_DKEY = 256
PEER_TOPK = 16
LN_EPS = 1e-5
NEG_BIG = -1e30

LANES = 128
BF16_ROWS = 16
DENSE_SUB = 512
DENSE_LW = 256
NA_ROWS_PER_STEP = 4
ROUTE_LW = 512
VMEM_LIMIT = 48 * 1024 * 1024


def _cparams(sem):
    return pltpu.CompilerParams(dimension_semantics=sem, vmem_limit_bytes=VMEM_LIMIT)


def _ln(x, g, b):
    mu = jnp.mean(x, axis=-1, keepdims=True)
    xc = x - mu
    var = jnp.mean(xc * xc, axis=-1, keepdims=True)
    return xc * lax.rsqrt(var + LN_EPS) * g + b


def _gelu(x):
    return x * (0.5 * (1.0 + jnp.tanh(0.7978845608028654 * (x + 0.044715 * (x * x * x)))))


def _sigmoid(x):
    return 1.0 / (1.0 + jnp.exp(-x))


def _dot_nt(a, b):
    return lax.dot_general(a, b, (((1,), (1,)), ((), ())), preferred_element_type=F32)


def _ln0_kernel(x_ref, g_ref, b_ref, xf_ref, xb_ref):
    y = _ln(x_ref[...], g_ref[...], b_ref[...])
    xf_ref[...] = y
    xb_ref[...] = y.astype(BF16)


def _ln0(x, g, b, tm=512):
    T, D = x.shape
    return pl.pallas_call(
        _ln0_kernel,
        grid=(T // tm,),
        in_specs=[pl.BlockSpec((tm, D), lambda i: (i, 0)),
                  pl.BlockSpec((1, D), lambda i: (0, 0)),
                  pl.BlockSpec((1, D), lambda i: (0, 0))],
        out_specs=[pl.BlockSpec((tm, D), lambda i: (i, 0)),
                   pl.BlockSpec((tm, D), lambda i: (i, 0))],
        out_shape=[jax.ShapeDtypeStruct((T, D), F32), jax.ShapeDtypeStruct((T, D), BF16)],
        compiler_params=_cparams(("parallel",)),
        name="ln0",
    )(x, g.reshape(1, D), b.reshape(1, D))


def _matmul_kernel(x_ref, w_ref, o_ref):
    o_ref[...] = jnp.dot(x_ref[...], w_ref[...], preferred_element_type=F32).astype(o_ref.dtype)


def _inproj(xb, w, tm=1024, tn=768):
    T, K = xb.shape
    N = w.shape[1]
    tm = min(tm, T)
    return pl.pallas_call(
        _matmul_kernel,
        grid=(T // tm, N // tn),
        in_specs=[pl.BlockSpec((tm, K), lambda i, j: (i, 0)),
                  pl.BlockSpec((K, tn), lambda i, j: (0, j))],
        out_specs=pl.BlockSpec((tm, tn), lambda i, j: (i, j)),
        out_shape=jax.ShapeDtypeStruct((T, N), BF16),
        compiler_params=_cparams(("parallel", "parallel")),
        name="inproj",
    )(xb, w)


def _na_kernel(q_ref, k_ref, v_ref, bias_ref, o_ref, *, rows, rb):
    nk = NA_ROWS * GRID_W
    first = lax.broadcasted_iota(jnp.int32, (GRID_W, LANES), 1) < HEAD_DIM
    scale = HEAD_DIM ** -0.5
    for rr in range(rb):
        r = pl.program_id(1) * rb + rr
        rs = jnp.clip(r - NA_ROWS // 2, 0, rows - NA_ROWS)
        case = r - rs
        start = pl.multiple_of(rs * GRID_W, GRID_W)
        qrows = slice(rr * GRID_W, (rr + 1) * GRID_W)
        for p in range(NA_WIDTH // LANES):
            cols = slice(p * LANES, (p + 1) * LANES)
            qp = q_ref[0, qrows, cols].astype(F32) * scale
            kp = k_ref[0, pl.ds(start, nk), cols]
            vp = v_ref[0, pl.ds(start, nk), cols]
            q2 = jnp.concatenate([jnp.where(first, qp, 0.0), jnp.where(first, 0.0, qp)], axis=0).astype(BF16)
            s = _dot_nt(q2, kp) + bias_ref[case, p * 2 * GRID_W:(p + 1) * 2 * GRID_W, :]
            m = jnp.max(s, axis=-1, keepdims=True)
            e = jnp.exp(s - m)
            l = jnp.sum(e, axis=-1, keepdims=True)
            o2 = jnp.dot(e.astype(BF16), vp, preferred_element_type=F32) * (1.0 / l)
            o_ref[0, qrows, cols] = jnp.where(first, o2[:GRID_W], o2[GRID_W:]).astype(o_ref.dtype)


def _na_bias_table(rpb):
    qcol = np.arange(GRID_W)
    kcol = np.arange(GRID_W)
    cstart = np.clip(qcol - NA_COLS // 2, 0, GRID_W - NA_COLS)
    ok = (kcol[None, :] >= cstart[:, None]) & (kcol[None, :] < cstart[:, None] + NA_COLS)
    dc = np.clip(kcol[None, :] - qcol[:, None], -(NA_COLS - 1), NA_COLS - 1) + (NA_COLS - 1)
    b = jnp.where(ok[None, None], rpb[:, :, dc], NEG_BIG)
    cases = []
    for c in range(NA_ROWS):
        dr = np.arange(NA_ROWS) - c + (NA_ROWS - 1)
        t = b[:, dr]
        cases.append(t.transpose(0, 2, 1, 3).reshape(NA_HEADS * GRID_W, NA_ROWS * GRID_W))
    return jnp.stack(cases, axis=0).astype(F32)


def _na_attention(h3, bias_tab, col0):
    B, S, _ = h3.shape
    rows = S // GRID_W
    cb = col0 // NA_WIDTH
    rb = NA_ROWS_PER_STEP
    return pl.pallas_call(
        functools.partial(_na_kernel, rows=rows, rb=rb),
        grid=(B, rows // rb),
        in_specs=[pl.BlockSpec((1, rb * GRID_W, NA_WIDTH), lambda b, r: (b, r, cb)),
                  pl.BlockSpec((1, S, NA_WIDTH), lambda b, r: (b, 0, cb + 1)),
                  pl.BlockSpec((1, S, NA_WIDTH), lambda b, r: (b, 0, cb + 2)),
                  pl.BlockSpec(bias_tab.shape, lambda b, r: (0, 0, 0))],
        out_specs=pl.BlockSpec((1, rb * GRID_W, NA_WIDTH), lambda b, r: (b, r, 0)),
        out_shape=jax.ShapeDtypeStruct((B, S, NA_WIDTH), BF16),
        compiler_params=_cparams(("parallel", "arbitrary")),
        name="na_attn",
    )(h3, h3, h3, bias_tab)


def _sgu_kernel(u_ref, v_ref, g_ref, b_ref, ws_ref, bs_ref, o_ref, *, nchunk):
    for c in range(nchunk):
        rs = slice(c * SGU_CHUNK, (c + 1) * SGU_CHUNK)
        v = _ln(_gelu(v_ref[0, rs, :].astype(F32)), g_ref[...], b_ref[...]).astype(BF16)
        u = _gelu(u_ref[0, rs, :].astype(F32))
        for g in range(SGU_GROUPS):
            cs = slice(g * SGU_GROUP_CH, (g + 1) * SGU_GROUP_CH)
            s = jnp.dot(ws_ref[g], v[:, cs], preferred_element_type=F32) + bs_ref[g]
            o_ref[0, rs, cs] = (u[:, cs] * s).astype(o_ref.dtype)


def _sgu(h3, ln_g, ln_b, ws, bs, col0, tc=512):
    B, S, _ = h3.shape
    tc = min(tc, S)
    cb = col0 // SGU_WIDTH
    bs_b = jnp.broadcast_to(bs[:, :, None], (SGU_GROUPS, SGU_CHUNK, SGU_GROUP_CH)).astype(F32)
    return pl.pallas_call(
        functools.partial(_sgu_kernel, nchunk=tc // SGU_CHUNK),
        grid=(B, S // tc),
        in_specs=[pl.BlockSpec((1, tc, SGU_WIDTH), lambda b, i: (b, i, cb)),
                  pl.BlockSpec((1, tc, SGU_WIDTH), lambda b, i: (b, i, cb + 1)),
                  pl.BlockSpec((1, SGU_WIDTH), lambda b, i: (0, 0)),
                  pl.BlockSpec((1, SGU_WIDTH), lambda b, i: (0, 0)),
                  pl.BlockSpec((SGU_GROUPS, SGU_CHUNK, SGU_CHUNK), lambda b, i: (0, 0, 0)),
                  pl.BlockSpec((SGU_GROUPS, SGU_CHUNK, SGU_GROUP_CH), lambda b, i: (0, 0, 0))],
        out_specs=pl.BlockSpec((1, tc, SGU_WIDTH), lambda b, i: (b, i, 0)),
        out_shape=jax.ShapeDtypeStruct((B, S, SGU_WIDTH), BF16),
        compiler_params=_cparams(("parallel", "parallel")),
        name="sgu",
    )(h3, h3, ln_g.reshape(1, -1), ln_b.reshape(1, -1), ws.astype(BF16), bs_b)


def _rope_tile(x, c, sa, sb):
    return x * c + pltpu.roll(x, LANES - HEAD_DIM // 2, 1) * sa + pltpu.roll(x, HEAD_DIM // 2, 1) * sb


def _dil_kernel(q_ref, k_ref, v_ref, c_ref, sa_ref, sb_ref, o_ref, lse_ref, kr_ref, *, L, QB, KB, half):
    qi = pl.program_id(2)
    ntile = DIL_GW // LANES
    rchunk = min(L, 256)

    @pl.when(qi == 0)
    def _():
        for c0 in range(0, L, rchunk):
            rs = slice(c0, c0 + rchunk)
            for t in range(ntile):
                cs = slice(t * LANES, (t + 1) * LANES)
                kr_ref[rs, cs] = _rope_tile(k_ref[0, rs, cs].astype(F32), c_ref[rs, cs], sa_ref[rs, cs],
                                            sb_ref[rs, cs]).astype(BF16)

    n0 = pl.multiple_of(qi * QB, QB)
    start = pl.multiple_of(jnp.clip(n0 - half, 0, L - KB), 16)
    qn = n0 + lax.broadcasted_iota(jnp.int32, (QB, KB), 0)
    kn = start + lax.broadcasted_iota(jnp.int32, (QB, KB), 1)
    ok = jnp.abs(kn - qn) <= half
    first = lax.broadcasted_iota(jnp.int32, (QB, LANES), 1) < HEAD_DIM
    scale = HEAD_DIM ** -0.5
    for t in range(ntile):
        cs = slice(t * LANES, (t + 1) * LANES)
        qr = _rope_tile(q_ref[0, :, cs].astype(F32), c_ref[pl.ds(n0, QB), cs], sa_ref[pl.ds(n0, QB), cs],
                        sb_ref[pl.ds(n0, QB), cs]) * scale
        kt = kr_ref[pl.ds(start, KB), cs]
        vt = v_ref[0, pl.ds(start, KB), cs]
        outs, lses = [], []
        for sub in range(2):
            qm = jnp.where(first, qr, 0.0) if sub == 0 else jnp.where(first, 0.0, qr)
            s = jnp.where(ok, _dot_nt(qm.astype(BF16), kt), NEG_BIG)
            m = jnp.max(s, axis=-1, keepdims=True)
            e = jnp.exp(s - m)
            den = jnp.sum(e, axis=-1, keepdims=True)
            outs.append(jnp.dot(e.astype(BF16), vt, preferred_element_type=F32) * (1.0 / den))
            lses.append(jnp.broadcast_to(m + jnp.log(den), (QB, LANES)))
        o_ref[0, :, cs] = jnp.where(first, outs[0], outs[1])
        lse_ref[0, :, cs] = jnp.where(first, lses[0], lses[1])


def _rope_tables(S):
    half = HEAD_DIM // 2
    inv = ROPE_THETA ** (-jnp.arange(half, dtype=F32) / half)
    ang = jnp.arange(S).astype(F32)[:, None] * inv[None, :]
    cos, sin = jnp.cos(ang), jnp.sin(ang)
    z = jnp.zeros_like(sin)
    c = jnp.tile(jnp.concatenate([cos, cos], axis=-1), (1, DIL_HEADS))
    sa = jnp.tile(jnp.concatenate([-sin, z], axis=-1), (1, DIL_HEADS))
    sb = jnp.tile(jnp.concatenate([z, sin], axis=-1), (1, DIL_HEADS))
    return c, sa, sb


def _dil_attention(h3, tabs, g, col0):
    B, S, C = h3.shape
    window, dil = DIL_PATTERNS[g]
    L = S // dil
    half = window // (2 * dil)
    QB = min(L, 256)
    KB = min(L, QB + 2 * half)
    gcol = col0 + g * 3 * DIL_GW
    if dil == 1:
        hv, cpb, qb = h3, C // DIL_GW, gcol // DIL_GW
    else:
        hv, cpb, qb = h3[:, :, gcol:gcol + 3 * DIL_GW].reshape(B, L, dil * 3 * DIL_GW), 3, 0
    c, sa, sb = [t.reshape(L, dil * DIL_GW) for t in tabs]
    tab_spec = pl.BlockSpec((L, DIL_GW), lambda b, r, i: (0, r))
    o, lse = pl.pallas_call(
        functools.partial(_dil_kernel, L=L, QB=QB, KB=KB, half=half),
        grid=(B, dil, L // QB),
        in_specs=[pl.BlockSpec((1, QB, DIL_GW), lambda b, r, i: (b, i, r * cpb + qb)),
                  pl.BlockSpec((1, L, DIL_GW), lambda b, r, i: (b, 0, r * cpb + qb + 1)),
                  pl.BlockSpec((1, L, DIL_GW), lambda b, r, i: (b, 0, r * cpb + qb + 2)),
                  tab_spec, tab_spec, tab_spec],
        out_specs=[pl.BlockSpec((1, QB, DIL_GW), lambda b, r, i: (b, i, r)),
                   pl.BlockSpec((1, QB, DIL_GW), lambda b, r, i: (b, i, r))],
        out_shape=[jax.ShapeDtypeStruct((B, L, dil * DIL_GW), F32),
                   jax.ShapeDtypeStruct((B, L, dil * DIL_GW), F32)],
        scratch_shapes=[pltpu.VMEM((L, DIL_GW), BF16)],
        compiler_params=_cparams(("parallel", "parallel", "arbitrary")),
        name=f"dil_attn_g{g}",
    )(hv, hv, hv, c, sa, sb)
    return o.reshape(B * S, DIL_GW), lse.reshape(B * S, DIL_GW)


def _merge_kernel(x_ref, gate_ref, a_ref, b_ref, o0_ref, o1_ref, o2_ref, l0_ref, l1_ref, l2_ref, p_ref,
                  wa_ref, wb_ref, wc_ref, wo_ref, g1_ref, b1_ref, wpg_ref, wp_ref,
                  xb_ref, base_ref, *, alpha, D):
    a = jnp.dot(a_ref[...], wa_ref[...], preferred_element_type=F32)
    b = jnp.dot(b_ref[...], wb_ref[...], preferred_element_type=F32)
    l0, l1, l2 = l0_ref[...], l1_ref[...], l2_ref[...]
    m = jnp.maximum(jnp.maximum(l0, l1), l2)
    e0, e1, e2 = jnp.exp(l0 - m), jnp.exp(l1 - m), jnp.exp(l2 - m)
    cin = (e0 * o0_ref[...] + e1 * o1_ref[...] + e2 * o2_ref[...]) * (1.0 / (e0 + e1 + e2))
    c = jnp.dot(cin.astype(BF16), wc_ref[...], preferred_element_type=F32)
    merged = (_sigmoid(gate_ref[:, 0:D].astype(F32)) * a
              + _sigmoid(gate_ref[:, D:2 * D].astype(F32)) * b
              + _sigmoid(gate_ref[:, 2 * D:3 * D].astype(F32)) * c)
    mix = jnp.dot(merged.astype(BF16), wo_ref[...], preferred_element_type=F32)
    x1 = _ln(alpha * x_ref[...] + mix, g1_ref[...], b1_ref[...])
    x1b = x1.astype(BF16)
    xb_ref[...] = x1b
    ple = (_sigmoid(jnp.dot(x1b, wpg_ref[...], preferred_element_type=F32))
           * jnp.dot(p_ref[...].astype(BF16), wp_ref[...], preferred_element_type=F32))
    base_ref[...] = alpha * x1 + ple


def _merge(x, h, na, sgu, dil_o, dil_l, p, wa, wb, wc, wo, g1, b1, wpg, wp, alpha, tm=256):
    T, D = x.shape
    tm = min(tm, T)
    row = lambda w: pl.BlockSpec((tm, w), lambda i: (i, 0))
    full = lambda arr: pl.BlockSpec(arr.shape, lambda i: (0,) * arr.ndim)
    g1, b1 = g1.reshape(1, D), b1.reshape(1, D)
    return pl.pallas_call(
        functools.partial(_merge_kernel, alpha=alpha, D=D),
        grid=(T // tm,),
        in_specs=[row(D), row(N_BRANCH * D), row(NA_WIDTH), row(SGU_WIDTH),
                  row(DIL_GW), row(DIL_GW), row(DIL_GW), row(DIL_GW), row(DIL_GW), row(DIL_GW),
                  row(p.shape[1]),
                  full(wa), full(wb), full(wc), full(wo), full(g1), full(b1), full(wpg), full(wp)],
        out_specs=[row(D), row(D)],
        out_shape=[jax.ShapeDtypeStruct((T, D), BF16), jax.ShapeDtypeStruct((T, D), F32)],
        compiler_params=_cparams(("parallel",)),
        name="merge",
    )(x, h, na, sgu, dil_o[0], dil_o[1], dil_o[2], dil_l[0], dil_l[1], dil_l[2], p,
      wa, wb, wc, wo, g1, b1, wpg, wp)


_CAND_ROWS = 16 + 7 * 8 + 8


def _cand_index():
    idx = [float(j) for j in range(16)]
    for i in range(1, 8):
        idx += [float(i * 16 + j) for j in range(8)]
    idx += [float(i * 16) for i in range(8, 16)]
    return np.broadcast_to(np.asarray(idx, np.float32)[:, None], (_CAND_ROWS, ROUTE_LW)).copy()


def _extract_top(s, order, n, tie_safe):
    rank = jnp.full(s.shape, 127.0, F32)
    vals = []
    for i in range(n):
        m = jnp.max(s, axis=0, keepdims=True)
        if tie_safe:
            pick = jnp.min(jnp.where(s == m, order, 1e9), axis=0, keepdims=True)
            sel = order == pick
        else:
            sel = s == m
        rank = jnp.where(sel, float(i), rank)
        s = jnp.where(sel, -jnp.inf, s)
        vals.append(m)
    return rank, vals


def _row_tile(ref, h, a, ls):
    row = ref[h, a:a + 1, ls]
    return jnp.broadcast_to(row, (BF16_ROWS, row.shape[1])).astype(BF16)


def _count_taken(rank):
    return jnp.sum(jnp.where(rank < 100.0, 1.0, 0.0), axis=0, keepdims=True)


def _route_chunk(s1, s2, cidx, tie_safe):
    key_order = lax.broadcasted_iota(jnp.int32, s1.shape, 0).astype(F32)
    rank1, v1 = _extract_top(s1, key_order, PEER_TOPK, tie_safe)
    rank2, v2 = _extract_top(s2, key_order, PEER_TOPK, tie_safe)
    v1a = jnp.concatenate(v1, axis=0)
    v2a = jnp.concatenate(v2, axis=0)
    e1r = jnp.exp(v1a - v1[0])
    e2r = jnp.exp(v2a - v2[0])

    def pairs(r1, r2, op):
        blocks = [op(r1[0:1], r2)]
        blocks += [op(r1[i:i + 1], r2[0:8]) for i in range(1, 8)]
        blocks += [op(r1[8:16], r2[0:1])]
        return jnp.concatenate(blocks, axis=0)

    cand = pairs(v1a, v2a, jnp.add)
    ecand = pairs(e1r, e2r, jnp.multiply)
    crank, _ = _extract_top(cand, cidx, PEER_TOPK, tie_safe)
    sel = jnp.where(crank < 100.0, 1.0, 0.0)
    z = jnp.sum(sel * ecand, axis=0, keepdims=True)
    cnt = [jnp.sum(sel[0:16], axis=0, keepdims=True)]
    cnt += [jnp.sum(sel[16 + 8 * (i - 1):16 + 8 * i], axis=0, keepdims=True) for i in range(1, 8)]
    cnt += [sel[72 + i:73 + i] for i in range(8)]
    nx1 = jnp.full(s1.shape, 1e9, F32)
    for i in range(PEER_TOPK):
        nx1 = jnp.where(rank1 == float(i), 0.5 - cnt[i], nx1)
    a1 = jnp.where(rank1 < 100.0, jnp.exp(s1 - v1[0]), 0.0) * (1.0 / z)
    e2 = jnp.exp(s2 - v2[0])
    excess = (_count_taken(rank1) + _count_taken(rank2) + _count_taken(crank)) - 3.0 * PEER_TOPK
    return a1, nx1, e2, -rank2, excess


def _route_kernel(x_ref, wq_ref, k1_ref, k2_ref, cidx_ref, a1_ref, nx1_ref, e2_ref, x2_ref, s1_ref, s2_ref,
                  *, tb):
    hk = PEER_DKEY // 2
    q = jnp.dot(x_ref[...], wq_ref[...], preferred_element_type=F32).astype(BF16)
    s1_ref[...] = _dot_nt(k1_ref[0], q[:, :hk])
    s2_ref[...] = _dot_nt(k2_ref[0], q[:, hk:])

    def chunk(c, carry):
        ls = pl.ds(pl.multiple_of(c * ROUTE_LW, ROUTE_LW), ROUTE_LW)

        def run(tie_safe):
            a1, nx1, e2, x2, excess = _route_chunk(s1_ref[:, ls], s2_ref[:, ls], cidx_ref[...], tie_safe)
            a1_ref[0, :, ls] = a1
            nx1_ref[0, :, ls] = nx1
            e2_ref[0, :, ls] = e2.astype(e2_ref.dtype)
            x2_ref[0, :, ls] = x2.astype(x2_ref.dtype)
            return excess

        tied = jnp.max(run(False)) > 0.0

        @pl.when(tied)
        def _():
            run(True)

        return carry

    lax.fori_loop(0, tb // ROUTE_LW, chunk, 0)


def _peer_route(xb, wq, k1, k2, tb):
    T, D = xb.shape
    H = PEER_HEADS
    cidx = jnp.asarray(_cand_index())
    out_spec = pl.BlockSpec((1, PEER_NKEYS, tb), lambda i, h: (h, 0, i))
    out_sds = lambda dt: jax.ShapeDtypeStruct((H, PEER_NKEYS, T), dt)
    return pl.pallas_call(
        functools.partial(_route_kernel, tb=tb),
        grid=(T // tb, H),
        in_specs=[pl.BlockSpec((tb, D), lambda i, h: (i, 0)),
                  pl.BlockSpec((D, PEER_DKEY), lambda i, h: (0, h)),
                  pl.BlockSpec((1, PEER_NKEYS, PEER_DKEY // 2), lambda i, h: (h, 0, 0)),
                  pl.BlockSpec((1, PEER_NKEYS, PEER_DKEY // 2), lambda i, h: (h, 0, 0)),
                  pl.BlockSpec((_CAND_ROWS, ROUTE_LW), lambda i, h: (0, 0))],
        out_specs=[out_spec] * 4,
        out_shape=[out_sds(F32), out_sds(F32), out_sds(BF16), out_sds(BF16)],
        scratch_shapes=[pltpu.VMEM((PEER_NKEYS, tb), F32), pltpu.VMEM((PEER_NKEYS, tb), F32)],
        compiler_params=_cparams(("parallel", "arbitrary")),
        name="peer_route",
    )(xb, wq, k1, k2, cidx)


def _dense_kernel(x_ref, u_ref, vt_ref, a1_ref, nx1_ref, e2_ref, x2_ref, base_ref, g_ref, b_ref,
                  yf_ref, yb_ref, acc_ref, act_ref, pw_ref, *, eb):
    e = pl.program_id(1)
    last = pl.num_programs(1) - 1
    tb = x_ref.shape[0]
    zero = jnp.zeros((), BF16)

    for sb in range(eb // DENSE_SUB):
        rows_e = slice(sb * DENSE_SUB, (sb + 1) * DENSE_SUB)
        act_ref[rows_e, :] = _gelu(_dot_nt(u_ref[rows_e, :], x_ref[...]).astype(BF16))
    nk = PEER_NKEYS // BF16_ROWS

    for al in range(eb // PEER_NKEYS):
        for lt in range(tb // DENSE_LW):
            ls = slice(lt * DENSE_LW, (lt + 1) * DENSE_LW)
            w = [None] * nk
            for h in range(PEER_HEADS):
                nxb = _row_tile(nx1_ref, h, al, ls)
                a1b = _row_tile(a1_ref, h, al, ls)
                for k in range(nk):
                    rows = slice(k * BF16_ROWS, (k + 1) * BF16_ROWS)
                    term = jnp.where(x2_ref[h, rows, ls] >= nxb, e2_ref[h, rows, ls], zero) * a1b
                    w[k] = term if w[k] is None else w[k] + term
            for k in range(nk):
                r0 = al * PEER_NKEYS + k * BF16_ROWS
                pw_ref[r0:r0 + BF16_ROWS, ls] = w[k] * act_ref[r0:r0 + BF16_ROWS, ls]

    d = jnp.dot(vt_ref[...], pw_ref[...], preferred_element_type=F32)

    @pl.when(e == 0)
    def _():
        acc_ref[...] = d

    @pl.when(e > 0)
    def _():
        acc_ref[...] += d

    @pl.when(e == last)
    def _():
        y = _ln(base_ref[...] + acc_ref[...].T, g_ref[...], b_ref[...])
        yf_ref[...] = y
        yb_ref[...] = y.astype(BF16)


def _peer_dense(xb, u, vt, route, base, g2, b2, tb, eb=2048):
    T, D = xb.shape
    NE = u.shape[0]
    H = PEER_HEADS
    rspec = pl.BlockSpec((H, PEER_NKEYS, tb), lambda i, e: (0, 0, i))
    aspec = pl.BlockSpec((H, eb // PEER_NKEYS, tb), lambda i, e: (0, e, i))
    return pl.pallas_call(
        functools.partial(_dense_kernel, eb=eb),
        grid=(T // tb, NE // eb),
        in_specs=[pl.BlockSpec((tb, D), lambda i, e: (i, 0)),
                  pl.BlockSpec((eb, D), lambda i, e: (e, 0)),
                  pl.BlockSpec((D, eb), lambda i, e: (0, e)),
                  aspec, aspec, rspec, rspec,
                  pl.BlockSpec((tb, D), lambda i, e: (i, 0)),
                  pl.BlockSpec((1, D), lambda i, e: (0, 0)),
                  pl.BlockSpec((1, D), lambda i, e: (0, 0))],
        out_specs=[pl.BlockSpec((tb, D), lambda i, e: (i, 0)),
                   pl.BlockSpec((tb, D), lambda i, e: (i, 0))],
        out_shape=[jax.ShapeDtypeStruct((T, D), F32), jax.ShapeDtypeStruct((T, D), BF16)],
        scratch_shapes=[pltpu.VMEM((D, tb), F32), pltpu.VMEM((eb, tb), BF16), pltpu.VMEM((eb, tb), BF16)],
        compiler_params=_cparams(("parallel", "arbitrary")),
        name="peer_dense",
    )(xb, u, vt, route[0], route[1], route[2], route[3], base, g2.reshape(1, D), b2.reshape(1, D))


def _layer_weights(i, w_in, rpb, sgu_ln_g, sgu_ln_b, sgu_w, sgu_b, w_br_a, w_br_b, w_br_c, w_out, ln1_g, ln1_b,
                   peer_wq, peer_k1, peer_k2, peer_u, peer_v, ple_w, ple_gate_w, ln2_g, ln2_b):
    D = w_in.shape[1]
    off_c = 3 * NA_WIDTH + 2 * SGU_WIDTH + 3 * DIL_WIDTH
    off_b = 3 * NA_WIDTH + 2 * SGU_WIDTH
    wc_cols = [w_in[i][:, off_b + part * DIL_WIDTH + g * DIL_GW: off_b + part * DIL_WIDTH + (g + 1) * DIL_GW]
               for g in range(len(DIL_PATTERNS)) for part in range(3)]
    w_perm = jnp.concatenate([w_in[i][:, off_c:], w_in[i][:, :off_b]] + wc_cols, axis=1).astype(BF16)
    return dict(
        w_in=w_perm, col_a=N_BRANCH * D, col_b=N_BRANCH * D + 3 * NA_WIDTH,
        col_c=N_BRANCH * D + 3 * NA_WIDTH + 2 * SGU_WIDTH,
        na_bias=_na_bias_table(rpb[i]),
        sgu_ln_g=sgu_ln_g[i], sgu_ln_b=sgu_ln_b[i], sgu_w=sgu_w[i], sgu_b=sgu_b[i],
        wa=w_br_a[i].astype(BF16), wb=w_br_b[i].astype(BF16), wc=w_br_c[i].astype(BF16),
        wo=w_out[i].astype(BF16), ln1_g=ln1_g[i], ln1_b=ln1_b[i],
        wq=peer_wq[i].astype(BF16), k1=peer_k1[i].astype(BF16), k2=peer_k2[i].astype(BF16),
        u=peer_u[i].astype(BF16), vt=peer_v[i].astype(BF16).T,
        wp=ple_w[i].astype(BF16), wpg=ple_gate_w[i].astype(BF16), ln2_g=ln2_g[i], ln2_b=ln2_b[i])


def _trunk(x, p, ln0_g, ln0_b, layers, alpha):
    B, S, D = x.shape
    T = B * S
    tb = min(512, T)
    tabs = _rope_tables(S)
    xf, xb = _ln0(x.reshape(T, D), ln0_g, ln0_b)
    for i, lw in enumerate(layers):
        h = _inproj(xb, lw["w_in"])
        h3 = h.reshape(B, S, -1)
        na = _na_attention(h3, lw["na_bias"], lw["col_a"]).reshape(T, NA_WIDTH)
        sg = _sgu(h3, lw["sgu_ln_g"], lw["sgu_ln_b"], lw["sgu_w"], lw["sgu_b"], lw["col_b"]).reshape(T, SGU_WIDTH)
        dil = [_dil_attention(h3, tabs, g, lw["col_c"]) for g in range(len(DIL_PATTERNS))]
        x1b, base = _merge(xf, h, na, sg, [d[0] for d in dil], [d[1] for d in dil], p[i].reshape(T, -1),
                           lw["wa"], lw["wb"], lw["wc"], lw["wo"], lw["ln1_g"], lw["ln1_b"],
                           lw["wpg"], lw["wp"], alpha)
        route = _peer_route(x1b, lw["wq"], lw["k1"], lw["k2"], tb)
        xf, xb = _peer_dense(x1b, lw["u"], lw["vt"], route, base, lw["ln2_g"], lw["ln2_b"], tb)
    return xf.reshape(B, S, D)


def kernel(x_prompt, x_sample, p_prompt, p_sample, ln0_g, ln0_b, w_in, rpb, sgu_ln_g, sgu_ln_b, sgu_w, sgu_b, w_br_a, w_br_b, w_br_c, w_out, ln1_g, ln1_b, peer_wq, peer_k1, peer_k2, peer_u, peer_v, ple_w, ple_gate_w, ln2_g, ln2_b):
    depth = w_in.shape[0]
    alpha = (2 * depth) ** 0.25
    layers = [_layer_weights(i, w_in, rpb, sgu_ln_g, sgu_ln_b, sgu_w, sgu_b, w_br_a, w_br_b, w_br_c, w_out,
                             ln1_g, ln1_b, peer_wq, peer_k1, peer_k2, peer_u, peer_v, ple_w, ple_gate_w,
                             ln2_g, ln2_b) for i in range(depth)]
    y_prompt = _trunk(x_prompt, p_prompt, ln0_g, ln0_b, layers, alpha)
    y_sample = _trunk(x_sample, p_sample, ln0_g, ln0_b, layers, alpha)
    return (y_prompt, y_sample)
```

```python
import functools

import numpy as np
import jax
import jax.numpy as jnp
from jax import lax
from jax.experimental import pallas as pl
from jax.experimental.pallas import tpu as pltpu

F32 = jnp.float32
BF16 = jnp.bfloat16

GRID_W = 64
HEAD_DIM = 64
NA_HEADS = 8
NA_WIDTH = NA_HEADS * HEAD_DIM
NA_ROWS = 8
NA_COLS = 16
SGU_GROUPS = 6
SGU_GROUP_CH = 128
SGU_WIDTH = SGU_GROUPS * SGU_GROUP_CH
SGU_CHUNK = 128
DIL_PATTERNS = ((128, 1), (512, 4), (2048, 16))
DIL_HEADS = 4
DIL_GW = DIL_HEADS * HEAD_DIM
DIL_WIDTH = len(DIL_PATTERNS) * DIL_GW
ROPE_THETA = 10000.0
N_BRANCH = 3
PEER_HEADS = 8
PEER_NKEYS = 128
PEER_DKEY = 256
PEER_TOPK = 16
LN_EPS = 1e-5
NEG_BIG = -1e30

LANES = 128
BF16_ROWS = 16
DENSE_SUB = 512
DENSE_LW = 256
NA_ROWS_PER_STEP = 4
ROUTE_LW = 256
VMEM_LIMIT = 48 * 1024 * 1024


def _cparams(sem):
    return pltpu.CompilerParams(dimension_semantics=sem, vmem_limit_bytes=VMEM_LIMIT)


def _ln(x, g, b):
    mu = jnp.mean(x, axis=-1, keepdims=True)
    xc = x - mu
    var = jnp.mean(xc * xc, axis=-1, keepdims=True)
    return xc * lax.rsqrt(var + LN_EPS) * g + b


def _gelu(x):
    return x * (0.5 * (1.0 + jnp.tanh(0.7978845608028654 * (x + 0.044715 * (x * x * x)))))


def _sigmoid(x):
    return 1.0 / (1.0 + jnp.exp(-x))


def _dot_nt(a, b):
    return lax.dot_general(a, b, (((1,), (1,)), ((), ())), preferred_element_type=F32)


def _ln0_kernel(x_ref, g_ref, b_ref, xf_ref, xb_ref):
    y = _ln(x_ref[...], g_ref[...], b_ref[...])
    xf_ref[...] = y
    xb_ref[...] = y.astype(BF16)


def _ln0(x, g, b, tm=512):
    T, D = x.shape
    return pl.pallas_call(
        _ln0_kernel,
        grid=(T // tm,),
        in_specs=[pl.BlockSpec((tm, D), lambda i: (i, 0)),
                  pl.BlockSpec((1, D), lambda i: (0, 0)),
                  pl.BlockSpec((1, D), lambda i: (0, 0))],
        out_specs=[pl.BlockSpec((tm, D), lambda i: (i, 0)),
                   pl.BlockSpec((tm, D), lambda i: (i, 0))],
        out_shape=[jax.ShapeDtypeStruct((T, D), F32), jax.ShapeDtypeStruct((T, D), BF16)],
        compiler_params=_cparams(("parallel",)),
        name="ln0",
    )(x, g.reshape(1, D), b.reshape(1, D))


def _matmul_kernel(x_ref, w_ref, o_ref):
    o_ref[...] = jnp.dot(x_ref[...], w_ref[...], preferred_element_type=F32).astype(o_ref.dtype)


def _inproj(xb, w, tm=1024, tn=768):
    T, K = xb.shape
    N = w.shape[1]
    tm = min(tm, T)
    return pl.pallas_call(
        _matmul_kernel,
        grid=(T // tm, N // tn),
        in_specs=[pl.BlockSpec((tm, K), lambda i, j: (i, 0)),
                  pl.BlockSpec((K, tn), lambda i, j: (0, j))],
        out_specs=pl.BlockSpec((tm, tn), lambda i, j: (i, j)),
        out_shape=jax.ShapeDtypeStruct((T, N), BF16),
        compiler_params=_cparams(("parallel", "parallel")),
        name="inproj",
    )(xb, w)


def _na_kernel(q_ref, k_ref, v_ref, bias_ref, o_ref, *, rows, rb):
    nk = NA_ROWS * GRID_W
    first = lax.broadcasted_iota(jnp.int32, (GRID_W, LANES), 1) < HEAD_DIM
    scale = HEAD_DIM ** -0.5
    for rr in range(rb):
        r = pl.program_id(1) * rb + rr
        rs = jnp.clip(r - NA_ROWS // 2, 0, rows - NA_ROWS)
        case = r - rs
        start = pl.multiple_of(rs * GRID_W, GRID_W)
        qrows = slice(rr * GRID_W, (rr + 1) * GRID_W)
        for p in range(NA_WIDTH // LANES):
            cols = slice(p * LANES, (p + 1) * LANES)
            qp = q_ref[0, qrows, cols].astype(F32) * scale
            kp = k_ref[0, pl.ds(start, nk), cols]
            vp = v_ref[0, pl.ds(start, nk), cols]
            q2 = jnp.concatenate([jnp.where(first, qp, 0.0), jnp.where(first, 0.0, qp)], axis=0).astype(BF16)
            s = _dot_nt(q2, kp) + bias_ref[case, p * 2 * GRID_W:(p + 1) * 2 * GRID_W, :]
            m = jnp.max(s, axis=-1, keepdims=True)
            e = jnp.exp(s - m)
            l = jnp.sum(e, axis=-1, keepdims=True)
            o2 = jnp.dot(e.astype(BF16), vp, preferred_element_type=F32) * (1.0 / l)
            o_ref[0, qrows, cols] = jnp.where(first, o2[:GRID_W], o2[GRID_W:]).astype(o_ref.dtype)


def _na_bias_table(rpb):
    qcol = np.arange(GRID_W)
    kcol = np.arange(GRID_W)
    cstart = np.clip(qcol - NA_COLS // 2, 0, GRID_W - NA_COLS)
    ok = (kcol[None, :] >= cstart[:, None]) & (kcol[None, :] < cstart[:, None] + NA_COLS)
    dc = np.clip(kcol[None, :] - qcol[:, None], -(NA_COLS - 1), NA_COLS - 1) + (NA_COLS - 1)
    b = jnp.where(ok[None, None], rpb[:, :, dc], NEG_BIG)
    cases = []
    for c in range(NA_ROWS):
        dr = np.arange(NA_ROWS) - c + (NA_ROWS - 1)
        t = b[:, dr]
        cases.append(t.transpose(0, 2, 1, 3).reshape(NA_HEADS * GRID_W, NA_ROWS * GRID_W))
    return jnp.stack(cases, axis=0).astype(F32)


def _na_attention(h3, bias_tab, col0):
    B, S, _ = h3.shape
    rows = S // GRID_W
    cb = col0 // NA_WIDTH
    rb = NA_ROWS_PER_STEP
    return pl.pallas_call(
        functools.partial(_na_kernel, rows=rows, rb=rb),
        grid=(B, rows // rb),
        in_specs=[pl.BlockSpec((1, rb * GRID_W, NA_WIDTH), lambda b, r: (b, r, cb)),
                  pl.BlockSpec((1, S, NA_WIDTH), lambda b, r: (b, 0, cb + 1)),
                  pl.BlockSpec((1, S, NA_WIDTH), lambda b, r: (b, 0, cb + 2)),
                  pl.BlockSpec(bias_tab.shape, lambda b, r: (0, 0, 0))],
        out_specs=pl.BlockSpec((1, rb * GRID_W, NA_WIDTH), lambda b, r: (b, r, 0)),
        out_shape=jax.ShapeDtypeStruct((B, S, NA_WIDTH), BF16),
        compiler_params=_cparams(("parallel", "arbitrary")),
        name="na_attn",
    )(h3, h3, h3, bias_tab)


def _sgu_kernel(u_ref, v_ref, g_ref, b_ref, ws_ref, bs_ref, o_ref, *, nchunk):
    for c in range(nchunk):
        rs = slice(c * SGU_CHUNK, (c + 1) * SGU_CHUNK)
        v = _ln(_gelu(v_ref[0, rs, :].astype(F32)), g_ref[...], b_ref[...]).astype(BF16)
        u = _gelu(u_ref[0, rs, :].astype(F32))
        for g in range(SGU_GROUPS):
            cs = slice(g * SGU_GROUP_CH, (g + 1) * SGU_GROUP_CH)
            s = jnp.dot(ws_ref[g], v[:, cs], preferred_element_type=F32) + bs_ref[g]
            o_ref[0, rs, cs] = (u[:, cs] * s).astype(o_ref.dtype)


def _sgu(h3, ln_g, ln_b, ws, bs, col0, tc=512):
    B, S, _ = h3.shape
    tc = min(tc, S)
    cb = col0 // SGU_WIDTH
    bs_b = jnp.broadcast_to(bs[:, :, None], (SGU_GROUPS, SGU_CHUNK, SGU_GROUP_CH)).astype(F32)
    return pl.pallas_call(
        functools.partial(_sgu_kernel, nchunk=tc // SGU_CHUNK),
        grid=(B, S // tc),
        in_specs=[pl.BlockSpec((1, tc, SGU_WIDTH), lambda b, i: (b, i, cb)),
                  pl.BlockSpec((1, tc, SGU_WIDTH), lambda b, i: (b, i, cb + 1)),
                  pl.BlockSpec((1, SGU_WIDTH), lambda b, i: (0, 0)),
                  pl.BlockSpec((1, SGU_WIDTH), lambda b, i: (0, 0)),
                  pl.BlockSpec((SGU_GROUPS, SGU_CHUNK, SGU_CHUNK), lambda b, i: (0, 0, 0)),
                  pl.BlockSpec((SGU_GROUPS, SGU_CHUNK, SGU_GROUP_CH), lambda b, i: (0, 0, 0))],
        out_specs=pl.BlockSpec((1, tc, SGU_WIDTH), lambda b, i: (b, i, 0)),
        out_shape=jax.ShapeDtypeStruct((B, S, SGU_WIDTH), BF16),
        compiler_params=_cparams(("parallel", "parallel")),
        name="sgu",
    )(h3, h3, ln_g.reshape(1, -1), ln_b.reshape(1, -1), ws.astype(BF16), bs_b)


def _rope_tile(x, c, sa, sb):
    return x * c + pltpu.roll(x, LANES - HEAD_DIM // 2, 1) * sa + pltpu.roll(x, HEAD_DIM // 2, 1) * sb


def _dil_kernel(q_ref, k_ref, v_ref, c_ref, sa_ref, sb_ref, o_ref, lse_ref, kr_ref, *, L, QB, KB, half):
    qi = pl.program_id(2)
    ntile = DIL_GW // LANES
    rchunk = min(L, 256)

    @pl.when(qi == 0)
    def _():
        for c0 in range(0, L, rchunk):
            rs = slice(c0, c0 + rchunk)
            for t in range(ntile):
                cs = slice(t * LANES, (t + 1) * LANES)
                kr_ref[rs, cs] = _rope_tile(k_ref[0, rs, cs].astype(F32), c_ref[rs, cs], sa_ref[rs, cs],
                                            sb_ref[rs, cs]).astype(BF16)

    n0 = pl.multiple_of(qi * QB, QB)
    start = pl.multiple_of(jnp.clip(n0 - half, 0, L - KB), 16)
    qn = n0 + lax.broadcasted_iota(jnp.int32, (QB, KB), 0)
    kn = start + lax.broadcasted_iota(jnp.int32, (QB, KB), 1)
    ok = jnp.abs(kn - qn) <= half
    first = lax.broadcasted_iota(jnp.int32, (QB, LANES), 1) < HEAD_DIM
    scale = HEAD_DIM ** -0.5
    for t in range(ntile):
        cs = slice(t * LANES, (t + 1) * LANES)
        qr = _rope_tile(q_ref[0, :, cs].astype(F32), c_ref[pl.ds(n0, QB), cs], sa_ref[pl.ds(n0, QB), cs],
                        sb_ref[pl.ds(n0, QB), cs]) * scale
        kt = kr_ref[pl.ds(start, KB), cs]
        vt = v_ref[0, pl.ds(start, KB), cs]
        outs, lses = [], []
        for sub in range(2):
            qm = jnp.where(first, qr, 0.0) if sub == 0 else jnp.where(first, 0.0, qr)
            s = jnp.where(ok, _dot_nt(qm.astype(BF16), kt), NEG_BIG)
            m = jnp.max(s, axis=-1, keepdims=True)
            e = jnp.exp(s - m)
            den = jnp.sum(e, axis=-1, keepdims=True)
            outs.append(jnp.dot(e.astype(BF16), vt, preferred_element_type=F32) * (1.0 / den))
            lses.append(jnp.broadcast_to(m + jnp.log(den), (QB, LANES)))
        o_ref[0, :, cs] = jnp.where(first, outs[0], outs[1])
        lse_ref[0, :, cs] = jnp.where(first, lses[0], lses[1])


def _rope_tables(S):
    half = HEAD_DIM // 2
    inv = ROPE_THETA ** (-jnp.arange(half, dtype=F32) / half)
    ang = jnp.arange(S).astype(F32)[:, None] * inv[None, :]
    cos, sin = jnp.cos(ang), jnp.sin(ang)
    z = jnp.zeros_like(sin)
    c = jnp.tile(jnp.concatenate([cos, cos], axis=-1), (1, DIL_HEADS))
    sa = jnp.tile(jnp.concatenate([-sin, z], axis=-1), (1, DIL_HEADS))
    sb = jnp.tile(jnp.concatenate([z, sin], axis=-1), (1, DIL_HEADS))
    return c, sa, sb


def _dil_attention(h3, tabs, g, col0):
    B, S, C = h3.shape
    window, dil = DIL_PATTERNS[g]
    L = S // dil
    half = window // (2 * dil)
    QB = min(L, 256)
    KB = min(L, QB + 2 * half)
    gcol = col0 + g * 3 * DIL_GW
    if dil == 1:
        hv, cpb, qb = h3, C // DIL_GW, gcol // DIL_GW
    else:
        hv, cpb, qb = h3[:, :, gcol:gcol + 3 * DIL_GW].reshape(B, L, dil * 3 * DIL_GW), 3, 0
    c, sa, sb = [t.reshape(L, dil * DIL_GW) for t in tabs]
    tab_spec = pl.BlockSpec((L, DIL_GW), lambda b, r, i: (0, r))
    o, lse = pl.pallas_call(
        functools.partial(_dil_kernel, L=L, QB=QB, KB=KB, half=half),
        grid=(B, dil, L // QB),
        in_specs=[pl.BlockSpec((1, QB, DIL_GW), lambda b, r, i: (b, i, r * cpb + qb)),
                  pl.BlockSpec((1, L, DIL_GW), lambda b, r, i: (b, 0, r * cpb + qb + 1)),
                  pl.BlockSpec((1, L, DIL_GW), lambda b, r, i: (b, 0, r * cpb + qb + 2)),
                  tab_spec, tab_spec, tab_spec],
        out_specs=[pl.BlockSpec((1, QB, DIL_GW), lambda b, r, i: (b, i, r)),
                   pl.BlockSpec((1, QB, DIL_GW), lambda b, r, i: (b, i, r))],
        out_shape=[jax.ShapeDtypeStruct((B, L, dil * DIL_GW), F32),
                   jax.ShapeDtypeStruct((B, L, dil * DIL_GW), F32)],
        scratch_shapes=[pltpu.VMEM((L, DIL_GW), BF16)],
        compiler_params=_cparams(("parallel", "parallel", "arbitrary")),
        name=f"dil_attn_g{g}",
    )(hv, hv, hv, c, sa, sb)
    return o.reshape(B * S, DIL_GW), lse.reshape(B * S, DIL_GW)


def _merge_kernel(x_ref, gate_ref, a_ref, b_ref, o0_ref, o1_ref, o2_ref, l0_ref, l1_ref, l2_ref, p_ref,
                  wa_ref, wb_ref, wc_ref, wo_ref, g1_ref, b1_ref, wpg_ref, wp_ref,
                  xb_ref, base_ref, *, alpha, D):
    a = jnp.dot(a_ref[...], wa_ref[...], preferred_element_type=F32)
    b = jnp.dot(b_ref[...], wb_ref[...], preferred_element_type=F32)
    l0, l1, l2 = l0_ref[...], l1_ref[...], l2_ref[...]
    m = jnp.maximum(jnp.maximum(l0, l1), l2)
    e0, e1, e2 = jnp.exp(l0 - m), jnp.exp(l1 - m), jnp.exp(l2 - m)
    cin = (e0 * o0_ref[...] + e1 * o1_ref[...] + e2 * o2_ref[...]) * (1.0 / (e0 + e1 + e2))
    c = jnp.dot(cin.astype(BF16), wc_ref[...], preferred_element_type=F32)
    merged = (_sigmoid(gate_ref[:, 0:D].astype(F32)) * a
              + _sigmoid(gate_ref[:, D:2 * D].astype(F32)) * b
              + _sigmoid(gate_ref[:, 2 * D:3 * D].astype(F32)) * c)
    mix = jnp.dot(merged.astype(BF16), wo_ref[...], preferred_element_type=F32)
    x1 = _ln(alpha * x_ref[...] + mix, g1_ref[...], b1_ref[...])
    x1b = x1.astype(BF16)
    xb_ref[...] = x1b
    ple = (_sigmoid(jnp.dot(x1b, wpg_ref[...], preferred_element_type=F32))
           * jnp.dot(p_ref[...].astype(BF16), wp_ref[...], preferred_element_type=F32))
    base_ref[...] = alpha * x1 + ple


def _merge(x, h, na, sgu, dil_o, dil_l, p, wa, wb, wc, wo, g1, b1, wpg, wp, alpha, tm=256):
    T, D = x.shape
    tm = min(tm, T)
    row = lambda w: pl.BlockSpec((tm, w), lambda i: (i, 0))
    full = lambda arr: pl.BlockSpec(arr.shape, lambda i: (0,) * arr.ndim)
    g1, b1 = g1.reshape(1, D), b1.reshape(1, D)
    return pl.pallas_call(
        functools.partial(_merge_kernel, alpha=alpha, D=D),
        grid=(T // tm,),
        in_specs=[row(D), row(N_BRANCH * D), row(NA_WIDTH), row(SGU_WIDTH),
                  row(DIL_GW), row(DIL_GW), row(DIL_GW), row(DIL_GW), row(DIL_GW), row(DIL_GW),
                  row(p.shape[1]),
                  full(wa), full(wb), full(wc), full(wo), full(g1), full(b1), full(wpg), full(wp)],
        out_specs=[row(D), row(D)],
        out_shape=[jax.ShapeDtypeStruct((T, D), BF16), jax.ShapeDtypeStruct((T, D), F32)],
        compiler_params=_cparams(("parallel",)),
        name="merge",
    )(x, h, na, sgu, dil_o[0], dil_o[1], dil_o[2], dil_l[0], dil_l[1], dil_l[2], p,
      wa, wb, wc, wo, g1, b1, wpg, wp)


_CAND_ROWS = 16 + 7 * 8 + 8


def _cand_index():
    idx = [float(j) for j in range(16)]
    for i in range(1, 8):
        idx += [float(i * 16 + j) for j in range(8)]
    idx += [float(i * 16) for i in range(8, 16)]
    return np.broadcast_to(np.asarray(idx, np.float32)[:, None], (_CAND_ROWS, ROUTE_LW)).copy()


def _extract_top(s, order, n):
    rank = jnp.full(s.shape, 127.0, F32)
    vals = []
    for i in range(n):
        m = jnp.max(s, axis=0, keepdims=True)
        pick = jnp.min(jnp.where(s == m, order, 1e9), axis=0, keepdims=True)
        sel = order == pick
        rank = jnp.where(sel, float(i), rank)
        s = jnp.where(sel, -jnp.inf, s)
        vals.append(m)
    return rank, vals


def _batcher_pairs(n):
    pairs = []

    def merge(lo, hi, r):
        step = r * 2
        if step < hi - lo:
            merge(lo, hi, step)
            merge(lo + r, hi, step)
            pairs.extend((i, i + r) for i in range(lo + r, hi - r, step))
        else:
            pairs.append((lo, lo + r))

    def sort(lo, hi):
        if hi - lo >= 1:
            mid = lo + (hi - lo) // 2
            sort(lo, mid)
            sort(mid + 1, hi)
            merge(lo, hi, 1)

    sort(0, n - 1)
    return pairs


def _sort_desc(xs):
    xs = list(xs)
    for i, j in _batcher_pairs(PEER_TOPK):
        if j < len(xs):
            xs[i], xs[j] = jnp.maximum(xs[i], xs[j]), jnp.minimum(xs[i], xs[j])
    return xs


def _merge_sublanes(xs):
    xs = list(xs)
    n = PEER_TOPK
    for shift in (4, 2, 1):
        t = [jnp.maximum(xs[i], pltpu.roll(xs[n - 1 - i], shift, 0)) for i in range(n)]
        for d in (8, 4, 2, 1):
            for i in range(n):
                if i & d == 0:
                    t[i], t[i + d] = jnp.maximum(t[i], t[i + d]), jnp.minimum(t[i], t[i + d])
        xs = t
    return xs


def _search_bits(x, v):
    c3 = x >= v[7]
    c2 = x >= jnp.where(c3, v[3], v[11])
    c1 = x >= jnp.where(c3, jnp.where(c2, v[1], v[5]), jnp.where(c2, v[9], v[13]))
    c0 = x >= jnp.where(c3,
                        jnp.where(c2, jnp.where(c1, v[0], v[2]), jnp.where(c1, v[4], v[6])),
                        jnp.where(c2, jnp.where(c1, v[8], v[10]), jnp.where(c1, v[12], v[14])))
    return c3, c2, c1, c0


def _pick_by_rank(bits, vals):
    c3, c2, c1, c0 = bits
    l0 = [jnp.where(c0, vals[2 * k], vals[2 * k + 1]) for k in range(8)]
    l1 = [jnp.where(c1, l0[2 * k], l0[2 * k + 1]) for k in range(4)]
    l2 = [jnp.where(c2, l1[2 * k], l1[2 * k + 1]) for k in range(2)]
    return jnp.where(c3, l2[0], l2[1])


def _route_sorted(s1, s2):
    W = s1.shape[1]
    sub_rows = 8
    g1 = [s1[sub_rows * g:sub_rows * (g + 1)] for g in range(PEER_NKEYS // sub_rows)]
    g2 = [s2[sub_rows * g:sub_rows * (g + 1)] for g in range(PEER_NKEYS // sub_rows)]
    v1 = _merge_sublanes(_sort_desc(g1))
    v2 = _merge_sublanes(_sort_desc(g2))
    sub = lax.broadcasted_iota(jnp.int32, (sub_rows, W), 0)

    def pack(vs):
        out = vs[0]
        for s in range(1, sub_rows):
            out = jnp.where(sub == s, vs[s], out)
        return out

    e1r = [jnp.exp(v - v1[0]) for v in v1]
    e2r = [jnp.exp(v - v2[0]) for v in v2]
    v2lo, v2hi, v1hi = pack(v2[:8]), pack(v2[8:]), pack(v1[8:])
    e2lo, e2hi, e1hi = pack(e2r[:8]), pack(e2r[8:]), pack(e1r[8:])
    cand = [v1[0] + v2lo, v1[0] + v2hi] + [v1[i] + v2lo for i in range(1, 8)] + [v1hi + v2[0]]
    ecand = [e1r[0] * e2lo, e1r[0] * e2hi] + [e1r[i] * e2lo for i in range(1, 8)] + [e1hi * e2r[0]]
    neg = jnp.full((sub_rows, W), -jnp.inf, F32)
    tau = _merge_sublanes(_sort_desc(cand) + [neg] * (PEER_TOPK - len(cand)))[PEER_TOPK - 1]
    sel = [jnp.where(c >= tau, 1.0, 0.0) for c in cand]
    colsum = lambda x: jnp.sum(x, axis=0, keepdims=True)
    z = colsum(sum(s * e for s, e in zip(sel, ecand)))
    cnt = [colsum(sel[0] + sel[1])] + [colsum(sel[1 + i]) for i in range(1, 8)] + [sel[9][i:i + 1] for i in range(8)]
    half_minus_cnt = [0.5 - c for c in cnt]
    inv_z = 1.0 / z
    a1, nx1, e2, x2 = [], [], [], []
    n1 = jnp.zeros((sub_rows, W), F32)
    n2 = jnp.zeros((sub_rows, W), F32)
    for x in g1:
        inside = x >= v1[PEER_TOPK - 1]
        nx1.append(jnp.where(inside, _pick_by_rank(_search_bits(x, v1), half_minus_cnt), 1e9))
        a1.append(jnp.where(inside, jnp.exp(x - v1[0]), 0.0) * inv_z)
        n1 = n1 + jnp.where(inside, 1.0, 0.0)
    for x in g2:
        inside = x >= v2[PEER_TOPK - 1]
        c3, c2, c1, c0 = _search_bits(x, v2)
        rank = (jnp.where(c3, 0.0, 8.0) + jnp.where(c2, 0.0, 4.0)) + (jnp.where(c1, 0.0, 2.0) + jnp.where(c0, 0.0, 1.0))
        x2.append(jnp.where(inside, -rank, -127.0))
        e2.append(jnp.exp(x - v2[0]))
        n2 = n2 + jnp.where(inside, 1.0, 0.0)
    dup = jnp.zeros((sub_rows, W), F32)
    for i in range(PEER_TOPK - 1):
        dup = dup + jnp.where(v1[i] == v1[i + 1], 1.0, 0.0) + jnp.where(v2[i] == v2[i + 1], 1.0, 0.0)
    flag = (jnp.abs(colsum(n1) - PEER_TOPK) + jnp.abs(colsum(n2) - PEER_TOPK)
            + jnp.abs(colsum(sum(sel)) - PEER_TOPK) + colsum(dup))
    cat = lambda xs: jnp.concatenate(xs, axis=0)
    return cat(a1), cat(nx1), cat(e2), cat(x2), flag


def _row_tile(ref, h, a, ls):
    row = ref[h, a:a + 1, ls]
    return jnp.broadcast_to(row, (BF16_ROWS, row.shape[1])).astype(BF16)


def _route_chunk(s1, s2, cidx):
    key_order = lax.broadcasted_iota(jnp.int32, s1.shape, 0).astype(F32)
    rank1, v1 = _extract_top(s1, key_order, PEER_TOPK)
    rank2, v2 = _extract_top(s2, key_order, PEER_TOPK)
    v1a = jnp.concatenate(v1, axis=0)
    v2a = jnp.concatenate(v2, axis=0)
    e1r = jnp.exp(v1a - v1[0])
    e2r = jnp.exp(v2a - v2[0])

    def pairs(r1, r2, op):
        blocks = [op(r1[0:1], r2)]
        blocks += [op(r1[i:i + 1], r2[0:8]) for i in range(1, 8)]
        blocks += [op(r1[8:16], r2[0:1])]
        return jnp.concatenate(blocks, axis=0)

    cand = pairs(v1a, v2a, jnp.add)
    ecand = pairs(e1r, e2r, jnp.multiply)
    crank, _ = _extract_top(cand, cidx, PEER_TOPK)
    sel = jnp.where(crank < 100.0, 1.0, 0.0)
    z = jnp.sum(sel * ecand, axis=0, keepdims=True)
    cnt = [jnp.sum(sel[0:16], axis=0, keepdims=True)]
    cnt += [jnp.sum(sel[16 + 8 * (i - 1):16 + 8 * i], axis=0, keepdims=True) for i in range(1, 8)]
    cnt += [sel[72 + i:73 + i] for i in range(8)]
    nx1 = jnp.full(s1.shape, 1e9, F32)
    for i in range(PEER_TOPK):
        nx1 = jnp.where(rank1 == float(i), 0.5 - cnt[i], nx1)
    a1 = jnp.where(rank1 < 100.0, jnp.exp(s1 - v1[0]), 0.0) * (1.0 / z)
    e2 = jnp.exp(s2 - v2[0])
    return a1, nx1, e2, -rank2


def _route_kernel(x_ref, wq_ref, k1_ref, k2_ref, cidx_ref, a1_ref, nx1_ref, e2_ref, x2_ref, q_ref, s1_ref, s2_ref,
                  *, tb):
    hk = PEER_DKEY // 2
    h = pl.program_id(1)

    @pl.when(h == 0)
    def _():
        for hh in range(PEER_HEADS):
            q_ref[hh] = jnp.dot(x_ref[...], wq_ref[:, hh * PEER_DKEY:(hh + 1) * PEER_DKEY],
                                preferred_element_type=F32).astype(BF16)

    s1_ref[...] = _dot_nt(k1_ref[0], q_ref[h, :, :hk])
    s2_ref[...] = _dot_nt(k2_ref[0], q_ref[h, :, hk:])

    def chunk(c, carry):
        ls = pl.ds(pl.multiple_of(c * ROUTE_LW, ROUTE_LW), ROUTE_LW)

        def emit(a1, nx1, e2, x2):
            a1_ref[0, :, ls] = a1
            nx1_ref[0, :, ls] = nx1
            e2_ref[0, :, ls] = e2.astype(e2_ref.dtype)
            x2_ref[0, :, ls] = x2.astype(x2_ref.dtype)

        *fast, flag = _route_sorted(s1_ref[:, ls], s2_ref[:, ls])
        emit(*fast)

        @pl.when(jnp.max(flag) > 0.0)
        def _():
            emit(*_route_chunk(s1_ref[:, ls], s2_ref[:, ls], cidx_ref[...]))

        return carry

    lax.fori_loop(0, tb // ROUTE_LW, chunk, 0)


def _peer_route(xb, wq, k1, k2, tb):
    T, D = xb.shape
    H = PEER_HEADS
    cidx = jnp.asarray(_cand_index())
    out_spec = pl.BlockSpec((1, PEER_NKEYS, tb), lambda i, h: (h, 0, i))
    out_sds = lambda dt: jax.ShapeDtypeStruct((H, PEER_NKEYS, T), dt)
    return pl.pallas_call(
        functools.partial(_route_kernel, tb=tb),
        grid=(T // tb, H),
        in_specs=[pl.BlockSpec((tb, D), lambda i, h: (i, 0)),
                  pl.BlockSpec((D, H * PEER_DKEY), lambda i, h: (0, 0)),
                  pl.BlockSpec((1, PEER_NKEYS, PEER_DKEY // 2), lambda i, h: (h, 0, 0)),
                  pl.BlockSpec((1, PEER_NKEYS, PEER_DKEY // 2), lambda i, h: (h, 0, 0)),
                  pl.BlockSpec((_CAND_ROWS, ROUTE_LW), lambda i, h: (0, 0))],
        out_specs=[out_spec] * 4,
        out_shape=[out_sds(F32), out_sds(F32), out_sds(BF16), out_sds(BF16)],
        scratch_shapes=[pltpu.VMEM((H, tb, PEER_DKEY), BF16),
                        pltpu.VMEM((PEER_NKEYS, tb), F32), pltpu.VMEM((PEER_NKEYS, tb), F32)],
        compiler_params=_cparams(("parallel", "arbitrary")),
        name="peer_route",
    )(xb, wq, k1, k2, cidx)


def _dense_kernel(x_ref, u_ref, vt_ref, a1_ref, nx1_ref, e2_ref, x2_ref, base_ref, g_ref, b_ref,
                  yf_ref, yb_ref, acc_ref, act_ref, pw_ref, *, eb):
    e = pl.program_id(1)
    last = pl.num_programs(1) - 1
    tb = x_ref.shape[0]
    zero = jnp.zeros((), BF16)

    for sb in range(eb // DENSE_SUB):
        rows_e = slice(sb * DENSE_SUB, (sb + 1) * DENSE_SUB)
        act_ref[rows_e, :] = _gelu(_dot_nt(u_ref[rows_e, :], x_ref[...]).astype(BF16))
    nk = PEER_NKEYS // BF16_ROWS

    for al in range(eb // PEER_NKEYS):
        for lt in range(tb // DENSE_LW):
            ls = slice(lt * DENSE_LW, (lt + 1) * DENSE_LW)
            w = [None] * nk
            for h in range(PEER_HEADS):
                nxb = _row_tile(nx1_ref, h, al, ls)
                a1b = _row_tile(a1_ref, h, al, ls)
                for k in range(nk):
                    rows = slice(k * BF16_ROWS, (k + 1) * BF16_ROWS)
                    term = jnp.where(x2_ref[h, rows, ls] >= nxb, e2_ref[h, rows, ls], zero) * a1b
                    w[k] = term if w[k] is None else w[k] + term
            for k in range(nk):
                r0 = al * PEER_NKEYS + k * BF16_ROWS
                pw_ref[r0:r0 + BF16_ROWS, ls] = w[k] * act_ref[r0:r0 + BF16_ROWS, ls]

    d = jnp.dot(vt_ref[...], pw_ref[...], preferred_element_type=F32)

    @pl.when(e == 0)
    def _():
        acc_ref[...] = d

    @pl.when(e > 0)
    def _():
        acc_ref[...] += d

    @pl.when(e == last)
    def _():
        y = _ln(base_ref[...] + acc_ref[...].T, g_ref[...], b_ref[...])
        yf_ref[...] = y
        yb_ref[...] = y.astype(BF16)


def _peer_dense(xb, u, vt, route, base, g2, b2, tb, eb=2048):
    T, D = xb.shape
    NE = u.shape[0]
    H = PEER_HEADS
    rspec = pl.BlockSpec((H, PEER_NKEYS, tb), lambda i, e: (0, 0, i))
    aspec = pl.BlockSpec((H, eb // PEER_NKEYS, tb), lambda i, e: (0, e, i))
    return pl.pallas_call(
        functools.partial(_dense_kernel, eb=eb),
        grid=(T // tb, NE // eb),
        in_specs=[pl.BlockSpec((tb, D), lambda i, e: (i, 0)),
                  pl.BlockSpec((eb, D), lambda i, e: (e, 0)),
                  pl.BlockSpec((D, eb), lambda i, e: (0, e)),
                  aspec, aspec, rspec, rspec,
                  pl.BlockSpec((tb, D), lambda i, e: (i, 0)),
                  pl.BlockSpec((1, D), lambda i, e: (0, 0)),
                  pl.BlockSpec((1, D), lambda i, e: (0, 0))],
        out_specs=[pl.BlockSpec((tb, D), lambda i, e: (i, 0)),
                   pl.BlockSpec((tb, D), lambda i, e: (i, 0))],
        out_shape=[jax.ShapeDtypeStruct((T, D), F32), jax.ShapeDtypeStruct((T, D), BF16)],
        scratch_shapes=[pltpu.VMEM((D, tb), F32), pltpu.VMEM((eb, tb), BF16), pltpu.VMEM((eb, tb), BF16)],
        compiler_params=_cparams(("parallel", "arbitrary")),
        name="peer_dense",
    )(xb, u, vt, route[0], route[1], route[2], route[3], base, g2.reshape(1, D), b2.reshape(1, D))


def _layer_weights(i, w_in, rpb, sgu_ln_g, sgu_ln_b, sgu_w, sgu_b, w_br_a, w_br_b, w_br_c, w_out, ln1_g, ln1_b,
                   peer_wq, peer_k1, peer_k2, peer_u, peer_v, ple_w, ple_gate_w, ln2_g, ln2_b):
    D = w_in.shape[1]
    off_c = 3 * NA_WIDTH + 2 * SGU_WIDTH + 3 * DIL_WIDTH
    off_b = 3 * NA_WIDTH + 2 * SGU_WIDTH
    wc_cols = [w_in[i][:, off_b + part * DIL_WIDTH + g * DIL_GW: off_b + part * DIL_WIDTH + (g + 1) * DIL_GW]
               for g in range(len(DIL_PATTERNS)) for part in range(3)]
    w_perm = jnp.concatenate([w_in[i][:, off_c:], w_in[i][:, :off_b]] + wc_cols, axis=1).astype(BF16)
    return dict(
        w_in=w_perm, col_a=N_BRANCH * D, col_b=N_BRANCH * D + 3 * NA_WIDTH,
        col_c=N_BRANCH * D + 3 * NA_WIDTH + 2 * SGU_WIDTH,
        na_bias=_na_bias_table(rpb[i]),
        sgu_ln_g=sgu_ln_g[i], sgu_ln_b=sgu_ln_b[i], sgu_w=sgu_w[i], sgu_b=sgu_b[i],
        wa=w_br_a[i].astype(BF16), wb=w_br_b[i].astype(BF16), wc=w_br_c[i].astype(BF16),
        wo=w_out[i].astype(BF16), ln1_g=ln1_g[i], ln1_b=ln1_b[i],
        wq=peer_wq[i].astype(BF16), k1=peer_k1[i].astype(BF16), k2=peer_k2[i].astype(BF16),
        u=peer_u[i].astype(BF16), vt=peer_v[i].astype(BF16).T,
        wp=ple_w[i].astype(BF16), wpg=ple_gate_w[i].astype(BF16), ln2_g=ln2_g[i], ln2_b=ln2_b[i])


def _trunk(x, p, ln0_g, ln0_b, layers, alpha):
    B, S, D = x.shape
    T = B * S
    tb = min(512, T)
    tabs = _rope_tables(S)
    xf, xb = _ln0(x.reshape(T, D), ln0_g, ln0_b)
    for i, lw in enumerate(layers):
        h = _inproj(xb, lw["w_in"])
        h3 = h.reshape(B, S, -1)
        na = _na_attention(h3, lw["na_bias"], lw["col_a"]).reshape(T, NA_WIDTH)
        sg = _sgu(h3, lw["sgu_ln_g"], lw["sgu_ln_b"], lw["sgu_w"], lw["sgu_b"], lw["col_b"]).reshape(T, SGU_WIDTH)
        dil = [_dil_attention(h3, tabs, g, lw["col_c"]) for g in range(len(DIL_PATTERNS))]
        x1b, base = _merge(xf, h, na, sg, [d[0] for d in dil], [d[1] for d in dil], p[i].reshape(T, -1),
                           lw["wa"], lw["wb"], lw["wc"], lw["wo"], lw["ln1_g"], lw["ln1_b"],
                           lw["wpg"], lw["wp"], alpha)
        route = _peer_route(x1b, lw["wq"], lw["k1"], lw["k2"], tb)
        xf, xb = _peer_dense(x1b, lw["u"], lw["vt"], route, base, lw["ln2_g"], lw["ln2_b"], tb)
    return xf.reshape(B, S, D)


def kernel(x_prompt, x_sample, p_prompt, p_sample, ln0_g, ln0_b, w_in, rpb, sgu_ln_g, sgu_ln_b, sgu_w, sgu_b, w_br_a, w_br_b, w_br_c, w_out, ln1_g, ln1_b, peer_wq, peer_k1, peer_k2, peer_u, peer_v, ple_w, ple_gate_w, ln2_g, ln2_b):
    depth = w_in.shape[0]
    alpha = (2 * depth) ** 0.25
    layers = [_layer_weights(i, w_in, rpb, sgu_ln_g, sgu_ln_b, sgu_w, sgu_b, w_br_a, w_br_b, w_br_c, w_out,
                             ln1_g, ln1_b, peer_wq, peer_k1, peer_k2, peer_u, peer_v, ple_w, ple_gate_w,
                             ln2_g, ln2_b) for i in range(depth)]
    y_prompt = _trunk(x_prompt, p_prompt, ln0_g, ln0_b, layers, alpha)
    y_sample = _trunk(x_sample, p_sample, ln0_g, ln0_b, layers, alpha)
    return (y_prompt, y_sample)
```

```python
import functools

import numpy as np
import jax
import jax.numpy as jnp
from jax import lax
from jax.experimental import pallas as pl
from jax.experimental.pallas import tpu as pltpu

F32 = jnp.float32
BF16 = jnp.bfloat16

GRID_W = 64
HEAD_DIM = 64
NA_HEADS = 8
NA_WIDTH = NA_HEADS * HEAD_DIM
NA_ROWS = 8
NA_COLS = 16
SGU_GROUPS = 6
SGU_GROUP_CH = 128
SGU_WIDTH = SGU_GROUPS * SGU_GROUP_CH
SGU_CHUNK = 128
DIL_PATTERNS = ((128, 1), (512, 4), (2048, 16))
DIL_HEADS = 4
DIL_GW = DIL_HEADS * HEAD_DIM
DIL_WIDTH = len(DIL_PATTERNS) * DIL_GW
ROPE_THETA = 10000.0
N_BRANCH = 3
PEER_HEADS = 8
PEER_NKEYS = 128
PEER_DKEY = 256
PEER_TOPK = 16
LN_EPS = 1e-5
NEG_BIG = -1e30

LANES = 128
BF16_ROWS = 16
DENSE_SUB = 512
DENSE_LW = 256
NA_ROWS_PER_STEP = 4
ROUTE_LW = 256
VMEM_LIMIT = 48 * 1024 * 1024


def _cparams(sem):
    return pltpu.CompilerParams(dimension_semantics=sem, vmem_limit_bytes=VMEM_LIMIT)


def _ln(x, g, b):
    mu = jnp.mean(x, axis=-1, keepdims=True)
    xc = x - mu
    var = jnp.mean(xc * xc, axis=-1, keepdims=True)
    return xc * lax.rsqrt(var + LN_EPS) * g + b


def _gelu(x):
    return x * (0.5 * (1.0 + jnp.tanh(0.7978845608028654 * (x + 0.044715 * (x * x * x)))))


def _sigmoid(x):
    return 1.0 / (1.0 + jnp.exp(-x))


def _dot_nt(a, b):
    return lax.dot_general(a, b, (((1,), (1,)), ((), ())), preferred_element_type=F32)


def _ln0_kernel(x_ref, g_ref, b_ref, xf_ref, xb_ref):
    y = _ln(x_ref[...], g_ref[...], b_ref[...])
    xf_ref[...] = y
    xb_ref[...] = y.astype(BF16)


def _ln0(x, g, b, tm=512):
    T, D = x.shape
    return pl.pallas_call(
        _ln0_kernel,
        grid=(T // tm,),
        in_specs=[pl.BlockSpec((tm, D), lambda i: (i, 0)),
                  pl.BlockSpec((1, D), lambda i: (0, 0)),
                  pl.BlockSpec((1, D), lambda i: (0, 0))],
        out_specs=[pl.BlockSpec((tm, D), lambda i: (i, 0)),
                   pl.BlockSpec((tm, D), lambda i: (i, 0))],
        out_shape=[jax.ShapeDtypeStruct((T, D), F32), jax.ShapeDtypeStruct((T, D), BF16)],
        compiler_params=_cparams(("parallel",)),
        name="ln0",
    )(x, g.reshape(1, D), b.reshape(1, D))


def _matmul_kernel(x_ref, w_ref, o_ref):
    o_ref[...] = jnp.dot(x_ref[...], w_ref[...], preferred_element_type=F32).astype(o_ref.dtype)


def _inproj(xb, w, tm=1024, tn=768):
    T, K = xb.shape
    N = w.shape[1]
    tm = min(tm, T)
    return pl.pallas_call(
        _matmul_kernel,
        grid=(T // tm, N // tn),
        in_specs=[pl.BlockSpec((tm, K), lambda i, j: (i, 0)),
                  pl.BlockSpec((K, tn), lambda i, j: (0, j))],
        out_specs=pl.BlockSpec((tm, tn), lambda i, j: (i, j)),
        out_shape=jax.ShapeDtypeStruct((T, N), BF16),
        compiler_params=_cparams(("parallel", "parallel")),
        name="inproj",
    )(xb, w)


def _matmul_residue_major_kernel(x_ref, w_ref, o_ref, acc_ref, *, dil):
    acc = jnp.dot(x_ref[...], w_ref[...], preferred_element_type=F32)
    ntile, tm, _ = acc_ref.shape
    for j in range(ntile):
        acc_ref[j] = acc[:, j * LANES:(j + 1) * LANES]
    for r in range(dil):
        for j in range(ntile):
            c0 = (r * ntile + j) * LANES
            o_ref[:, c0:c0 + LANES] = acc_ref[j, pl.ds(r, tm // dil, stride=dil), :].astype(o_ref.dtype)


def _inproj_residue_major(xb, w, dil, tm=1024):
    T, K = xb.shape
    N = w.shape[1]
    tm = min(tm, T)
    return pl.pallas_call(
        functools.partial(_matmul_residue_major_kernel, dil=dil),
        grid=(T // tm,),
        in_specs=[pl.BlockSpec((tm, K), lambda i: (i, 0)),
                  pl.BlockSpec((K, N), lambda i: (0, 0))],
        out_specs=pl.BlockSpec((tm // dil, dil * N), lambda i: (i, 0)),
        out_shape=jax.ShapeDtypeStruct((T // dil, dil * N), BF16),
        scratch_shapes=[pltpu.VMEM((N // LANES, tm, LANES), F32)],
        compiler_params=_cparams(("parallel",)),
        name=f"inproj_d{dil}",
    )(xb, w)


def _na_kernel(q_ref, k_ref, v_ref, bias_ref, o_ref, *, rows, rb):
    nk = NA_ROWS * GRID_W
    first = lax.broadcasted_iota(jnp.int32, (GRID_W, LANES), 1) < HEAD_DIM
    scale = HEAD_DIM ** -0.5
    for rr in range(rb):
        r = pl.program_id(1) * rb + rr
        rs = jnp.clip(r - NA_ROWS // 2, 0, rows - NA_ROWS)
        case = r - rs
        start = pl.multiple_of(rs * GRID_W, GRID_W)
        qrows = slice(rr * GRID_W, (rr + 1) * GRID_W)
        for p in range(NA_WIDTH // LANES):
            cols = slice(p * LANES, (p + 1) * LANES)
            qp = q_ref[0, qrows, cols].astype(F32) * scale
            kp = k_ref[0, pl.ds(start, nk), cols]
            vp = v_ref[0, pl.ds(start, nk), cols]
            q2 = jnp.concatenate([jnp.where(first, qp, 0.0), jnp.where(first, 0.0, qp)], axis=0).astype(BF16)
            s = _dot_nt(q2, kp) + bias_ref[case, p * 2 * GRID_W:(p + 1) * 2 * GRID_W, :]
            m = jnp.max(s, axis=-1, keepdims=True)
            e = jnp.exp(s - m)
            l = jnp.sum(e, axis=-1, keepdims=True)
            o2 = jnp.dot(e.astype(BF16), vp, preferred_element_type=F32) * (1.0 / l)
            o_ref[0, qrows, cols] = jnp.where(first, o2[:GRID_W], o2[GRID_W:]).astype(o_ref.dtype)


def _na_bias_table(rpb):
    qcol = np.arange(GRID_W)
    kcol = np.arange(GRID_W)
    cstart = np.clip(qcol - NA_COLS // 2, 0, GRID_W - NA_COLS)
    ok = (kcol[None, :] >= cstart[:, None]) & (kcol[None, :] < cstart[:, None] + NA_COLS)
    dc = np.clip(kcol[None, :] - qcol[:, None], -(NA_COLS - 1), NA_COLS - 1) + (NA_COLS - 1)
    b = jnp.where(ok[None, None], rpb[:, :, dc], NEG_BIG)
    cases = []
    for c in range(NA_ROWS):
        dr = np.arange(NA_ROWS) - c + (NA_ROWS - 1)
        t = b[:, dr]
        cases.append(t.transpose(0, 2, 1, 3).reshape(NA_HEADS * GRID_W, NA_ROWS * GRID_W))
    return jnp.stack(cases, axis=0).astype(F32)


def _na_attention(h3, bias_tab, col0):
    B, S, _ = h3.shape
    rows = S // GRID_W
    cb = col0 // NA_WIDTH
    rb = NA_ROWS_PER_STEP
    return pl.pallas_call(
        functools.partial(_na_kernel, rows=rows, rb=rb),
        grid=(B, rows // rb),
        in_specs=[pl.BlockSpec((1, rb * GRID_W, NA_WIDTH), lambda b, r: (b, r, cb)),
                  pl.BlockSpec((1, S, NA_WIDTH), lambda b, r: (b, 0, cb + 1)),
                  pl.BlockSpec((1, S, NA_WIDTH), lambda b, r: (b, 0, cb + 2)),
                  pl.BlockSpec(bias_tab.shape, lambda b, r: (0, 0, 0))],
        out_specs=pl.BlockSpec((1, rb * GRID_W, NA_WIDTH), lambda b, r: (b, r, 0)),
        out_shape=jax.ShapeDtypeStruct((B, S, NA_WIDTH), BF16),
        compiler_params=_cparams(("parallel", "arbitrary")),
        name="na_attn",
    )(h3, h3, h3, bias_tab)


def _sgu_kernel(u_ref, v_ref, g_ref, b_ref, ws_ref, bs_ref, o_ref, *, nchunk):
    for c in range(nchunk):
        rs = slice(c * SGU_CHUNK, (c + 1) * SGU_CHUNK)
        v = _ln(_gelu(v_ref[0, rs, :].astype(F32)), g_ref[...], b_ref[...]).astype(BF16)
        u = _gelu(u_ref[0, rs, :].astype(F32))
        for g in range(SGU_GROUPS):
            cs = slice(g * SGU_GROUP_CH, (g + 1) * SGU_GROUP_CH)
            s = jnp.dot(ws_ref[g], v[:, cs], preferred_element_type=F32) + bs_ref[g]
            o_ref[0, rs, cs] = (u[:, cs] * s).astype(o_ref.dtype)


def _sgu(h3, ln_g, ln_b, ws, bs, col0, tc=512):
    B, S, _ = h3.shape
    tc = min(tc, S)
    cb = col0 // SGU_WIDTH
    bs_b = jnp.broadcast_to(bs[:, :, None], (SGU_GROUPS, SGU_CHUNK, SGU_GROUP_CH)).astype(F32)
    return pl.pallas_call(
        functools.partial(_sgu_kernel, nchunk=tc // SGU_CHUNK),
        grid=(B, S // tc),
        in_specs=[pl.BlockSpec((1, tc, SGU_WIDTH), lambda b, i: (b, i, cb)),
                  pl.BlockSpec((1, tc, SGU_WIDTH), lambda b, i: (b, i, cb + 1)),
                  pl.BlockSpec((1, SGU_WIDTH), lambda b, i: (0, 0)),
                  pl.BlockSpec((1, SGU_WIDTH), lambda b, i: (0, 0)),
                  pl.BlockSpec((SGU_GROUPS, SGU_CHUNK, SGU_CHUNK), lambda b, i: (0, 0, 0)),
                  pl.BlockSpec((SGU_GROUPS, SGU_CHUNK, SGU_GROUP_CH), lambda b, i: (0, 0, 0))],
        out_specs=pl.BlockSpec((1, tc, SGU_WIDTH), lambda b, i: (b, i, 0)),
        out_shape=jax.ShapeDtypeStruct((B, S, SGU_WIDTH), BF16),
        compiler_params=_cparams(("parallel", "parallel")),
        name="sgu",
    )(h3, h3, ln_g.reshape(1, -1), ln_b.reshape(1, -1), ws.astype(BF16), bs_b)


def _rope_tile(x, c, sa, sb):
    return x * c + pltpu.roll(x, LANES - HEAD_DIM // 2, 1) * sa + pltpu.roll(x, HEAD_DIM // 2, 1) * sb


def _dil_kernel(q_ref, k_ref, v_ref, c_ref, sa_ref, sb_ref, o_ref, lse_ref, kr_ref, *, L, QB, KB, half, dil):
    qi = pl.program_id(2)
    ntile = DIL_GW // LANES
    rchunk = min(L, 256)

    @pl.when(qi == 0)
    def _():
        for c0 in range(0, L, rchunk):
            rs = slice(c0, c0 + rchunk)
            for t in range(ntile):
                cs = slice(t * LANES, (t + 1) * LANES)
                kr_ref[rs, cs] = _rope_tile(k_ref[0, rs, cs].astype(F32), c_ref[rs, cs], sa_ref[rs, cs],
                                            sb_ref[rs, cs]).astype(BF16)

    n0 = pl.multiple_of(qi * QB, QB)
    start = pl.multiple_of(jnp.clip(n0 - half, 0, L - KB), 16)
    qn = n0 + lax.broadcasted_iota(jnp.int32, (QB, KB), 0)
    kn = start + lax.broadcasted_iota(jnp.int32, (QB, KB), 1)
    ok = jnp.abs(kn - qn) <= half
    first = lax.broadcasted_iota(jnp.int32, (QB, LANES), 1) < HEAD_DIM
    scale = HEAD_DIM ** -0.5
    if dil == 1:
        orows = slice(None)
    else:
        orows = pl.ds(n0 * dil + pl.program_id(1), QB, stride=dil)
    for t in range(ntile):
        cs = slice(t * LANES, (t + 1) * LANES)
        qr = _rope_tile(q_ref[0, :, cs].astype(F32), c_ref[pl.ds(n0, QB), cs], sa_ref[pl.ds(n0, QB), cs],
                        sb_ref[pl.ds(n0, QB), cs]) * scale
        kt = kr_ref[pl.ds(start, KB), cs]
        vt = v_ref[0, pl.ds(start, KB), cs]
        outs, lses = [], []
        for sub in range(2):
            qm = jnp.where(first, qr, 0.0) if sub == 0 else jnp.where(first, 0.0, qr)
            s = jnp.where(ok, _dot_nt(qm.astype(BF16), kt), NEG_BIG)
            m = jnp.max(s, axis=-1, keepdims=True)
            e = jnp.exp(s - m)
            den = jnp.sum(e, axis=-1, keepdims=True)
            outs.append(jnp.dot(e.astype(BF16), vt, preferred_element_type=F32) * (1.0 / den))
            lses.append(jnp.broadcast_to(m + jnp.log(den), (QB, LANES)))
        o_ref[0, t, orows, :] = jnp.where(first, outs[0], outs[1])
        lse_ref[0, t, orows, :] = jnp.where(first, lses[0], lses[1])


def _rope_tables(S):
    half = HEAD_DIM // 2
    inv = ROPE_THETA ** (-jnp.arange(half, dtype=F32) / half)
    ang = jnp.arange(S).astype(F32)[:, None] * inv[None, :]
    cos, sin = jnp.cos(ang), jnp.sin(ang)
    z = jnp.zeros_like(sin)
    c = jnp.tile(jnp.concatenate([cos, cos], axis=-1), (1, DIL_HEADS))
    sa = jnp.tile(jnp.concatenate([-sin, z], axis=-1), (1, DIL_HEADS))
    sb = jnp.tile(jnp.concatenate([z, sin], axis=-1), (1, DIL_HEADS))
    return c, sa, sb


def _dil_attention(hsrc, tabs, g, B, S, col0=0):
    window, dil = DIL_PATTERNS[g]
    L = S // dil
    half = window // (2 * dil)
    QB = min(L, 256)
    KB = min(L, QB + 2 * half)
    ntile = DIL_GW // LANES
    if dil == 1:
        hv, cpb, qb = hsrc, hsrc.shape[-1] // DIL_GW, col0 // DIL_GW
        out_spec = pl.BlockSpec((1, ntile, QB, LANES), lambda b, r, i: (b, 0, i, 0))
    else:
        hv, cpb, qb = hsrc.reshape(B, L, dil * 3 * DIL_GW), 3, 0
        out_spec = pl.BlockSpec((1, ntile, S, LANES), lambda b, r, i: (b, 0, 0, 0))
    c, sa, sb = [t.reshape(L, dil * DIL_GW) for t in tabs]
    tab_spec = pl.BlockSpec((L, DIL_GW), lambda b, r, i: (0, r))
    o, lse = pl.pallas_call(
        functools.partial(_dil_kernel, L=L, QB=QB, KB=KB, half=half, dil=dil),
        grid=(B, dil, L // QB),
        in_specs=[pl.BlockSpec((1, QB, DIL_GW), lambda b, r, i: (b, i, r * cpb + qb)),
                  pl.BlockSpec((1, L, DIL_GW), lambda b, r, i: (b, 0, r * cpb + qb + 1)),
                  pl.BlockSpec((1, L, DIL_GW), lambda b, r, i: (b, 0, r * cpb + qb + 2)),
                  tab_spec, tab_spec, tab_spec],
        out_specs=[out_spec, out_spec],
        out_shape=[jax.ShapeDtypeStruct((B, ntile, S, LANES), F32)] * 2,
        scratch_shapes=[pltpu.VMEM((L, DIL_GW), BF16)],
        compiler_params=_cparams(("parallel", "arbitrary", "arbitrary")),
        name=f"dil_attn_g{g}",
    )(hv, hv, hv, c, sa, sb)
    return o, lse


def _merge_kernel(x_ref, gate_ref, a_ref, b_ref, o0_ref, o1_ref, o2_ref, l0_ref, l1_ref, l2_ref, p_ref,
                  wa_ref, wb_ref, wc_ref, wo_ref, g1_ref, b1_ref, wpg_ref, wp_ref,
                  xb_ref, base_ref, *, alpha, D):
    a = jnp.dot(a_ref[...], wa_ref[...], preferred_element_type=F32)
    b = jnp.dot(b_ref[...], wb_ref[...], preferred_element_type=F32)
    halves = []
    for t in range(DIL_GW // LANES):
        l0, l1, l2 = l0_ref[0, t], l1_ref[0, t], l2_ref[0, t]
        m = jnp.maximum(jnp.maximum(l0, l1), l2)
        e0, e1, e2 = jnp.exp(l0 - m), jnp.exp(l1 - m), jnp.exp(l2 - m)
        halves.append((e0 * o0_ref[0, t] + e1 * o1_ref[0, t] + e2 * o2_ref[0, t]) * (1.0 / (e0 + e1 + e2)))
    cin = jnp.concatenate(halves, axis=-1)
    c = jnp.dot(cin.astype(BF16), wc_ref[...], preferred_element_type=F32)
    merged = (_sigmoid(gate_ref[:, 0:D].astype(F32)) * a
              + _sigmoid(gate_ref[:, D:2 * D].astype(F32)) * b
              + _sigmoid(gate_ref[:, 2 * D:3 * D].astype(F32)) * c)
    mix = jnp.dot(merged.astype(BF16), wo_ref[...], preferred_element_type=F32)
    x1 = _ln(alpha * x_ref[...] + mix, g1_ref[...], b1_ref[...])
    x1b = x1.astype(BF16)
    xb_ref[...] = x1b
    ple = (_sigmoid(jnp.dot(x1b, wpg_ref[...], preferred_element_type=F32))
           * jnp.dot(p_ref[...].astype(BF16), wp_ref[...], preferred_element_type=F32))
    base_ref[...] = alpha * x1 + ple


def _merge(x, h, na, sgu, dil_o, dil_l, p, wa, wb, wc, wo, g1, b1, wpg, wp, alpha, S, tm=512):
    T, D = x.shape
    tm = min(tm, S)
    per_seq = S // tm
    row = lambda w: pl.BlockSpec((tm, w), lambda i: (i, 0))
    full = lambda arr: pl.BlockSpec(arr.shape, lambda i: (0,) * arr.ndim)
    dil = pl.BlockSpec((1, DIL_GW // LANES, tm, LANES), lambda i: (i // per_seq, 0, i % per_seq, 0))
    g1, b1 = g1.reshape(1, D), b1.reshape(1, D)
    return pl.pallas_call(
        functools.partial(_merge_kernel, alpha=alpha, D=D),
        grid=(T // tm,),
        in_specs=[row(D), row(N_BRANCH * D), row(NA_WIDTH), row(SGU_WIDTH),
                  dil, dil, dil, dil, dil, dil,
                  row(p.shape[1]),
                  full(wa), full(wb), full(wc), full(wo), full(g1), full(b1), full(wpg), full(wp)],
        out_specs=[row(D), row(D)],
        out_shape=[jax.ShapeDtypeStruct((T, D), BF16), jax.ShapeDtypeStruct((T, D), F32)],
        compiler_params=_cparams(("parallel",)),
        name="merge",
    )(x, h, na, sgu, dil_o[0], dil_o[1], dil_o[2], dil_l[0], dil_l[1], dil_l[2], p,
      wa, wb, wc, wo, g1, b1, wpg, wp)


_CAND_ROWS = 16 + 7 * 8 + 8


def _cand_index():
    idx = [float(j) for j in range(16)]
    for i in range(1, 8):
        idx += [float(i * 16 + j) for j in range(8)]
    idx += [float(i * 16) for i in range(8, 16)]
    return np.broadcast_to(np.asarray(idx, np.float32)[:, None], (_CAND_ROWS, ROUTE_LW)).copy()


def _extract_top(s, order, n):
    rank = jnp.full(s.shape, 127.0, F32)
    vals = []
    for i in range(n):
        m = jnp.max(s, axis=0, keepdims=True)
        pick = jnp.min(jnp.where(s == m, order, 1e9), axis=0, keepdims=True)
        sel = order == pick
        rank = jnp.where(sel, float(i), rank)
        s = jnp.where(sel, -jnp.inf, s)
        vals.append(m)
    return rank, vals


def _batcher_pairs(n):
    pairs = []

    def merge(lo, hi, r):
        step = r * 2
        if step < hi - lo:
            merge(lo, hi, step)
            merge(lo + r, hi, step)
            pairs.extend((i, i + r) for i in range(lo + r, hi - r, step))
        else:
            pairs.append((lo, lo + r))

    def sort(lo, hi):
        if hi - lo >= 1:
            mid = lo + (hi - lo) // 2
            sort(lo, mid)
            sort(mid + 1, hi)
            merge(lo, hi, 1)

    sort(0, n - 1)
    return pairs


def _sort_desc(xs):
    xs = list(xs)
    for i, j in _batcher_pairs(PEER_TOPK):
        if j < len(xs):
            xs[i], xs[j] = jnp.maximum(xs[i], xs[j]), jnp.minimum(xs[i], xs[j])
    return xs


def _merge_sublanes(xs):
    xs = list(xs)
    n = PEER_TOPK
    for shift in (4, 2, 1):
        t = [jnp.maximum(xs[i], pltpu.roll(xs[n - 1 - i], shift, 0)) for i in range(n)]
        for d in (8, 4, 2, 1):
            for i in range(n):
                if i & d == 0:
                    t[i], t[i + d] = jnp.maximum(t[i], t[i + d]), jnp.minimum(t[i], t[i + d])
        xs = t
    return xs


def _search_bits(x, v):
    c3 = x >= v[7]
    c2 = x >= jnp.where(c3, v[3], v[11])
    c1 = x >= jnp.where(c3, jnp.where(c2, v[1], v[5]), jnp.where(c2, v[9], v[13]))
    c0 = x >= jnp.where(c3,
                        jnp.where(c2, jnp.where(c1, v[0], v[2]), jnp.where(c1, v[4], v[6])),
                        jnp.where(c2, jnp.where(c1, v[8], v[10]), jnp.where(c1, v[12], v[14])))
    return c3, c2, c1, c0


def _pick_by_rank(bits, vals):
    c3, c2, c1, c0 = bits
    l0 = [jnp.where(c0, vals[2 * k], vals[2 * k + 1]) for k in range(8)]
    l1 = [jnp.where(c1, l0[2 * k], l0[2 * k + 1]) for k in range(4)]
    l2 = [jnp.where(c2, l1[2 * k], l1[2 * k + 1]) for k in range(2)]
    return jnp.where(c3, l2[0], l2[1])


def _route_sorted(s1, s2):
    W = s1.shape[1]
    sub_rows = 8
    g1 = [s1[sub_rows * g:sub_rows * (g + 1)] for g in range(PEER_NKEYS // sub_rows)]
    g2 = [s2[sub_rows * g:sub_rows * (g + 1)] for g in range(PEER_NKEYS // sub_rows)]
    v1 = _merge_sublanes(_sort_desc(g1))
    v2 = _merge_sublanes(_sort_desc(g2))
    sub = lax.broadcasted_iota(jnp.int32, (sub_rows, W), 0)

    def pack(vs):
        out = vs[0]
        for s in range(1, sub_rows):
            out = jnp.where(sub == s, vs[s], out)
        return out

    e1r = [jnp.exp(v - v1[0]) for v in v1]
    e2r = [jnp.exp(v - v2[0]) for v in v2]
    v2lo, v2hi, v1hi = pack(v2[:8]), pack(v2[8:]), pack(v1[8:])
    e2lo, e2hi, e1hi = pack(e2r[:8]), pack(e2r[8:]), pack(e1r[8:])
    cand = [v1[0] + v2lo, v1[0] + v2hi] + [v1[i] + v2lo for i in range(1, 8)] + [v1hi + v2[0]]
    ecand = [e1r[0] * e2lo, e1r[0] * e2hi] + [e1r[i] * e2lo for i in range(1, 8)] + [e1hi * e2r[0]]
    neg = jnp.full((sub_rows, W), -jnp.inf, F32)
    tau = _merge_sublanes(_sort_desc(cand) + [neg] * (PEER_TOPK - len(cand)))[PEER_TOPK - 1]
    sel = [jnp.where(c >= tau, 1.0, 0.0) for c in cand]
    colsum = lambda x: jnp.sum(x, axis=0, keepdims=True)
    z = colsum(sum(s * e for s, e in zip(sel, ecand)))
    cnt = [colsum(sel[0] + sel[1])] + [colsum(sel[1 + i]) for i in range(1, 8)] + [sel[9][i:i + 1] for i in range(8)]
    half_minus_cnt = [0.5 - c for c in cnt]
    inv_z = 1.0 / z
    a1, nx1, e2, x2 = [], [], [], []
    n1 = jnp.zeros((sub_rows, W), F32)
    n2 = jnp.zeros((sub_rows, W), F32)
    for x in g1:
        inside = x >= v1[PEER_TOPK - 1]
        nx1.append(jnp.where(inside, _pick_by_rank(_search_bits(x, v1), half_minus_cnt), 1e9))
        a1.append(jnp.where(inside, jnp.exp(x - v1[0]), 0.0) * inv_z)
        n1 = n1 + jnp.where(inside, 1.0, 0.0)
    for x in g2:
        inside = x >= v2[PEER_TOPK - 1]
        c3, c2, c1, c0 = _search_bits(x, v2)
        rank = (jnp.where(c3, 0.0, 8.0) + jnp.where(c2, 0.0, 4.0)) + (jnp.where(c1, 0.0, 2.0) + jnp.where(c0, 0.0, 1.0))
        x2.append(jnp.where(inside, -rank, -127.0))
        e2.append(jnp.exp(x - v2[0]))
        n2 = n2 + jnp.where(inside, 1.0, 0.0)
    dup = jnp.zeros((sub_rows, W), F32)
    for i in range(PEER_TOPK - 1):
        dup = dup + jnp.where(v1[i] == v1[i + 1], 1.0, 0.0) + jnp.where(v2[i] == v2[i + 1], 1.0, 0.0)
    flag = (jnp.abs(colsum(n1) - PEER_TOPK) + jnp.abs(colsum(n2) - PEER_TOPK)
            + jnp.abs(colsum(sum(sel)) - PEER_TOPK) + colsum(dup))
    cat = lambda xs: jnp.concatenate(xs, axis=0)
    return cat(a1), cat(nx1), cat(e2), cat(x2), flag


def _row_tile(ref, h, a, ls):
    row = ref[h, a:a + 1, ls]
    return jnp.broadcast_to(row, (BF16_ROWS, row.shape[1])).astype(BF16)


def _route_chunk(s1, s2, cidx):
    key_order = lax.broadcasted_iota(jnp.int32, s1.shape, 0).astype(F32)
    rank1, v1 = _extract_top(s1, key_order, PEER_TOPK)
    rank2, v2 = _extract_top(s2, key_order, PEER_TOPK)
    v1a = jnp.concatenate(v1, axis=0)
    v2a = jnp.concatenate(v2, axis=0)
    e1r = jnp.exp(v1a - v1[0])
    e2r = jnp.exp(v2a - v2[0])

    def pairs(r1, r2, op):
        blocks = [op(r1[0:1], r2)]
        blocks += [op(r1[i:i + 1], r2[0:8]) for i in range(1, 8)]
        blocks += [op(r1[8:16], r2[0:1])]
        return jnp.concatenate(blocks, axis=0)

    cand = pairs(v1a, v2a, jnp.add)
    ecand = pairs(e1r, e2r, jnp.multiply)
    crank, _ = _extract_top(cand, cidx, PEER_TOPK)
    sel = jnp.where(crank < 100.0, 1.0, 0.0)
    z = jnp.sum(sel * ecand, axis=0, keepdims=True)
    cnt = [jnp.sum(sel[0:16], axis=0, keepdims=True)]
    cnt += [jnp.sum(sel[16 + 8 * (i - 1):16 + 8 * i], axis=0, keepdims=True) for i in range(1, 8)]
    cnt += [sel[72 + i:73 + i] for i in range(8)]
    nx1 = jnp.full(s1.shape, 1e9, F32)
    for i in range(PEER_TOPK):
        nx1 = jnp.where(rank1 == float(i), 0.5 - cnt[i], nx1)
    a1 = jnp.where(rank1 < 100.0, jnp.exp(s1 - v1[0]), 0.0) * (1.0 / z)
    e2 = jnp.exp(s2 - v2[0])
    return a1, nx1, e2, -rank2


def _route_kernel(x_ref, wq_ref, k1_ref, k2_ref, cidx_ref, a1_ref, nx1_ref, e2_ref, x2_ref, q_ref, s1_ref, s2_ref,
                  *, tb):
    hk = PEER_DKEY // 2
    h = pl.program_id(1)

    @pl.when(h == 0)
    def _():
        for hh in range(PEER_HEADS):
            q_ref[hh] = jnp.dot(x_ref[...], wq_ref[:, hh * PEER_DKEY:(hh + 1) * PEER_DKEY],
                                preferred_element_type=F32).astype(BF16)

    s1_ref[...] = _dot_nt(k1_ref[0], q_ref[h, :, :hk])
    s2_ref[...] = _dot_nt(k2_ref[0], q_ref[h, :, hk:])

    def chunk(c, carry):
        ls = pl.ds(pl.multiple_of(c * ROUTE_LW, ROUTE_LW), ROUTE_LW)

        def emit(a1, nx1, e2, x2):
            a1_ref[0, :, ls] = a1
            nx1_ref[0, :, ls] = nx1
            e2_ref[0, :, ls] = e2.astype(e2_ref.dtype)
            x2_ref[0, :, ls] = x2.astype(x2_ref.dtype)

        *fast, flag = _route_sorted(s1_ref[:, ls], s2_ref[:, ls])
        emit(*fast)

        @pl.when(jnp.max(flag) > 0.0)
        def _():
            emit(*_route_chunk(s1_ref[:, ls], s2_ref[:, ls], cidx_ref[...]))

        return carry

    lax.fori_loop(0, tb // ROUTE_LW, chunk, 0)


def _peer_route(xb, wq, k1, k2, tb):
    T, D = xb.shape
    H = PEER_HEADS
    cidx = jnp.asarray(_cand_index())
    out_spec = pl.BlockSpec((1, PEER_NKEYS, tb), lambda i, h: (h, 0, i))
    out_sds = lambda dt: jax.ShapeDtypeStruct((H, PEER_NKEYS, T), dt)
    return pl.pallas_call(
        functools.partial(_route_kernel, tb=tb),
        grid=(T // tb, H),
        in_specs=[pl.BlockSpec((tb, D), lambda i, h: (i, 0)),
                  pl.BlockSpec((D, H * PEER_DKEY), lambda i, h: (0, 0)),
                  pl.BlockSpec((1, PEER_NKEYS, PEER_DKEY // 2), lambda i, h: (h, 0, 0)),
                  pl.BlockSpec((1, PEER_NKEYS, PEER_DKEY // 2), lambda i, h: (h, 0, 0)),
                  pl.BlockSpec((_CAND_ROWS, ROUTE_LW), lambda i, h: (0, 0))],
        out_specs=[out_spec] * 4,
        out_shape=[out_sds(F32), out_sds(F32), out_sds(BF16), out_sds(BF16)],
        scratch_shapes=[pltpu.VMEM((H, tb, PEER_DKEY), BF16),
                        pltpu.VMEM((PEER_NKEYS, tb), F32), pltpu.VMEM((PEER_NKEYS, tb), F32)],
        compiler_params=_cparams(("parallel", "arbitrary")),
        name="peer_route",
    )(xb, wq, k1, k2, cidx)


def _dense_kernel(x_ref, u_ref, vt_ref, a1_ref, nx1_ref, e2_ref, x2_ref, base_ref, g_ref, b_ref,
                  yf_ref, yb_ref, acc_ref, act_ref, pw_ref, *, eb):
    e = pl.program_id(1)
    last = pl.num_programs(1) - 1
    tb = x_ref.shape[0]
    zero = jnp.zeros((), BF16)

    for sb in range(eb // DENSE_SUB):
        rows_e = slice(sb * DENSE_SUB, (sb + 1) * DENSE_SUB)
        act_ref[rows_e, :] = _gelu(_dot_nt(u_ref[rows_e, :], x_ref[...]).astype(BF16))
    nk = PEER_NKEYS // BF16_ROWS

    for al in range(eb // PEER_NKEYS):
        for lt in range(tb // DENSE_LW):
            ls = slice(lt * DENSE_LW, (lt + 1) * DENSE_LW)
            w = [None] * nk
            for h in range(PEER_HEADS):
                nxb = _row_tile(nx1_ref, h, al, ls)
                a1b = _row_tile(a1_ref, h, al, ls)
                for k in range(nk):
                    rows = slice(k * BF16_ROWS, (k + 1) * BF16_ROWS)
                    term = jnp.where(x2_ref[h, rows, ls] >= nxb, e2_ref[h, rows, ls], zero) * a1b
                    w[k] = term if w[k] is None else w[k] + term
            for k in range(nk):
                r0 = al * PEER_NKEYS + k * BF16_ROWS
                pw_ref[r0:r0 + BF16_ROWS, ls] = w[k] * act_ref[r0:r0 + BF16_ROWS, ls]

    d = jnp.dot(vt_ref[...], pw_ref[...], preferred_element_type=F32)

    @pl.when(e == 0)
    def _():
        acc_ref[...] = d

    @pl.when(e > 0)
    def _():
        acc_ref[...] += d

    @pl.when(e == last)
    def _():
        y = _ln(base_ref[...] + acc_ref[...].T, g_ref[...], b_ref[...])
        yf_ref[...] = y
        yb_ref[...] = y.astype(BF16)


def _peer_dense(xb, u, vt, route, base, g2, b2, tb, eb=2048):
    T, D = xb.shape
    NE = u.shape[0]
    H = PEER_HEADS
    rspec = pl.BlockSpec((H, PEER_NKEYS, tb), lambda i, e: (0, 0, i))
    aspec = pl.BlockSpec((H, eb // PEER_NKEYS, tb), lambda i, e: (0, e, i))
    return pl.pallas_call(
        functools.partial(_dense_kernel, eb=eb),
        grid=(T // tb, NE // eb),
        in_specs=[pl.BlockSpec((tb, D), lambda i, e: (i, 0)),
                  pl.BlockSpec((eb, D), lambda i, e: (e, 0)),
                  pl.BlockSpec((D, eb), lambda i, e: (0, e)),
                  aspec, aspec, rspec, rspec,
                  pl.BlockSpec((tb, D), lambda i, e: (i, 0)),
                  pl.BlockSpec((1, D), lambda i, e: (0, 0)),
                  pl.BlockSpec((1, D), lambda i, e: (0, 0))],
        out_specs=[pl.BlockSpec((tb, D), lambda i, e: (i, 0)),
                   pl.BlockSpec((tb, D), lambda i, e: (i, 0))],
        out_shape=[jax.ShapeDtypeStruct((T, D), F32), jax.ShapeDtypeStruct((T, D), BF16)],
        scratch_shapes=[pltpu.VMEM((D, tb), F32), pltpu.VMEM((eb, tb), BF16), pltpu.VMEM((eb, tb), BF16)],
        compiler_params=_cparams(("parallel", "arbitrary")),
        name="peer_dense",
    )(xb, u, vt, route[0], route[1], route[2], route[3], base, g2.reshape(1, D), b2.reshape(1, D))


def _layer_weights(i, w_in, rpb, sgu_ln_g, sgu_ln_b, sgu_w, sgu_b, w_br_a, w_br_b, w_br_c, w_out, ln1_g, ln1_b,
                   peer_wq, peer_k1, peer_k2, peer_u, peer_v, ple_w, ple_gate_w, ln2_g, ln2_b):
    D = w_in.shape[1]
    off_c = 3 * NA_WIDTH + 2 * SGU_WIDTH + 3 * DIL_WIDTH
    off_b = 3 * NA_WIDTH + 2 * SGU_WIDTH

    def group_w(g):
        return jnp.concatenate(
            [w_in[i][:, off_b + part * DIL_WIDTH + g * DIL_GW: off_b + part * DIL_WIDTH + (g + 1) * DIL_GW]
             for part in range(3)], axis=1).astype(BF16)

    w_perm = jnp.concatenate([w_in[i][:, off_c:].astype(BF16), w_in[i][:, :off_b].astype(BF16), group_w(0)], axis=1)
    return dict(
        w_in=w_perm, w_dil={g: group_w(g) for g in range(1, len(DIL_PATTERNS))},
        col_a=N_BRANCH * D, col_b=N_BRANCH * D + 3 * NA_WIDTH,
        col_c=N_BRANCH * D + 3 * NA_WIDTH + 2 * SGU_WIDTH,
        na_bias=_na_bias_table(rpb[i]),
        sgu_ln_g=sgu_ln_g[i], sgu_ln_b=sgu_ln_b[i], sgu_w=sgu_w[i], sgu_b=sgu_b[i],
        wa=w_br_a[i].astype(BF16), wb=w_br_b[i].astype(BF16), wc=w_br_c[i].astype(BF16),
        wo=w_out[i].astype(BF16), ln1_g=ln1_g[i], ln1_b=ln1_b[i],
        wq=peer_wq[i].astype(BF16), k1=peer_k1[i].astype(BF16), k2=peer_k2[i].astype(BF16),
        u=peer_u[i].astype(BF16), vt=peer_v[i].astype(BF16).T,
        wp=ple_w[i].astype(BF16), wpg=ple_gate_w[i].astype(BF16), ln2_g=ln2_g[i], ln2_b=ln2_b[i])


def _trunk(x, p, ln0_g, ln0_b, layers, alpha):
    B, S, D = x.shape
    T = B * S
    tb = min(512, T)
    tabs = _rope_tables(S)
    xf, xb = _ln0(x.reshape(T, D), ln0_g, ln0_b)
    for i, lw in enumerate(layers):
        h = _inproj(xb, lw["w_in"])
        h3 = h.reshape(B, S, -1)
        na = _na_attention(h3, lw["na_bias"], lw["col_a"]).reshape(T, NA_WIDTH)
        sg = _sgu(h3, lw["sgu_ln_g"], lw["sgu_ln_b"], lw["sgu_w"], lw["sgu_b"], lw["col_b"]).reshape(T, SGU_WIDTH)
        dil = [_dil_attention(h3, tabs, 0, B, S, lw["col_c"])]
        for g in range(1, len(DIL_PATTERNS)):
            hd = _inproj_residue_major(xb, lw["w_dil"][g], DIL_PATTERNS[g][1])
            dil.append(_dil_attention(hd, tabs, g, B, S))
        x1b, base = _merge(xf, h, na, sg, [d[0] for d in dil], [d[1] for d in dil], p[i].reshape(T, -1),
                           lw["wa"], lw["wb"], lw["wc"], lw["wo"], lw["ln1_g"], lw["ln1_b"],
                           lw["wpg"], lw["wp"], alpha, S)
        route = _peer_route(x1b, lw["wq"], lw["k1"], lw["k2"], tb)
        xf, xb = _peer_dense(x1b, lw["u"], lw["vt"], route, base, lw["ln2_g"], lw["ln2_b"], tb)
    return xf.reshape(B, S, D)


def kernel(x_prompt, x_sample, p_prompt, p_sample, ln0_g, ln0_b, w_in, rpb, sgu_ln_g, sgu_ln_b, sgu_w, sgu_b, w_br_a, w_br_b, w_br_c, w_out, ln1_g, ln1_b, peer_wq, peer_k1, peer_k2, peer_u, peer_v, ple_w, ple_gate_w, ln2_g, ln2_b):
    depth = w_in.shape[0]
    alpha = (2 * depth) ** 0.25
    layers = [_layer_weights(i, w_in, rpb, sgu_ln_g, sgu_ln_b, sgu_w, sgu_b, w_br_a, w_br_b, w_br_c, w_out,
                             ln1_g, ln1_b, peer_wq, peer_k1, peer_k2, peer_u, peer_v, ple_w, ple_gate_w,
                             ln2_g, ln2_b) for i in range(depth)]
    y_prompt = _trunk(x_prompt, p_prompt, ln0_g, ln0_b, layers, alpha)
    y_sample = _trunk(x_sample, p_sample, ln0_g, ln0_b, layers, alpha)
    return (y_prompt, y_sample)
```

```python
import functools

import numpy as np
import jax
import jax.numpy as jnp
from jax import lax
from jax.experimental import pallas as pl
from jax.experimental.pallas import tpu as pltpu

F32 = jnp.float32
BF16 = jnp.bfloat16

GRID_W = 64
HEAD_DIM = 64
NA_HEADS = 8
NA_WIDTH = NA_HEADS * HEAD_DIM
NA_ROWS = 8
NA_COLS = 16
SGU_GROUPS = 6
SGU_GROUP_CH = 128
SGU_WIDTH = SGU_GROUPS * SGU_GROUP_CH
SGU_CHUNK = 128
DIL_PATTERNS = ((128, 1), (512, 4), (2048, 16))
DIL_HEADS = 4
DIL_GW = DIL_HEADS * HEAD_DIM
DIL_WIDTH = len(DIL_PATTERNS) * DIL_GW
ROPE_THETA = 10000.0
N_BRANCH = 3
PEER_HEADS = 8
PEER_NKEYS = 128
PEER_DKEY = 256
PEER_TOPK = 16
LN_EPS = 1e-5
NEG_BIG = -1e30

LANES = 128
BF16_ROWS = 16
DENSE_SUB = 512
DENSE_LW = 256
DIL_QUERY_BLOCK = 256
NA_ROWS_PER_STEP = 4
ROUTE_LW = 256
VMEM_LIMIT = 48 * 1024 * 1024


def _cparams(sem):
    return pltpu.CompilerParams(dimension_semantics=sem, vmem_limit_bytes=VMEM_LIMIT)


def _ln(x, g, b):
    mu = jnp.mean(x, axis=-1, keepdims=True)
    xc = x - mu
    var = jnp.mean(xc * xc, axis=-1, keepdims=True)
    return xc * lax.rsqrt(var + LN_EPS) * g + b


def _gelu(x):
    return x * (0.5 * (1.0 + jnp.tanh(0.7978845608028654 * (x + 0.044715 * (x * x * x)))))


def _sigmoid(x):
    return 1.0 / (1.0 + jnp.exp(-x))


def _dot_nt(a, b):
    return lax.dot_general(a, b, (((1,), (1,)), ((), ())), preferred_element_type=F32)


def _ln0_kernel(x_ref, g_ref, b_ref, xf_ref, xb_ref):
    y = _ln(x_ref[...], g_ref[...], b_ref[...])
    xf_ref[...] = y
    xb_ref[...] = y.astype(BF16)


def _ln0(x, g, b, tm=512):
    T, D = x.shape
    return pl.pallas_call(
        _ln0_kernel,
        grid=(T // tm,),
        in_specs=[pl.BlockSpec((tm, D), lambda i: (i, 0)),
                  pl.BlockSpec((1, D), lambda i: (0, 0)),
                  pl.BlockSpec((1, D), lambda i: (0, 0))],
        out_specs=[pl.BlockSpec((tm, D), lambda i: (i, 0)),
                   pl.BlockSpec((tm, D), lambda i: (i, 0))],
        out_shape=[jax.ShapeDtypeStruct((T, D), F32), jax.ShapeDtypeStruct((T, D), BF16)],
        compiler_params=_cparams(("parallel",)),
        name="ln0",
    )(x, g.reshape(1, D), b.reshape(1, D))


def _matmul_kernel(x_ref, w_ref, o_ref):
    o_ref[...] = jnp.dot(x_ref[...], w_ref[...], preferred_element_type=F32).astype(o_ref.dtype)


def _inproj(xb, w, tm=1024, tn=2304):
    T, K = xb.shape
    N = w.shape[1]
    tm = min(tm, T)
    return pl.pallas_call(
        _matmul_kernel,
        grid=(T // tm, N // tn),
        in_specs=[pl.BlockSpec((tm, K), lambda i, j: (i, 0)),
                  pl.BlockSpec((K, tn), lambda i, j: (0, j))],
        out_specs=pl.BlockSpec((tm, tn), lambda i, j: (i, j)),
        out_shape=jax.ShapeDtypeStruct((T, N), BF16),
        compiler_params=_cparams(("parallel", "parallel")),
        name="inproj",
    )(xb, w)


def _matmul_residue_major_kernel(x_ref, w_ref, o_ref, acc_ref, *, dil):
    acc = jnp.dot(x_ref[...], w_ref[...], preferred_element_type=F32)
    ntile, tm, _ = acc_ref.shape
    for j in range(ntile):
        acc_ref[j] = acc[:, j * LANES:(j + 1) * LANES]
    for r in range(dil):
        for j in range(ntile):
            c0 = (r * ntile + j) * LANES
            o_ref[:, c0:c0 + LANES] = acc_ref[j, pl.ds(r, tm // dil, stride=dil), :].astype(o_ref.dtype)


def _inproj_residue_major(xb, w, dil, tm=1024):
    T, K = xb.shape
    N = w.shape[1]
    tm = min(tm, T)
    return pl.pallas_call(
        functools.partial(_matmul_residue_major_kernel, dil=dil),
        grid=(T // tm,),
        in_specs=[pl.BlockSpec((tm, K), lambda i: (i, 0)),
                  pl.BlockSpec((K, N), lambda i: (0, 0))],
        out_specs=pl.BlockSpec((tm // dil, dil * N), lambda i: (i, 0)),
        out_shape=jax.ShapeDtypeStruct((T // dil, dil * N), BF16),
        scratch_shapes=[pltpu.VMEM((N // LANES, tm, LANES), F32)],
        compiler_params=_cparams(("parallel",)),
        name=f"inproj_d{dil}",
    )(xb, w)


def _na_kernel(q_ref, k_ref, v_ref, bias_ref, o_ref, s_ref, p_ref, *, rows, rb):
    nk = NA_ROWS * GRID_W
    npair = NA_WIDTH // LANES
    first = lax.broadcasted_iota(jnp.int32, (GRID_W, LANES), 1) < HEAD_DIM
    scale = HEAD_DIM ** -0.5
    units = []
    for rr in range(rb):
        r = pl.program_id(1) * rb + rr
        rs = jnp.clip(r - NA_ROWS // 2, 0, rows - NA_ROWS)
        for p in range(npair):
            units.append((rr * npair + p, slice(rr * GRID_W, (rr + 1) * GRID_W), slice(p * LANES, (p + 1) * LANES),
                          r - rs, pl.multiple_of(rs * GRID_W, GRID_W), p))
    for u, qrows, cols, case, start, p in units:
        qp = q_ref[0, qrows, cols].astype(F32) * scale
        q2 = jnp.concatenate([jnp.where(first, qp, 0.0), jnp.where(first, 0.0, qp)], axis=0).astype(BF16)
        s_ref[u] = (_dot_nt(q2, k_ref[0, pl.ds(start, nk), cols])
                    + bias_ref[case, p * 2 * GRID_W:(p + 1) * 2 * GRID_W, :])
    for u, *_ in units:
        s = s_ref[u]
        e = jnp.exp(s - jnp.max(s, axis=-1, keepdims=True))
        p_ref[u] = (e * (1.0 / jnp.sum(e, axis=-1, keepdims=True))).astype(BF16)
    for u, qrows, cols, case, start, p in units:
        o2 = jnp.dot(p_ref[u], v_ref[0, pl.ds(start, nk), cols], preferred_element_type=F32)
        o_ref[0, qrows, cols] = jnp.where(first, o2[:GRID_W], o2[GRID_W:]).astype(o_ref.dtype)


def _na_bias_table(rpb):
    qcol = np.arange(GRID_W)
    kcol = np.arange(GRID_W)
    cstart = np.clip(qcol - NA_COLS // 2, 0, GRID_W - NA_COLS)
    ok = (kcol[None, :] >= cstart[:, None]) & (kcol[None, :] < cstart[:, None] + NA_COLS)
    dc = np.clip(kcol[None, :] - qcol[:, None], -(NA_COLS - 1), NA_COLS - 1) + (NA_COLS - 1)
    b = jnp.where(ok[None, None], rpb[:, :, dc], NEG_BIG)
    cases = []
    for c in range(NA_ROWS):
        dr = np.arange(NA_ROWS) - c + (NA_ROWS - 1)
        t = b[:, dr]
        cases.append(t.transpose(0, 2, 1, 3).reshape(NA_HEADS * GRID_W, NA_ROWS * GRID_W))
    return jnp.stack(cases, axis=0).astype(F32)


def _na_attention(h3, bias_tab, col0):
    B, S, _ = h3.shape
    rows = S // GRID_W
    cb = col0 // NA_WIDTH
    rb = NA_ROWS_PER_STEP
    return pl.pallas_call(
        functools.partial(_na_kernel, rows=rows, rb=rb),
        grid=(B, rows // rb),
        in_specs=[pl.BlockSpec((1, rb * GRID_W, NA_WIDTH), lambda b, r: (b, r, cb)),
                  pl.BlockSpec((1, S, NA_WIDTH), lambda b, r: (b, 0, cb + 1)),
                  pl.BlockSpec((1, S, NA_WIDTH), lambda b, r: (b, 0, cb + 2)),
                  pl.BlockSpec(bias_tab.shape, lambda b, r: (0, 0, 0))],
        out_specs=pl.BlockSpec((1, rb * GRID_W, NA_WIDTH), lambda b, r: (b, r, 0)),
        out_shape=jax.ShapeDtypeStruct((B, S, NA_WIDTH), BF16),
        scratch_shapes=[pltpu.VMEM((rb * NA_WIDTH // LANES, 2 * GRID_W, NA_ROWS * GRID_W), F32),
                        pltpu.VMEM((rb * NA_WIDTH // LANES, 2 * GRID_W, NA_ROWS * GRID_W), BF16)],
        compiler_params=_cparams(("parallel", "arbitrary")),
        name="na_attn",
    )(h3, h3, h3, bias_tab)


def _sgu_kernel(u_ref, v_ref, g_ref, b_ref, ws_ref, bs_ref, o_ref, *, nchunk):
    for c in range(nchunk):
        rs = slice(c * SGU_CHUNK, (c + 1) * SGU_CHUNK)
        v = _ln(_gelu(v_ref[0, rs, :].astype(F32)), g_ref[...], b_ref[...]).astype(BF16)
        u = _gelu(u_ref[0, rs, :].astype(F32))
        for g in range(SGU_GROUPS):
            cs = slice(g * SGU_GROUP_CH, (g + 1) * SGU_GROUP_CH)
            s = jnp.dot(ws_ref[g], v[:, cs], preferred_element_type=F32) + bs_ref[g]
            o_ref[0, rs, cs] = (u[:, cs] * s).astype(o_ref.dtype)


def _sgu(h3, ln_g, ln_b, ws, bs, col0, tc=512):
    B, S, _ = h3.shape
    tc = min(tc, S)
    cb = col0 // SGU_WIDTH
    bs_b = jnp.broadcast_to(bs[:, :, None], (SGU_GROUPS, SGU_CHUNK, SGU_GROUP_CH)).astype(F32)
    return pl.pallas_call(
        functools.partial(_sgu_kernel, nchunk=tc // SGU_CHUNK),
        grid=(B, S // tc),
        in_specs=[pl.BlockSpec((1, tc, SGU_WIDTH), lambda b, i: (b, i, cb)),
                  pl.BlockSpec((1, tc, SGU_WIDTH), lambda b, i: (b, i, cb + 1)),
                  pl.BlockSpec((1, SGU_WIDTH), lambda b, i: (0, 0)),
                  pl.BlockSpec((1, SGU_WIDTH), lambda b, i: (0, 0)),
                  pl.BlockSpec((SGU_GROUPS, SGU_CHUNK, SGU_CHUNK), lambda b, i: (0, 0, 0)),
                  pl.BlockSpec((SGU_GROUPS, SGU_CHUNK, SGU_GROUP_CH), lambda b, i: (0, 0, 0))],
        out_specs=pl.BlockSpec((1, tc, SGU_WIDTH), lambda b, i: (b, i, 0)),
        out_shape=jax.ShapeDtypeStruct((B, S, SGU_WIDTH), BF16),
        compiler_params=_cparams(("parallel", "parallel")),
        name="sgu",
    )(h3, h3, ln_g.reshape(1, -1), ln_b.reshape(1, -1), ws.astype(BF16), bs_b)


def _rope_tile(x, c, sa, sb):
    return x * c + pltpu.roll(x, LANES - HEAD_DIM // 2, 1) * sa + pltpu.roll(x, HEAD_DIM // 2, 1) * sb


def _dil_kernel(q_ref, k_ref, v_ref, c_ref, sa_ref, sb_ref, o_ref, lse_ref, kr_ref, *, L, QB, KB, half, dil):
    qi = pl.program_id(2)
    ntile = DIL_GW // LANES
    rchunk = min(L, 256)

    @pl.when(qi == 0)
    def _():
        for c0 in range(0, L, rchunk):
            rs = slice(c0, c0 + rchunk)
            for t in range(ntile):
                cs = slice(t * LANES, (t + 1) * LANES)
                kr_ref[rs, cs] = _rope_tile(k_ref[0, rs, cs].astype(F32), c_ref[rs, cs], sa_ref[rs, cs],
                                            sb_ref[rs, cs]).astype(BF16)

    n0 = pl.multiple_of(qi * QB, QB)
    start = pl.multiple_of(jnp.clip(n0 - half, 0, L - KB), 16)
    qn = n0 + lax.broadcasted_iota(jnp.int32, (QB, KB), 0)
    kn = start + lax.broadcasted_iota(jnp.int32, (QB, KB), 1)
    ok = jnp.abs(kn - qn) <= half
    first = lax.broadcasted_iota(jnp.int32, (QB, LANES), 1) < HEAD_DIM
    scale = HEAD_DIM ** -0.5
    if dil == 1:
        orows = slice(None)
    else:
        orows = pl.ds(n0 * dil + pl.program_id(1), QB, stride=dil)
    for t in range(ntile):
        cs = slice(t * LANES, (t + 1) * LANES)
        qr = _rope_tile(q_ref[0, :, cs].astype(F32), c_ref[pl.ds(n0, QB), cs], sa_ref[pl.ds(n0, QB), cs],
                        sb_ref[pl.ds(n0, QB), cs]) * scale
        kt = kr_ref[pl.ds(start, KB), cs]
        vt = v_ref[0, pl.ds(start, KB), cs]
        outs, lses = [], []
        for sub in range(2):
            qm = jnp.where(first, qr, 0.0) if sub == 0 else jnp.where(first, 0.0, qr)
            s = jnp.where(ok, _dot_nt(qm.astype(BF16), kt), NEG_BIG)
            m = jnp.max(s, axis=-1, keepdims=True)
            e = jnp.exp(s - m)
            den = jnp.sum(e, axis=-1, keepdims=True)
            outs.append(jnp.dot(e.astype(BF16), vt, preferred_element_type=F32) * (1.0 / den))
            lses.append(jnp.broadcast_to(m + jnp.log(den), (QB, LANES)))
        o_ref[0, t, orows, :] = jnp.where(first, outs[0], outs[1])
        lse_ref[0, t, orows, :] = jnp.where(first, lses[0], lses[1])


def _rope_tables(S):
    half = HEAD_DIM // 2
    inv = ROPE_THETA ** (-jnp.arange(half, dtype=F32) / half)
    ang = jnp.arange(S).astype(F32)[:, None] * inv[None, :]
    cos, sin = jnp.cos(ang), jnp.sin(ang)
    z = jnp.zeros_like(sin)
    c = jnp.tile(jnp.concatenate([cos, cos], axis=-1), (1, DIL_HEADS))
    sa = jnp.tile(jnp.concatenate([-sin, z], axis=-1), (1, DIL_HEADS))
    sb = jnp.tile(jnp.concatenate([z, sin], axis=-1), (1, DIL_HEADS))
    return c, sa, sb


def _dil_attention(hsrc, tabs, g, B, S, col0=0):
    window, dil = DIL_PATTERNS[g]
    L = S // dil
    half = window // (2 * dil)
    QB = min(L, DIL_QUERY_BLOCK)
    KB = min(L, QB + 2 * half)
    ntile = DIL_GW // LANES
    if dil == 1:
        hv, cpb, qb = hsrc, hsrc.shape[-1] // DIL_GW, col0 // DIL_GW
        out_spec = pl.BlockSpec((1, ntile, QB, LANES), lambda b, r, i: (b, 0, i, 0))
    else:
        hv, cpb, qb = hsrc.reshape(B, L, dil * 3 * DIL_GW), 3, 0
        out_spec = pl.BlockSpec((1, ntile, S, LANES), lambda b, r, i: (b, 0, 0, 0))
    c, sa, sb = [t.reshape(L, dil * DIL_GW) for t in tabs]
    tab_spec = pl.BlockSpec((L, DIL_GW), lambda b, r, i: (0, r))
    o, lse = pl.pallas_call(
        functools.partial(_dil_kernel, L=L, QB=QB, KB=KB, half=half, dil=dil),
        grid=(B, dil, L // QB),
        in_specs=[pl.BlockSpec((1, QB, DIL_GW), lambda b, r, i: (b, i, r * cpb + qb)),
                  pl.BlockSpec((1, L, DIL_GW), lambda b, r, i: (b, 0, r * cpb + qb + 1)),
                  pl.BlockSpec((1, L, DIL_GW), lambda b, r, i: (b, 0, r * cpb + qb + 2)),
                  tab_spec, tab_spec, tab_spec],
        out_specs=[out_spec, out_spec],
        out_shape=[jax.ShapeDtypeStruct((B, ntile, S, LANES), F32)] * 2,
        scratch_shapes=[pltpu.VMEM((L, DIL_GW), BF16)],
        compiler_params=_cparams(("parallel", "arbitrary", "arbitrary")),
        name=f"dil_attn_g{g}",
    )(hv, hv, hv, c, sa, sb)
    return o, lse


def _merge_kernel(x_ref, gate_ref, a_ref, b_ref, o0_ref, o1_ref, o2_ref, l0_ref, l1_ref, l2_ref, p_ref,
                  wa_ref, wb_ref, wc_ref, wo_ref, g1_ref, b1_ref, wpg_ref, wp_ref,
                  xb_ref, base_ref, *, alpha, D):
    a = jnp.dot(a_ref[...], wa_ref[...], preferred_element_type=F32)
    b = jnp.dot(b_ref[...], wb_ref[...], preferred_element_type=F32)
    halves = []
    for t in range(DIL_GW // LANES):
        l0, l1, l2 = l0_ref[0, t], l1_ref[0, t], l2_ref[0, t]
        m = jnp.maximum(jnp.maximum(l0, l1), l2)
        e0, e1, e2 = jnp.exp(l0 - m), jnp.exp(l1 - m), jnp.exp(l2 - m)
        halves.append((e0 * o0_ref[0, t] + e1 * o1_ref[0, t] + e2 * o2_ref[0, t]) * (1.0 / (e0 + e1 + e2)))
    cin = jnp.concatenate(halves, axis=-1)
    c = jnp.dot(cin.astype(BF16), wc_ref[...], preferred_element_type=F32)
    merged = (_sigmoid(gate_ref[:, 0:D].astype(F32)) * a
              + _sigmoid(gate_ref[:, D:2 * D].astype(F32)) * b
              + _sigmoid(gate_ref[:, 2 * D:3 * D].astype(F32)) * c)
    mix = jnp.dot(merged.astype(BF16), wo_ref[...], preferred_element_type=F32)
    x1 = _ln(alpha * x_ref[...] + mix, g1_ref[...], b1_ref[...])
    x1b = x1.astype(BF16)
    xb_ref[...] = x1b
    ple = (_sigmoid(jnp.dot(x1b, wpg_ref[...], preferred_element_type=F32))
           * jnp.dot(p_ref[...].astype(BF16), wp_ref[...], preferred_element_type=F32))
    base_ref[...] = alpha * x1 + ple


def _merge(x, h, na, sgu, dil_o, dil_l, p, wa, wb, wc, wo, g1, b1, wpg, wp, alpha, S, tm=512):
    T, D = x.shape
    tm = min(tm, S)
    per_seq = S // tm
    row = lambda w: pl.BlockSpec((tm, w), lambda i: (i, 0))
    full = lambda arr: pl.BlockSpec(arr.shape, lambda i: (0,) * arr.ndim)
    dil = pl.BlockSpec((1, DIL_GW // LANES, tm, LANES), lambda i: (i // per_seq, 0, i % per_seq, 0))
    g1, b1 = g1.reshape(1, D), b1.reshape(1, D)
    return pl.pallas_call(
        functools.partial(_merge_kernel, alpha=alpha, D=D),
        grid=(T // tm,),
        in_specs=[row(D), row(N_BRANCH * D), row(NA_WIDTH), row(SGU_WIDTH),
                  dil, dil, dil, dil, dil, dil,
                  row(p.shape[1]),
                  full(wa), full(wb), full(wc), full(wo), full(g1), full(b1), full(wpg), full(wp)],
        out_specs=[row(D), row(D)],
        out_shape=[jax.ShapeDtypeStruct((T, D), BF16), jax.ShapeDtypeStruct((T, D), F32)],
        compiler_params=_cparams(("parallel",)),
        name="merge",
    )(x, h, na, sgu, dil_o[0], dil_o[1], dil_o[2], dil_l[0], dil_l[1], dil_l[2], p,
      wa, wb, wc, wo, g1, b1, wpg, wp)


_CAND_ROWS = 16 + 7 * 8 + 8


def _cand_index():
    idx = [float(j) for j in range(16)]
    for i in range(1, 8):
        idx += [float(i * 16 + j) for j in range(8)]
    idx += [float(i * 16) for i in range(8, 16)]
    return np.broadcast_to(np.asarray(idx, np.float32)[:, None], (_CAND_ROWS, ROUTE_LW)).copy()


def _extract_top(s, order, n):
    rank = jnp.full(s.shape, 127.0, F32)
    vals = []
    for i in range(n):
        m = jnp.max(s, axis=0, keepdims=True)
        pick = jnp.min(jnp.where(s == m, order, 1e9), axis=0, keepdims=True)
        sel = order == pick
        rank = jnp.where(sel, float(i), rank)
        s = jnp.where(sel, -jnp.inf, s)
        vals.append(m)
    return rank, vals


def _batcher_pairs(n):
    pairs = []

    def merge(lo, hi, r):
        step = r * 2
        if step < hi - lo:
            merge(lo, hi, step)
            merge(lo + r, hi, step)
            pairs.extend((i, i + r) for i in range(lo + r, hi - r, step))
        else:
            pairs.append((lo, lo + r))

    def sort(lo, hi):
        if hi - lo >= 1:
            mid = lo + (hi - lo) // 2
            sort(lo, mid)
            sort(mid + 1, hi)
            merge(lo, hi, 1)

    sort(0, n - 1)
    return pairs


def _sort_desc(xs):
    xs = list(xs)
    for i, j in _batcher_pairs(PEER_TOPK):
        if j < len(xs):
            xs[i], xs[j] = jnp.maximum(xs[i], xs[j]), jnp.minimum(xs[i], xs[j])
    return xs


def _merge_sublanes(xs):
    xs = list(xs)
    n = PEER_TOPK
    for shift in (4, 2, 1):
        t = [jnp.maximum(xs[i], pltpu.roll(xs[n - 1 - i], shift, 0)) for i in range(n)]
        for d in (8, 4, 2, 1):
            for i in range(n):
                if i & d == 0:
                    t[i], t[i + d] = jnp.maximum(t[i], t[i + d]), jnp.minimum(t[i], t[i + d])
        xs = t
    return xs


def _search_bits(x, v):
    c3 = x >= v[7]
    c2 = x >= jnp.where(c3, v[3], v[11])
    c1 = x >= jnp.where(c3, jnp.where(c2, v[1], v[5]), jnp.where(c2, v[9], v[13]))
    c0 = x >= jnp.where(c3,
                        jnp.where(c2, jnp.where(c1, v[0], v[2]), jnp.where(c1, v[4], v[6])),
                        jnp.where(c2, jnp.where(c1, v[8], v[10]), jnp.where(c1, v[12], v[14])))
    return c3, c2, c1, c0


def _pick_by_rank(bits, vals):
    c3, c2, c1, c0 = bits
    l0 = [jnp.where(c0, vals[2 * k], vals[2 * k + 1]) for k in range(8)]
    l1 = [jnp.where(c1, l0[2 * k], l0[2 * k + 1]) for k in range(4)]
    l2 = [jnp.where(c2, l1[2 * k], l1[2 * k + 1]) for k in range(2)]
    return jnp.where(c3, l2[0], l2[1])


def _route_sorted(s1, s2):
    W = s1.shape[1]
    sub_rows = 8
    g1 = [s1[sub_rows * g:sub_rows * (g + 1)] for g in range(PEER_NKEYS // sub_rows)]
    g2 = [s2[sub_rows * g:sub_rows * (g + 1)] for g in range(PEER_NKEYS // sub_rows)]
    v1 = _merge_sublanes(_sort_desc(g1))
    v2 = _merge_sublanes(_sort_desc(g2))
    sub = lax.broadcasted_iota(jnp.int32, (sub_rows, W), 0)

    def pack(vs):
        out = vs[0]
        for s in range(1, sub_rows):
            out = jnp.where(sub == s, vs[s], out)
        return out

    e1r = [jnp.exp(v - v1[0]) for v in v1]
    e2r = [jnp.exp(v - v2[0]) for v in v2]
    v2lo, v2hi, v1hi = pack(v2[:8]), pack(v2[8:]), pack(v1[8:])
    e2lo, e2hi, e1hi = pack(e2r[:8]), pack(e2r[8:]), pack(e1r[8:])
    cand = [v1[0] + v2lo, v1[0] + v2hi] + [v1[i] + v2lo for i in range(1, 8)] + [v1hi + v2[0]]
    ecand = [e1r[0] * e2lo, e1r[0] * e2hi] + [e1r[i] * e2lo for i in range(1, 8)] + [e1hi * e2r[0]]
    neg = jnp.full((sub_rows, W), -jnp.inf, F32)
    tau = _merge_sublanes(_sort_desc(cand) + [neg] * (PEER_TOPK - len(cand)))[PEER_TOPK - 1]
    sel = [jnp.where(c >= tau, 1.0, 0.0) for c in cand]
    colsum = lambda x: jnp.sum(x, axis=0, keepdims=True)
    z = colsum(sum(s * e for s, e in zip(sel, ecand)))
    cnt = [colsum(sel[0] + sel[1])] + [colsum(sel[1 + i]) for i in range(1, 8)] + [sel[9][i:i + 1] for i in range(8)]
    half_minus_cnt = [0.5 - c for c in cnt]
    inv_z = 1.0 / z
    a1, nx1, e2, x2 = [], [], [], []
    n1 = jnp.zeros((sub_rows, W), F32)
    n2 = jnp.zeros((sub_rows, W), F32)
    for x in g1:
        inside = x >= v1[PEER_TOPK - 1]
        nx1.append(jnp.where(inside, _pick_by_rank(_search_bits(x, v1), half_minus_cnt), 1e9))
        a1.append(jnp.where(inside, jnp.exp(x - v1[0]), 0.0) * inv_z)
        n1 = n1 + jnp.where(inside, 1.0, 0.0)
    for x in g2:
        inside = x >= v2[PEER_TOPK - 1]
        c3, c2, c1, c0 = _search_bits(x, v2)
        rank = (jnp.where(c3, 0.0, 8.0) + jnp.where(c2, 0.0, 4.0)) + (jnp.where(c1, 0.0, 2.0) + jnp.where(c0, 0.0, 1.0))
        x2.append(jnp.where(inside, -rank, -127.0))
        e2.append(jnp.exp(x - v2[0]))
        n2 = n2 + jnp.where(inside, 1.0, 0.0)
    dup = jnp.zeros((sub_rows, W), F32)
    for i in range(PEER_TOPK - 1):
        dup = dup + jnp.where(v1[i] == v1[i + 1], 1.0, 0.0) + jnp.where(v2[i] == v2[i + 1], 1.0, 0.0)
    flag = (jnp.abs(colsum(n1) - PEER_TOPK) + jnp.abs(colsum(n2) - PEER_TOPK)
            + jnp.abs(colsum(sum(sel)) - PEER_TOPK) + colsum(dup))
    cat = lambda xs: jnp.concatenate(xs, axis=0)
    return cat(a1), cat(nx1), cat(e2), cat(x2), flag


def _row_tile(ref, h, a, ls):
    row = ref[h, a:a + 1, ls]
    return jnp.broadcast_to(row, (BF16_ROWS, row.shape[1])).astype(BF16)


def _route_chunk(s1, s2, cidx):
    key_order = lax.broadcasted_iota(jnp.int32, s1.shape, 0).astype(F32)
    rank1, v1 = _extract_top(s1, key_order, PEER_TOPK)
    rank2, v2 = _extract_top(s2, key_order, PEER_TOPK)
    v1a = jnp.concatenate(v1, axis=0)
    v2a = jnp.concatenate(v2, axis=0)
    e1r = jnp.exp(v1a - v1[0])
    e2r = jnp.exp(v2a - v2[0])

    def pairs(r1, r2, op):
        blocks = [op(r1[0:1], r2)]
        blocks += [op(r1[i:i + 1], r2[0:8]) for i in range(1, 8)]
        blocks += [op(r1[8:16], r2[0:1])]
        return jnp.concatenate(blocks, axis=0)

    cand = pairs(v1a, v2a, jnp.add)
    ecand = pairs(e1r, e2r, jnp.multiply)
    crank, _ = _extract_top(cand, cidx, PEER_TOPK)
    sel = jnp.where(crank < 100.0, 1.0, 0.0)
    z = jnp.sum(sel * ecand, axis=0, keepdims=True)
    cnt = [jnp.sum(sel[0:16], axis=0, keepdims=True)]
    cnt += [jnp.sum(sel[16 + 8 * (i - 1):16 + 8 * i], axis=0, keepdims=True) for i in range(1, 8)]
    cnt += [sel[72 + i:73 + i] for i in range(8)]
    nx1 = jnp.full(s1.shape, 1e9, F32)
    for i in range(PEER_TOPK):
        nx1 = jnp.where(rank1 == float(i), 0.5 - cnt[i], nx1)
    a1 = jnp.where(rank1 < 100.0, jnp.exp(s1 - v1[0]), 0.0) * (1.0 / z)
    e2 = jnp.exp(s2 - v2[0])
    return a1, nx1, e2, -rank2


def _route_kernel(x_ref, wq_ref, k1_ref, k2_ref, cidx_ref, a1_ref, nx1_ref, e2_ref, x2_ref, q_ref, s1_ref, s2_ref,
                  *, tb):
    hk = PEER_DKEY // 2
    h = pl.program_id(1)

    @pl.when(h == 0)
    def _():
        for hh in range(PEER_HEADS):
            q_ref[hh] = jnp.dot(x_ref[...], wq_ref[:, hh * PEER_DKEY:(hh + 1) * PEER_DKEY],
                                preferred_element_type=F32).astype(BF16)

    s1_ref[...] = _dot_nt(k1_ref[0], q_ref[h, :, :hk])
    s2_ref[...] = _dot_nt(k2_ref[0], q_ref[h, :, hk:])

    def chunk(c, carry):
        ls = pl.ds(pl.multiple_of(c * ROUTE_LW, ROUTE_LW), ROUTE_LW)

        def emit(a1, nx1, e2, x2):
            a1_ref[0, :, ls] = a1
            nx1_ref[0, :, ls] = nx1
            e2_ref[0, :, ls] = e2.astype(e2_ref.dtype)
            x2_ref[0, :, ls] = x2.astype(x2_ref.dtype)

        *fast, flag = _route_sorted(s1_ref[:, ls], s2_ref[:, ls])
        emit(*fast)

        @pl.when(jnp.max(flag) > 0.0)
        def _():
            emit(*_route_chunk(s1_ref[:, ls], s2_ref[:, ls], cidx_ref[...]))

        return carry

    lax.fori_loop(0, tb // ROUTE_LW, chunk, 0)


def _peer_route(xb, wq, k1, k2, tb):
    T, D = xb.shape
    H = PEER_HEADS
    cidx = jnp.asarray(_cand_index())
    out_spec = pl.BlockSpec((1, PEER_NKEYS, tb), lambda i, h: (h, 0, i))
    out_sds = lambda dt: jax.ShapeDtypeStruct((H, PEER_NKEYS, T), dt)
    return pl.pallas_call(
        functools.partial(_route_kernel, tb=tb),
        grid=(T // tb, H),
        in_specs=[pl.BlockSpec((tb, D), lambda i, h: (i, 0)),
                  pl.BlockSpec((D, H * PEER_DKEY), lambda i, h: (0, 0)),
                  pl.BlockSpec((1, PEER_NKEYS, PEER_DKEY // 2), lambda i, h: (h, 0, 0)),
                  pl.BlockSpec((1, PEER_NKEYS, PEER_DKEY // 2), lambda i, h: (h, 0, 0)),
                  pl.BlockSpec((_CAND_ROWS, ROUTE_LW), lambda i, h: (0, 0))],
        out_specs=[out_spec] * 4,
        out_shape=[out_sds(F32), out_sds(F32), out_sds(BF16), out_sds(BF16)],
        scratch_shapes=[pltpu.VMEM((H, tb, PEER_DKEY), BF16),
                        pltpu.VMEM((PEER_NKEYS, tb), F32), pltpu.VMEM((PEER_NKEYS, tb), F32)],
        compiler_params=_cparams(("parallel", "arbitrary")),
        name="peer_route",
    )(xb, wq, k1, k2, cidx)


def _dense_kernel(x_ref, u_ref, vt_ref, a1_ref, nx1_ref, e2_ref, x2_ref, base_ref, g_ref, b_ref,
                  yf_ref, yb_ref, acc_ref, act_ref, pw_ref, *, eb):
    e = pl.program_id(1)
    last = pl.num_programs(1) - 1
    tb = x_ref.shape[0]
    zero = jnp.zeros((), BF16)

    for sb in range(eb // DENSE_SUB):
        rows_e = slice(sb * DENSE_SUB, (sb + 1) * DENSE_SUB)
        act_ref[rows_e, :] = _gelu(_dot_nt(u_ref[rows_e, :], x_ref[...]).astype(BF16))
    nk = PEER_NKEYS // BF16_ROWS

    for al in range(eb // PEER_NKEYS):
        for lt in range(tb // DENSE_LW):
            ls = slice(lt * DENSE_LW, (lt + 1) * DENSE_LW)
            w = [None] * nk
            for h in range(PEER_HEADS):
                nxb = _row_tile(nx1_ref, h, al, ls)
                a1b = _row_tile(a1_ref, h, al, ls)
                for k in range(nk):
                    rows = slice(k * BF16_ROWS, (k + 1) * BF16_ROWS)
                    term = jnp.where(x2_ref[h, rows, ls] >= nxb, e2_ref[h, rows, ls], zero) * a1b
                    w[k] = term if w[k] is None else w[k] + term
            for k in range(nk):
                r0 = al * PEER_NKEYS + k * BF16_ROWS
                pw_ref[r0:r0 + BF16_ROWS, ls] = w[k] * act_ref[r0:r0 + BF16_ROWS, ls]

    d = jnp.dot(vt_ref[...], pw_ref[...], preferred_element_type=F32)

    @pl.when(e == 0)
    def _():
        acc_ref[...] = d

    @pl.when(e > 0)
    def _():
        acc_ref[...] += d

    @pl.when(e == last)
    def _():
        y = _ln(base_ref[...] + acc_ref[...].T, g_ref[...], b_ref[...])
        yf_ref[...] = y
        yb_ref[...] = y.astype(BF16)


def _peer_dense(xb, u, vt, route, base, g2, b2, tb, eb=2048):
    T, D = xb.shape
    NE = u.shape[0]
    H = PEER_HEADS
    rspec = pl.BlockSpec((H, PEER_NKEYS, tb), lambda i, e: (0, 0, i))
    aspec = pl.BlockSpec((H, eb // PEER_NKEYS, tb), lambda i, e: (0, e, i))
    return pl.pallas_call(
        functools.partial(_dense_kernel, eb=eb),
        grid=(T // tb, NE // eb),
        in_specs=[pl.BlockSpec((tb, D), lambda i, e: (i, 0)),
                  pl.BlockSpec((eb, D), lambda i, e: (e, 0)),
                  pl.BlockSpec((D, eb), lambda i, e: (0, e)),
                  aspec, aspec, rspec, rspec,
                  pl.BlockSpec((tb, D), lambda i, e: (i, 0)),
                  pl.BlockSpec((1, D), lambda i, e: (0, 0)),
                  pl.BlockSpec((1, D), lambda i, e: (0, 0))],
        out_specs=[pl.BlockSpec((tb, D), lambda i, e: (i, 0)),
                   pl.BlockSpec((tb, D), lambda i, e: (i, 0))],
        out_shape=[jax.ShapeDtypeStruct((T, D), F32), jax.ShapeDtypeStruct((T, D), BF16)],
        scratch_shapes=[pltpu.VMEM((D, tb), F32), pltpu.VMEM((eb, tb), BF16), pltpu.VMEM((eb, tb), BF16)],
        compiler_params=_cparams(("parallel", "arbitrary")),
        name="peer_dense",
    )(xb, u, vt, route[0], route[1], route[2], route[3], base, g2.reshape(1, D), b2.reshape(1, D))


def _layer_weights(i, w_in, rpb, sgu_ln_g, sgu_ln_b, sgu_w, sgu_b, w_br_a, w_br_b, w_br_c, w_out, ln1_g, ln1_b,
                   peer_wq, peer_k1, peer_k2, peer_u, peer_v, ple_w, ple_gate_w, ln2_g, ln2_b):
    D = w_in.shape[1]
    off_c = 3 * NA_WIDTH + 2 * SGU_WIDTH + 3 * DIL_WIDTH
    off_b = 3 * NA_WIDTH + 2 * SGU_WIDTH

    def group_w(g):
        return jnp.concatenate(
            [w_in[i][:, off_b + part * DIL_WIDTH + g * DIL_GW: off_b + part * DIL_WIDTH + (g + 1) * DIL_GW]
             for part in range(3)], axis=1).astype(BF16)

    w_perm = jnp.concatenate([w_in[i][:, off_c:].astype(BF16), w_in[i][:, :off_b].astype(BF16), group_w(0)], axis=1)
    return dict(
        w_in=w_perm, w_dil={g: group_w(g) for g in range(1, len(DIL_PATTERNS))},
        col_a=N_BRANCH * D, col_b=N_BRANCH * D + 3 * NA_WIDTH,
        col_c=N_BRANCH * D + 3 * NA_WIDTH + 2 * SGU_WIDTH,
        na_bias=_na_bias_table(rpb[i]),
        sgu_ln_g=sgu_ln_g[i], sgu_ln_b=sgu_ln_b[i], sgu_w=sgu_w[i], sgu_b=sgu_b[i],
        wa=w_br_a[i].astype(BF16), wb=w_br_b[i].astype(BF16), wc=w_br_c[i].astype(BF16),
        wo=w_out[i].astype(BF16), ln1_g=ln1_g[i], ln1_b=ln1_b[i],
        wq=peer_wq[i].astype(BF16), k1=peer_k1[i].astype(BF16), k2=peer_k2[i].astype(BF16),
        u=peer_u[i].astype(BF16), vt=peer_v[i].astype(BF16).T,
        wp=ple_w[i].astype(BF16), wpg=ple_gate_w[i].astype(BF16), ln2_g=ln2_g[i], ln2_b=ln2_b[i])


def _trunk(x, p, ln0_g, ln0_b, layers, alpha):
    B, S, D = x.shape
    T = B * S
    tb = min(512, T)
    tabs = _rope_tables(S)
    xf, xb = _ln0(x.reshape(T, D), ln0_g, ln0_b)
    for i, lw in enumerate(layers):
        h = _inproj(xb, lw["w_in"])
        h3 = h.reshape(B, S, -1)
        na = _na_attention(h3, lw["na_bias"], lw["col_a"]).reshape(T, NA_WIDTH)
        sg = _sgu(h3, lw["sgu_ln_g"], lw["sgu_ln_b"], lw["sgu_w"], lw["sgu_b"], lw["col_b"]).reshape(T, SGU_WIDTH)
        dil = [_dil_attention(h3, tabs, 0, B, S, lw["col_c"])]
        for g in range(1, len(DIL_PATTERNS)):
            hd = _inproj_residue_major(xb, lw["w_dil"][g], DIL_PATTERNS[g][1])
            dil.append(_dil_attention(hd, tabs, g, B, S))
        x1b, base = _merge(xf, h, na, sg, [d[0] for d in dil], [d[1] for d in dil], p[i].reshape(T, -1),
                           lw["wa"], lw["wb"], lw["wc"], lw["wo"], lw["ln1_g"], lw["ln1_b"],
                           lw["wpg"], lw["wp"], alpha, S)
        route = _peer_route(x1b, lw["wq"], lw["k1"], lw["k2"], tb)
        xf, xb = _peer_dense(x1b, lw["u"], lw["vt"], route, base, lw["ln2_g"], lw["ln2_b"], tb)
    return xf.reshape(B, S, D)


def kernel(x_prompt, x_sample, p_prompt, p_sample, ln0_g, ln0_b, w_in, rpb, sgu_ln_g, sgu_ln_b, sgu_w, sgu_b, w_br_a, w_br_b, w_br_c, w_out, ln1_g, ln1_b, peer_wq, peer_k1, peer_k2, peer_u, peer_v, ple_w, ple_gate_w, ln2_g, ln2_b):
    depth = w_in.shape[0]
    alpha = (2 * depth) ** 0.25
    layers = [_layer_weights(i, w_in, rpb, sgu_ln_g, sgu_ln_b, sgu_w, sgu_b, w_br_a, w_br_b, w_br_c, w_out,
                             ln1_g, ln1_b, peer_wq, peer_k1, peer_k2, peer_u, peer_v, ple_w, ple_gate_w,
                             ln2_g, ln2_b) for i in range(depth)]
    y_prompt = _trunk(x_prompt, p_prompt, ln0_g, ln0_b, layers, alpha)
    y_sample = _trunk(x_sample, p_sample, ln0_g, ln0_b, layers, alpha)
    return (y_prompt, y_sample)
```

```python
import functools

import numpy as np
import jax
import jax.numpy as jnp
from jax import lax
from jax.experimental import pallas as pl
from jax.experimental.pallas import tpu as pltpu

F32 = jnp.float32
BF16 = jnp.bfloat16

GRID_W = 64
HEAD_DIM = 64
NA_HEADS = 8
NA_WIDTH = NA_HEADS * HEAD_DIM
NA_ROWS = 8
NA_COLS = 16
SGU_GROUPS = 6
SGU_GROUP_CH = 128
SGU_WIDTH = SGU_GROUPS * SGU_GROUP_CH
SGU_CHUNK = 128
DIL_PATTERNS = ((128, 1), (512, 4), (2048, 16))
DIL_HEADS = 4
DIL_GW = DIL_HEADS * HEAD_DIM
DIL_WIDTH = len(DIL_PATTERNS) * DIL_GW
ROPE_THETA = 10000.0
N_BRANCH = 3
PEER_HEADS = 8
PEER_NKEYS = 128
PEER_DKEY = 256
PEER_TOPK = 16
LN_EPS = 1e-5
NEG_BIG = -1e30

LANES = 128
BF16_ROWS = 16
DENSE_SUB = 512
DENSE_LW = 256
DIL_QUERY_BLOCK = 256
NA_ROWS_PER_STEP = 4
ROUTE_LW = 256
VMEM_LIMIT = 48 * 1024 * 1024


def _cparams(sem):
    return pltpu.CompilerParams(dimension_semantics=sem, vmem_limit_bytes=VMEM_LIMIT)


def _ln(x, g, b):
    mu = jnp.mean(x, axis=-1, keepdims=True)
    xc = x - mu
    var = jnp.mean(xc * xc, axis=-1, keepdims=True)
    return xc * lax.rsqrt(var + LN_EPS) * g + b


def _gelu(x):
    return x * (0.5 * (1.0 + jnp.tanh(0.7978845608028654 * (x + 0.044715 * (x * x * x)))))


def _sigmoid(x):
    return 1.0 / (1.0 + jnp.exp(-x))


def _dot_nt(a, b):
    return lax.dot_general(a, b, (((1,), (1,)), ((), ())), preferred_element_type=F32)


def _ln0_kernel(x_ref, g_ref, b_ref, xf_ref, xb_ref):
    y = _ln(x_ref[...], g_ref[...], b_ref[...])
    xf_ref[...] = y
    xb_ref[...] = y.astype(BF16)


def _ln0(x, g, b, tm=512):
    T, D = x.shape
    return pl.pallas_call(
        _ln0_kernel,
        grid=(T // tm,),
        in_specs=[pl.BlockSpec((tm, D), lambda i: (i, 0)),
                  pl.BlockSpec((1, D), lambda i: (0, 0)),
                  pl.BlockSpec((1, D), lambda i: (0, 0))],
        out_specs=[pl.BlockSpec((tm, D), lambda i: (i, 0)),
                   pl.BlockSpec((tm, D), lambda i: (i, 0))],
        out_shape=[jax.ShapeDtypeStruct((T, D), F32), jax.ShapeDtypeStruct((T, D), BF16)],
        compiler_params=_cparams(("parallel",)),
        name="ln0",
    )(x, g.reshape(1, D), b.reshape(1, D))


def _matmul_kernel(x_ref, w_ref, o_ref):
    o_ref[...] = jnp.dot(x_ref[...], w_ref[...], preferred_element_type=F32).astype(o_ref.dtype)


def _inproj(xb, w, tm=1024, tn=2304):
    T, K = xb.shape
    N = w.shape[1]
    tm = min(tm, T)
    return pl.pallas_call(
        _matmul_kernel,
        grid=(T // tm, N // tn),
        in_specs=[pl.BlockSpec((tm, K), lambda i, j: (i, 0)),
                  pl.BlockSpec((K, tn), lambda i, j: (0, j))],
        out_specs=pl.BlockSpec((tm, tn), lambda i, j: (i, j)),
        out_shape=jax.ShapeDtypeStruct((T, N), BF16),
        compiler_params=_cparams(("parallel", "parallel")),
        name="inproj",
    )(xb, w)


def _matmul_residue_major_kernel(x_ref, w_ref, o_ref, acc_ref, *, dil):
    acc = jnp.dot(x_ref[...], w_ref[...], preferred_element_type=F32)
    ntile, tm, _ = acc_ref.shape
    for j in range(ntile):
        acc_ref[j] = acc[:, j * LANES:(j + 1) * LANES]
    for r in range(dil):
        for j in range(ntile):
            c0 = (r * ntile + j) * LANES
            o_ref[:, c0:c0 + LANES] = acc_ref[j, pl.ds(r, tm // dil, stride=dil), :].astype(o_ref.dtype)


def _inproj_residue_major(xb, w, dil, tm=1024):
    T, K = xb.shape
    N = w.shape[1]
    tm = min(tm, T)
    return pl.pallas_call(
        functools.partial(_matmul_residue_major_kernel, dil=dil),
        grid=(T // tm,),
        in_specs=[pl.BlockSpec((tm, K), lambda i: (i, 0)),
                  pl.BlockSpec((K, N), lambda i: (0, 0))],
        out_specs=pl.BlockSpec((tm // dil, dil * N), lambda i: (i, 0)),
        out_shape=jax.ShapeDtypeStruct((T // dil, dil * N), BF16),
        scratch_shapes=[pltpu.VMEM((N // LANES, tm, LANES), F32)],
        compiler_params=_cparams(("parallel",)),
        name=f"inproj_d{dil}",
    )(xb, w)


def _na_kernel(q_ref, k_ref, v_ref, bias_ref, o_ref, s_ref, p_ref, *, rows, rb):
    nk = NA_ROWS * GRID_W
    npair = NA_WIDTH // LANES
    first = lax.broadcasted_iota(jnp.int32, (GRID_W, LANES), 1) < HEAD_DIM
    scale = HEAD_DIM ** -0.5
    units = []
    for rr in range(rb):
        r = pl.program_id(1) * rb + rr
        rs = jnp.clip(r - NA_ROWS // 2, 0, rows - NA_ROWS)
        for p in range(npair):
            units.append((rr * npair + p, slice(rr * GRID_W, (rr + 1) * GRID_W), slice(p * LANES, (p + 1) * LANES),
                          r - rs, pl.multiple_of(rs * GRID_W, GRID_W), p))
    for u, qrows, cols, case, start, p in units:
        qp = q_ref[0, qrows, cols].astype(F32) * scale
        q2 = jnp.concatenate([jnp.where(first, qp, 0.0), jnp.where(first, 0.0, qp)], axis=0).astype(BF16)
        s_ref[u] = (_dot_nt(q2, k_ref[0, pl.ds(start, nk), cols])
                    + bias_ref[case, p * 2 * GRID_W:(p + 1) * 2 * GRID_W, :])
    for u, *_ in units:
        s = s_ref[u]
        e = jnp.exp(s - jnp.max(s, axis=-1, keepdims=True))
        p_ref[u] = (e * (1.0 / jnp.sum(e, axis=-1, keepdims=True))).astype(BF16)
    for u, qrows, cols, case, start, p in units:
        o2 = jnp.dot(p_ref[u], v_ref[0, pl.ds(start, nk), cols], preferred_element_type=F32)
        o_ref[0, qrows, cols] = jnp.where(first, o2[:GRID_W], o2[GRID_W:]).astype(o_ref.dtype)


def _na_bias_table(rpb):
    qcol = np.arange(GRID_W)
    kcol = np.arange(GRID_W)
    cstart = np.clip(qcol - NA_COLS // 2, 0, GRID_W - NA_COLS)
    ok = (kcol[None, :] >= cstart[:, None]) & (kcol[None, :] < cstart[:, None] + NA_COLS)
    dc = np.clip(kcol[None, :] - qcol[:, None], -(NA_COLS - 1), NA_COLS - 1) + (NA_COLS - 1)
    b = jnp.where(ok[None, None], rpb[:, :, dc], NEG_BIG)
    cases = []
    for c in range(NA_ROWS):
        dr = np.arange(NA_ROWS) - c + (NA_ROWS - 1)
        t = b[:, dr]
        cases.append(t.transpose(0, 2, 1, 3).reshape(NA_HEADS * GRID_W, NA_ROWS * GRID_W))
    return jnp.stack(cases, axis=0).astype(F32)


def _na_attention(h3, bias_tab, col0):
    B, S, _ = h3.shape
    rows = S // GRID_W
    cb = col0 // NA_WIDTH
    rb = NA_ROWS_PER_STEP
    return pl.pallas_call(
        functools.partial(_na_kernel, rows=rows, rb=rb),
        grid=(B, rows // rb),
        in_specs=[pl.BlockSpec((1, rb * GRID_W, NA_WIDTH), lambda b, r: (b, r, cb)),
                  pl.BlockSpec((1, S, NA_WIDTH), lambda b, r: (b, 0, cb + 1)),
                  pl.BlockSpec((1, S, NA_WIDTH), lambda b, r: (b, 0, cb + 2)),
                  pl.BlockSpec(bias_tab.shape, lambda b, r: (0, 0, 0))],
        out_specs=pl.BlockSpec((1, rb * GRID_W, NA_WIDTH), lambda b, r: (b, r, 0)),
        out_shape=jax.ShapeDtypeStruct((B, S, NA_WIDTH), BF16),
        scratch_shapes=[pltpu.VMEM((rb * NA_WIDTH // LANES, 2 * GRID_W, NA_ROWS * GRID_W), F32),
                        pltpu.VMEM((rb * NA_WIDTH // LANES, 2 * GRID_W, NA_ROWS * GRID_W), BF16)],
        compiler_params=_cparams(("parallel", "arbitrary")),
        name="na_attn",
    )(h3, h3, h3, bias_tab)


def _sgu_kernel(u_ref, v_ref, g_ref, b_ref, ws_ref, bs_ref, o_ref, *, nchunk):
    for c in range(nchunk):
        rs = slice(c * SGU_CHUNK, (c + 1) * SGU_CHUNK)
        v = _ln(_gelu(v_ref[0, rs, :].astype(F32)), g_ref[...], b_ref[...]).astype(BF16)
        u = _gelu(u_ref[0, rs, :].astype(F32))
        for g in range(SGU_GROUPS):
            cs = slice(g * SGU_GROUP_CH, (g + 1) * SGU_GROUP_CH)
            s = jnp.dot(ws_ref[g], v[:, cs], preferred_element_type=F32) + bs_ref[g]
            o_ref[0, rs, cs] = (u[:, cs] * s).astype(o_ref.dtype)


def _sgu(h3, ln_g, ln_b, ws, bs, col0, tc=512):
    B, S, _ = h3.shape
    tc = min(tc, S)
    cb = col0 // SGU_WIDTH
    bs_b = jnp.broadcast_to(bs[:, :, None], (SGU_GROUPS, SGU_CHUNK, SGU_GROUP_CH)).astype(F32)
    return pl.pallas_call(
        functools.partial(_sgu_kernel, nchunk=tc // SGU_CHUNK),
        grid=(B, S // tc),
        in_specs=[pl.BlockSpec((1, tc, SGU_WIDTH), lambda b, i: (b, i, cb)),
                  pl.BlockSpec((1, tc, SGU_WIDTH), lambda b, i: (b, i, cb + 1)),
                  pl.BlockSpec((1, SGU_WIDTH), lambda b, i: (0, 0)),
                  pl.BlockSpec((1, SGU_WIDTH), lambda b, i: (0, 0)),
                  pl.BlockSpec((SGU_GROUPS, SGU_CHUNK, SGU_CHUNK), lambda b, i: (0, 0, 0)),
                  pl.BlockSpec((SGU_GROUPS, SGU_CHUNK, SGU_GROUP_CH), lambda b, i: (0, 0, 0))],
        out_specs=pl.BlockSpec((1, tc, SGU_WIDTH), lambda b, i: (b, i, 0)),
        out_shape=jax.ShapeDtypeStruct((B, S, SGU_WIDTH), BF16),
        compiler_params=_cparams(("parallel", "parallel")),
        name="sgu",
    )(h3, h3, ln_g.reshape(1, -1), ln_b.reshape(1, -1), ws.astype(BF16), bs_b)


def _rope_tile(x, c, sa, sb):
    return x * c + pltpu.roll(x, LANES - HEAD_DIM // 2, 1) * sa + pltpu.roll(x, HEAD_DIM // 2, 1) * sb


def _dil_kernel(q_ref, k_ref, v_ref, c_ref, sa_ref, sb_ref, o_ref, lse_ref, kr_ref, s_ref, p_ref, l_ref,
                *, L, QB, KB, half, dil):
    qi = pl.program_id(2)
    ntile = DIL_GW // LANES
    rchunk = min(L, 256)

    @pl.when(qi == 0)
    def _():
        for c0 in range(0, L, rchunk):
            rs = slice(c0, c0 + rchunk)
            for t in range(ntile):
                cs = slice(t * LANES, (t + 1) * LANES)
                kr_ref[rs, cs] = _rope_tile(k_ref[0, rs, cs].astype(F32), c_ref[rs, cs], sa_ref[rs, cs],
                                            sb_ref[rs, cs]).astype(BF16)

    n0 = pl.multiple_of(qi * QB, QB)
    start = pl.multiple_of(jnp.clip(n0 - half, 0, L - KB), 16)
    qn = n0 + lax.broadcasted_iota(jnp.int32, (QB, KB), 0)
    kn = start + lax.broadcasted_iota(jnp.int32, (QB, KB), 1)
    ok = jnp.abs(kn - qn) <= half
    first = lax.broadcasted_iota(jnp.int32, (QB, LANES), 1) < HEAD_DIM
    scale = HEAD_DIM ** -0.5
    if dil == 1:
        orows = slice(None)
    else:
        orows = pl.ds(n0 * dil + pl.program_id(1), QB, stride=dil)
    for t in range(ntile):
        cs = slice(t * LANES, (t + 1) * LANES)
        qr = _rope_tile(q_ref[0, :, cs].astype(F32), c_ref[pl.ds(n0, QB), cs], sa_ref[pl.ds(n0, QB), cs],
                        sb_ref[pl.ds(n0, QB), cs]) * scale
        kt = kr_ref[pl.ds(start, KB), cs]
        for sub in range(2):
            qm = jnp.where(first, qr, 0.0) if sub == 0 else jnp.where(first, 0.0, qr)
            s_ref[2 * t + sub] = jnp.where(ok, _dot_nt(qm.astype(BF16), kt), NEG_BIG)
    for u in range(2 * ntile):
        s = s_ref[u]
        m = jnp.max(s, axis=-1, keepdims=True)
        e = jnp.exp(s - m)
        den = jnp.sum(e, axis=-1, keepdims=True)
        p_ref[u] = (e * (1.0 / den)).astype(BF16)
        l_ref[u] = jnp.broadcast_to(m + jnp.log(den), (QB, LANES))
    for t in range(ntile):
        vt = v_ref[0, pl.ds(start, KB), t * LANES:(t + 1) * LANES]
        outs = [jnp.dot(p_ref[2 * t + sub], vt, preferred_element_type=F32) for sub in range(2)]
        o_ref[0, t, orows, :] = jnp.where(first, outs[0], outs[1])
        lse_ref[0, t, orows, :] = jnp.where(first, l_ref[2 * t], l_ref[2 * t + 1])


def _rope_tables(S):
    half = HEAD_DIM // 2
    inv = ROPE_THETA ** (-jnp.arange(half, dtype=F32) / half)
    ang = jnp.arange(S).astype(F32)[:, None] * inv[None, :]
    cos, sin = jnp.cos(ang), jnp.sin(ang)
    z = jnp.zeros_like(sin)
    c = jnp.tile(jnp.concatenate([cos, cos], axis=-1), (1, DIL_HEADS))
    sa = jnp.tile(jnp.concatenate([-sin, z], axis=-1), (1, DIL_HEADS))
    sb = jnp.tile(jnp.concatenate([z, sin], axis=-1), (1, DIL_HEADS))
    return c, sa, sb


def _dil_attention(hsrc, tabs, g, B, S, col0=0):
    window, dil = DIL_PATTERNS[g]
    L = S // dil
    half = window // (2 * dil)
    QB = min(L, DIL_QUERY_BLOCK)
    KB = min(L, QB + 2 * half)
    ntile = DIL_GW // LANES
    if dil == 1:
        hv, cpb, qb = hsrc, hsrc.shape[-1] // DIL_GW, col0 // DIL_GW
        out_spec = pl.BlockSpec((1, ntile, QB, LANES), lambda b, r, i: (b, 0, i, 0))
    else:
        hv, cpb, qb = hsrc.reshape(B, L, dil * 3 * DIL_GW), 3, 0
        out_spec = pl.BlockSpec((1, ntile, S, LANES), lambda b, r, i: (b, 0, 0, 0))
    c, sa, sb = [t.reshape(L, dil * DIL_GW) for t in tabs]
    tab_spec = pl.BlockSpec((L, DIL_GW), lambda b, r, i: (0, r))
    o, lse = pl.pallas_call(
        functools.partial(_dil_kernel, L=L, QB=QB, KB=KB, half=half, dil=dil),
        grid=(B, dil, L // QB),
        in_specs=[pl.BlockSpec((1, QB, DIL_GW), lambda b, r, i: (b, i, r * cpb + qb)),
                  pl.BlockSpec((1, L, DIL_GW), lambda b, r, i: (b, 0, r * cpb + qb + 1)),
                  pl.BlockSpec((1, L, DIL_GW), lambda b, r, i: (b, 0, r * cpb + qb + 2)),
                  tab_spec, tab_spec, tab_spec],
        out_specs=[out_spec, out_spec],
        out_shape=[jax.ShapeDtypeStruct((B, ntile, S, LANES), F32)] * 2,
        scratch_shapes=[pltpu.VMEM((L, DIL_GW), BF16), pltpu.VMEM((DIL_HEADS, QB, KB), F32),
                        pltpu.VMEM((DIL_HEADS, QB, KB), BF16), pltpu.VMEM((DIL_HEADS, QB, LANES), F32)],
        compiler_params=_cparams(("parallel", "arbitrary", "arbitrary")),
        name=f"dil_attn_g{g}",
    )(hv, hv, hv, c, sa, sb)
    return o, lse


def _merge_kernel(x_ref, gate_ref, a_ref, b_ref, o0_ref, o1_ref, o2_ref, l0_ref, l1_ref, l2_ref, p_ref,
                  wa_ref, wb_ref, wc_ref, wo_ref, g1_ref, b1_ref, wpg_ref, wp_ref,
                  xb_ref, base_ref, *, alpha, D):
    a = jnp.dot(a_ref[...], wa_ref[...], preferred_element_type=F32)
    b = jnp.dot(b_ref[...], wb_ref[...], preferred_element_type=F32)
    halves = []
    for t in range(DIL_GW // LANES):
        l0, l1, l2 = l0_ref[0, t], l1_ref[0, t], l2_ref[0, t]
        m = jnp.maximum(jnp.maximum(l0, l1), l2)
        e0, e1, e2 = jnp.exp(l0 - m), jnp.exp(l1 - m), jnp.exp(l2 - m)
        halves.append((e0 * o0_ref[0, t] + e1 * o1_ref[0, t] + e2 * o2_ref[0, t]) * (1.0 / (e0 + e1 + e2)))
    cin = jnp.concatenate(halves, axis=-1)
    c = jnp.dot(cin.astype(BF16), wc_ref[...], preferred_element_type=F32)
    merged = (_sigmoid(gate_ref[:, 0:D].astype(F32)) * a
              + _sigmoid(gate_ref[:, D:2 * D].astype(F32)) * b
              + _sigmoid(gate_ref[:, 2 * D:3 * D].astype(F32)) * c)
    mix = jnp.dot(merged.astype(BF16), wo_ref[...], preferred_element_type=F32)
    x1 = _ln(alpha * x_ref[...] + mix, g1_ref[...], b1_ref[...])
    x1b = x1.astype(BF16)
    xb_ref[...] = x1b
    ple = (_sigmoid(jnp.dot(x1b, wpg_ref[...], preferred_element_type=F32))
           * jnp.dot(p_ref[...].astype(BF16), wp_ref[...], preferred_element_type=F32))
    base_ref[...] = alpha * x1 + ple


def _merge(x, h, na, sgu, dil_o, dil_l, p, wa, wb, wc, wo, g1, b1, wpg, wp, alpha, S, tm=512):
    T, D = x.shape
    tm = min(tm, S)
    per_seq = S // tm
    row = lambda w: pl.BlockSpec((tm, w), lambda i: (i, 0))
    full = lambda arr: pl.BlockSpec(arr.shape, lambda i: (0,) * arr.ndim)
    dil = pl.BlockSpec((1, DIL_GW // LANES, tm, LANES), lambda i: (i // per_seq, 0, i % per_seq, 0))
    g1, b1 = g1.reshape(1, D), b1.reshape(1, D)
    return pl.pallas_call(
        functools.partial(_merge_kernel, alpha=alpha, D=D),
        grid=(T // tm,),
        in_specs=[row(D), row(N_BRANCH * D), row(NA_WIDTH), row(SGU_WIDTH),
                  dil, dil, dil, dil, dil, dil,
                  row(p.shape[1]),
                  full(wa), full(wb), full(wc), full(wo), full(g1), full(b1), full(wpg), full(wp)],
        out_specs=[row(D), row(D)],
        out_shape=[jax.ShapeDtypeStruct((T, D), BF16), jax.ShapeDtypeStruct((T, D), F32)],
        compiler_params=_cparams(("parallel",)),
        name="merge",
    )(x, h, na, sgu, dil_o[0], dil_o[1], dil_o[2], dil_l[0], dil_l[1], dil_l[2], p,
      wa, wb, wc, wo, g1, b1, wpg, wp)


_CAND_ROWS = 16 + 7 * 8 + 8


def _cand_index():
    idx = [float(j) for j in range(16)]
    for i in range(1, 8):
        idx += [float(i * 16 + j) for j in range(8)]
    idx += [float(i * 16) for i in range(8, 16)]
    return np.broadcast_to(np.asarray(idx, np.float32)[:, None], (_CAND_ROWS, ROUTE_LW)).copy()


def _extract_top(s, order, n):
    rank = jnp.full(s.shape, 127.0, F32)
    vals = []
    for i in range(n):
        m = jnp.max(s, axis=0, keepdims=True)
        pick = jnp.min(jnp.where(s == m, order, 1e9), axis=0, keepdims=True)
        sel = order == pick
        rank = jnp.where(sel, float(i), rank)
        s = jnp.where(sel, -jnp.inf, s)
        vals.append(m)
    return rank, vals


def _batcher_pairs(n):
    pairs = []

    def merge(lo, hi, r):
        step = r * 2
        if step < hi - lo:
            merge(lo, hi, step)
            merge(lo + r, hi, step)
            pairs.extend((i, i + r) for i in range(lo + r, hi - r, step))
        else:
            pairs.append((lo, lo + r))

    def sort(lo, hi):
        if hi - lo >= 1:
            mid = lo + (hi - lo) // 2
            sort(lo, mid)
            sort(mid + 1, hi)
            merge(lo, hi, 1)

    sort(0, n - 1)
    return pairs


def _sort_desc(xs):
    xs = list(xs)
    for i, j in _batcher_pairs(PEER_TOPK):
        if j < len(xs):
            xs[i], xs[j] = jnp.maximum(xs[i], xs[j]), jnp.minimum(xs[i], xs[j])
    return xs


def _merge_sublanes(xs):
    xs = list(xs)
    n = PEER_TOPK
    for shift in (4, 2, 1):
        t = [jnp.maximum(xs[i], pltpu.roll(xs[n - 1 - i], shift, 0)) for i in range(n)]
        for d in (8, 4, 2, 1):
            for i in range(n):
                if i & d == 0:
                    t[i], t[i + d] = jnp.maximum(t[i], t[i + d]), jnp.minimum(t[i], t[i + d])
        xs = t
    return xs


def _search_bits(x, v):
    c3 = x >= v[7]
    c2 = x >= jnp.where(c3, v[3], v[11])
    c1 = x >= jnp.where(c3, jnp.where(c2, v[1], v[5]), jnp.where(c2, v[9], v[13]))
    c0 = x >= jnp.where(c3,
                        jnp.where(c2, jnp.where(c1, v[0], v[2]), jnp.where(c1, v[4], v[6])),
                        jnp.where(c2, jnp.where(c1, v[8], v[10]), jnp.where(c1, v[12], v[14])))
    return c3, c2, c1, c0


def _pick_by_rank(bits, vals):
    c3, c2, c1, c0 = bits
    l0 = [jnp.where(c0, vals[2 * k], vals[2 * k + 1]) for k in range(8)]
    l1 = [jnp.where(c1, l0[2 * k], l0[2 * k + 1]) for k in range(4)]
    l2 = [jnp.where(c2, l1[2 * k], l1[2 * k + 1]) for k in range(2)]
    return jnp.where(c3, l2[0], l2[1])


def _route_sorted(s1, s2):
    W = s1.shape[1]
    sub_rows = 8
    g1 = [s1[sub_rows * g:sub_rows * (g + 1)] for g in range(PEER_NKEYS // sub_rows)]
    g2 = [s2[sub_rows * g:sub_rows * (g + 1)] for g in range(PEER_NKEYS // sub_rows)]
    v1 = _merge_sublanes(_sort_desc(g1))
    v2 = _merge_sublanes(_sort_desc(g2))
    sub = lax.broadcasted_iota(jnp.int32, (sub_rows, W), 0)

    def pack(vs):
        out = vs[0]
        for s in range(1, sub_rows):
            out = jnp.where(sub == s, vs[s], out)
        return out

    e1r = [jnp.exp(v - v1[0]) for v in v1]
    e2r = [jnp.exp(v - v2[0]) for v in v2]
    v2lo, v2hi, v1hi = pack(v2[:8]), pack(v2[8:]), pack(v1[8:])
    e2lo, e2hi, e1hi = pack(e2r[:8]), pack(e2r[8:]), pack(e1r[8:])
    cand = [v1[0] + v2lo, v1[0] + v2hi] + [v1[i] + v2lo for i in range(1, 8)] + [v1hi + v2[0]]
    ecand = [e1r[0] * e2lo, e1r[0] * e2hi] + [e1r[i] * e2lo for i in range(1, 8)] + [e1hi * e2r[0]]
    neg = jnp.full((sub_rows, W), -jnp.inf, F32)
    tau = _merge_sublanes(_sort_desc(cand) + [neg] * (PEER_TOPK - len(cand)))[PEER_TOPK - 1]
    sel = [jnp.where(c >= tau, 1.0, 0.0) for c in cand]
    colsum = lambda x: jnp.sum(x, axis=0, keepdims=True)
    z = colsum(sum(s * e for s, e in zip(sel, ecand)))
    cnt = [colsum(sel[0] + sel[1])] + [colsum(sel[1 + i]) for i in range(1, 8)] + [sel[9][i:i + 1] for i in range(8)]
    half_minus_cnt = [0.5 - c for c in cnt]
    inv_z = 1.0 / z
    a1, nx1, e2, x2 = [], [], [], []
    n1 = jnp.zeros((sub_rows, W), F32)
    n2 = jnp.zeros((sub_rows, W), F32)
    for x in g1:
        inside = x >= v1[PEER_TOPK - 1]
        nx1.append(jnp.where(inside, _pick_by_rank(_search_bits(x, v1), half_minus_cnt), 1e9))
        a1.append(jnp.where(inside, jnp.exp(x - v1[0]), 0.0) * inv_z)
        n1 = n1 + jnp.where(inside, 1.0, 0.0)
    for x in g2:
        inside = x >= v2[PEER_TOPK - 1]
        c3, c2, c1, c0 = _search_bits(x, v2)
        rank = (jnp.where(c3, 0.0, 8.0) + jnp.where(c2, 0.0, 4.0)) + (jnp.where(c1, 0.0, 2.0) + jnp.where(c0, 0.0, 1.0))
        x2.append(jnp.where(inside, -rank, -127.0))
        e2.append(jnp.exp(x - v2[0]))
        n2 = n2 + jnp.where(inside, 1.0, 0.0)
    dup = jnp.zeros((sub_rows, W), F32)
    for i in range(PEER_TOPK - 1):
        dup = dup + jnp.where(v1[i] == v1[i + 1], 1.0, 0.0) + jnp.where(v2[i] == v2[i + 1], 1.0, 0.0)
    flag = (jnp.abs(colsum(n1) - PEER_TOPK) + jnp.abs(colsum(n2) - PEER_TOPK)
            + jnp.abs(colsum(sum(sel)) - PEER_TOPK) + colsum(dup))
    cat = lambda xs: jnp.concatenate(xs, axis=0)
    return cat(a1), cat(nx1), cat(e2), cat(x2), flag


def _row_tile(ref, h, a, ls):
    row = ref[h, a:a + 1, ls]
    return jnp.broadcast_to(row, (BF16_ROWS, row.shape[1])).astype(BF16)


def _route_chunk(s1, s2, cidx):
    key_order = lax.broadcasted_iota(jnp.int32, s1.shape, 0).astype(F32)
    rank1, v1 = _extract_top(s1, key_order, PEER_TOPK)
    rank2, v2 = _extract_top(s2, key_order, PEER_TOPK)
    v1a = jnp.concatenate(v1, axis=0)
    v2a = jnp.concatenate(v2, axis=0)
    e1r = jnp.exp(v1a - v1[0])
    e2r = jnp.exp(v2a - v2[0])

    def pairs(r1, r2, op):
        blocks = [op(r1[0:1], r2)]
        blocks += [op(r1[i:i + 1], r2[0:8]) for i in range(1, 8)]
        blocks += [op(r1[8:16], r2[0:1])]
        return jnp.concatenate(blocks, axis=0)

    cand = pairs(v1a, v2a, jnp.add)
    ecand = pairs(e1r, e2r, jnp.multiply)
    crank, _ = _extract_top(cand, cidx, PEER_TOPK)
    sel = jnp.where(crank < 100.0, 1.0, 0.0)
    z = jnp.sum(sel * ecand, axis=0, keepdims=True)
    cnt = [jnp.sum(sel[0:16], axis=0, keepdims=True)]
    cnt += [jnp.sum(sel[16 + 8 * (i - 1):16 + 8 * i], axis=0, keepdims=True) for i in range(1, 8)]
    cnt += [sel[72 + i:73 + i] for i in range(8)]
    nx1 = jnp.full(s1.shape, 1e9, F32)
    for i in range(PEER_TOPK):
        nx1 = jnp.where(rank1 == float(i), 0.5 - cnt[i], nx1)
    a1 = jnp.where(rank1 < 100.0, jnp.exp(s1 - v1[0]), 0.0) * (1.0 / z)
    e2 = jnp.exp(s2 - v2[0])
    return a1, nx1, e2, -rank2


def _route_kernel(x_ref, wq_ref, k1_ref, k2_ref, cidx_ref, a1_ref, nx1_ref, e2_ref, x2_ref, q_ref, s1_ref, s2_ref,
                  *, tb):
    hk = PEER_DKEY // 2
    h = pl.program_id(1)

    @pl.when(h == 0)
    def _():
        for hh in range(PEER_HEADS):
            q_ref[hh] = jnp.dot(x_ref[...], wq_ref[:, hh * PEER_DKEY:(hh + 1) * PEER_DKEY],
                                preferred_element_type=F32).astype(BF16)

    s1_ref[...] = _dot_nt(k1_ref[0], q_ref[h, :, :hk])
    s2_ref[...] = _dot_nt(k2_ref[0], q_ref[h, :, hk:])

    def chunk(c, carry):
        ls = pl.ds(pl.multiple_of(c * ROUTE_LW, ROUTE_LW), ROUTE_LW)

        def emit(a1, nx1, e2, x2):
            a1_ref[0, :, ls] = a1
            nx1_ref[0, :, ls] = nx1
            e2_ref[0, :, ls] = e2.astype(e2_ref.dtype)
            x2_ref[0, :, ls] = x2.astype(x2_ref.dtype)

        *fast, flag = _route_sorted(s1_ref[:, ls], s2_ref[:, ls])
        emit(*fast)

        @pl.when(jnp.max(flag) > 0.0)
        def _():
            emit(*_route_chunk(s1_ref[:, ls], s2_ref[:, ls], cidx_ref[...]))

        return carry

    lax.fori_loop(0, tb // ROUTE_LW, chunk, 0)


def _peer_route(xb, wq, k1, k2, tb):
    T, D = xb.shape
    H = PEER_HEADS
    cidx = jnp.asarray(_cand_index())
    out_spec = pl.BlockSpec((1, PEER_NKEYS, tb), lambda i, h: (h, 0, i))
    out_sds = lambda dt: jax.ShapeDtypeStruct((H, PEER_NKEYS, T), dt)
    return pl.pallas_call(
        functools.partial(_route_kernel, tb=tb),
        grid=(T // tb, H),
        in_specs=[pl.BlockSpec((tb, D), lambda i, h: (i, 0)),
                  pl.BlockSpec((D, H * PEER_DKEY), lambda i, h: (0, 0)),
                  pl.BlockSpec((1, PEER_NKEYS, PEER_DKEY // 2), lambda i, h: (h, 0, 0)),
                  pl.BlockSpec((1, PEER_NKEYS, PEER_DKEY // 2), lambda i, h: (h, 0, 0)),
                  pl.BlockSpec((_CAND_ROWS, ROUTE_LW), lambda i, h: (0, 0))],
        out_specs=[out_spec] * 4,
        out_shape=[out_sds(F32), out_sds(F32), out_sds(BF16), out_sds(BF16)],
        scratch_shapes=[pltpu.VMEM((H, tb, PEER_DKEY), BF16),
                        pltpu.VMEM((PEER_NKEYS, tb), F32), pltpu.VMEM((PEER_NKEYS, tb), F32)],
        compiler_params=_cparams(("parallel", "arbitrary")),
        name="peer_route",
    )(xb, wq, k1, k2, cidx)


def _dense_kernel(x_ref, u_ref, vt_ref, a1_ref, nx1_ref, e2_ref, x2_ref, base_ref, g_ref, b_ref,
                  yf_ref, yb_ref, acc_ref, act_ref, pw_ref, *, eb):
    e = pl.program_id(1)
    last = pl.num_programs(1) - 1
    tb = x_ref.shape[0]
    zero = jnp.zeros((), BF16)

    for sb in range(eb // DENSE_SUB):
        rows_e = slice(sb * DENSE_SUB, (sb + 1) * DENSE_SUB)
        act_ref[rows_e, :] = _gelu(_dot_nt(u_ref[rows_e, :], x_ref[...]).astype(BF16))
    nk = PEER_NKEYS // BF16_ROWS

    for al in range(eb // PEER_NKEYS):
        for lt in range(tb // DENSE_LW):
            ls = slice(lt * DENSE_LW, (lt + 1) * DENSE_LW)
            w = [None] * nk
            for h in range(PEER_HEADS):
                nxb = _row_tile(nx1_ref, h, al, ls)
                a1b = _row_tile(a1_ref, h, al, ls)
                for k in range(nk):
                    rows = slice(k * BF16_ROWS, (k + 1) * BF16_ROWS)
                    term = jnp.where(x2_ref[h, rows, ls] >= nxb, e2_ref[h, rows, ls], zero) * a1b
                    w[k] = term if w[k] is None else w[k] + term
            for k in range(nk):
                r0 = al * PEER_NKEYS + k * BF16_ROWS
                pw_ref[r0:r0 + BF16_ROWS, ls] = w[k] * act_ref[r0:r0 + BF16_ROWS, ls]

    d = jnp.dot(vt_ref[...], pw_ref[...], preferred_element_type=F32)

    @pl.when(e == 0)
    def _():
        acc_ref[...] = d

    @pl.when(e > 0)
    def _():
        acc_ref[...] += d

    @pl.when(e == last)
    def _():
        y = _ln(base_ref[...] + acc_ref[...].T, g_ref[...], b_ref[...])
        yf_ref[...] = y
        yb_ref[...] = y.astype(BF16)


def _peer_dense(xb, u, vt, route, base, g2, b2, tb, eb=2048):
    T, D = xb.shape
    NE = u.shape[0]
    H = PEER_HEADS
    rspec = pl.BlockSpec((H, PEER_NKEYS, tb), lambda i, e: (0, 0, i))
    aspec = pl.BlockSpec((H, eb // PEER_NKEYS, tb), lambda i, e: (0, e, i))
    return pl.pallas_call(
        functools.partial(_dense_kernel, eb=eb),
        grid=(T // tb, NE // eb),
        in_specs=[pl.BlockSpec((tb, D), lambda i, e: (i, 0)),
                  pl.BlockSpec((eb, D), lambda i, e: (e, 0)),
                  pl.BlockSpec((D, eb), lambda i, e: (0, e)),
                  aspec, aspec, rspec, rspec,
                  pl.BlockSpec((tb, D), lambda i, e: (i, 0)),
                  pl.BlockSpec((1, D), lambda i, e: (0, 0)),
                  pl.BlockSpec((1, D), lambda i, e: (0, 0))],
        out_specs=[pl.BlockSpec((tb, D), lambda i, e: (i, 0)),
                   pl.BlockSpec((tb, D), lambda i, e: (i, 0))],
        out_shape=[jax.ShapeDtypeStruct((T, D), F32), jax.ShapeDtypeStruct((T, D), BF16)],
        scratch_shapes=[pltpu.VMEM((D, tb), F32), pltpu.VMEM((eb, tb), BF16), pltpu.VMEM((eb, tb), BF16)],
        compiler_params=_cparams(("parallel", "arbitrary")),
        name="peer_dense",
    )(xb, u, vt, route[0], route[1], route[2], route[3], base, g2.reshape(1, D), b2.reshape(1, D))


def _layer_weights(i, w_in, rpb, sgu_ln_g, sgu_ln_b, sgu_w, sgu_b, w_br_a, w_br_b, w_br_c, w_out, ln1_g, ln1_b,
                   peer_wq, peer_k1, peer_k2, peer_u, peer_v, ple_w, ple_gate_w, ln2_g, ln2_b):
    D = w_in.shape[1]
    off_c = 3 * NA_WIDTH + 2 * SGU_WIDTH + 3 * DIL_WIDTH
    off_b = 3 * NA_WIDTH + 2 * SGU_WIDTH

    def group_w(g):
        return jnp.concatenate(
            [w_in[i][:, off_b + part * DIL_WIDTH + g * DIL_GW: off_b + part * DIL_WIDTH + (g + 1) * DIL_GW]
             for part in range(3)], axis=1).astype(BF16)

    w_perm = jnp.concatenate([w_in[i][:, off_c:].astype(BF16), w_in[i][:, :off_b].astype(BF16), group_w(0)], axis=1)
    return dict(
        w_in=w_perm, w_dil={g: group_w(g) for g in range(1, len(DIL_PATTERNS))},
        col_a=N_BRANCH * D, col_b=N_BRANCH * D + 3 * NA_WIDTH,
        col_c=N_BRANCH * D + 3 * NA_WIDTH + 2 * SGU_WIDTH,
        na_bias=_na_bias_table(rpb[i]),
        sgu_ln_g=sgu_ln_g[i], sgu_ln_b=sgu_ln_b[i], sgu_w=sgu_w[i], sgu_b=sgu_b[i],
        wa=w_br_a[i].astype(BF16), wb=w_br_b[i].astype(BF16), wc=w_br_c[i].astype(BF16),
        wo=w_out[i].astype(BF16), ln1_g=ln1_g[i], ln1_b=ln1_b[i],
        wq=peer_wq[i].astype(BF16), k1=peer_k1[i].astype(BF16), k2=peer_k2[i].astype(BF16),
        u=peer_u[i].astype(BF16), vt=peer_v[i].astype(BF16).T,
        wp=ple_w[i].astype(BF16), wpg=ple_gate_w[i].astype(BF16), ln2_g=ln2_g[i], ln2_b=ln2_b[i])


def _trunk(x, p, ln0_g, ln0_b, layers, alpha):
    B, S, D = x.shape
    T = B * S
    tb = min(512, T)
    tabs = _rope_tables(S)
    xf, xb = _ln0(x.reshape(T, D), ln0_g, ln0_b)
    for i, lw in enumerate(layers):
        h = _inproj(xb, lw["w_in"])
        h3 = h.reshape(B, S, -1)
        na = _na_attention(h3, lw["na_bias"], lw["col_a"]).reshape(T, NA_WIDTH)
        sg = _sgu(h3, lw["sgu_ln_g"], lw["sgu_ln_b"], lw["sgu_w"], lw["sgu_b"], lw["col_b"]).reshape(T, SGU_WIDTH)
        dil = [_dil_attention(h3, tabs, 0, B, S, lw["col_c"])]
        for g in range(1, len(DIL_PATTERNS)):
            hd = _inproj_residue_major(xb, lw["w_dil"][g], DIL_PATTERNS[g][1])
            dil.append(_dil_attention(hd, tabs, g, B, S))
        x1b, base = _merge(xf, h, na, sg, [d[0] for d in dil], [d[1] for d in dil], p[i].reshape(T, -1),
                           lw["wa"], lw["wb"], lw["wc"], lw["wo"], lw["ln1_g"], lw["ln1_b"],
                           lw["wpg"], lw["wp"], alpha, S)
        route = _peer_route(x1b, lw["wq"], lw["k1"], lw["k2"], tb)
        xf, xb = _peer_dense(x1b, lw["u"], lw["vt"], route, base, lw["ln2_g"], lw["ln2_b"], tb)
    return xf.reshape(B, S, D)


def kernel(x_prompt, x_sample, p_prompt, p_sample, ln0_g, ln0_b, w_in, rpb, sgu_ln_g, sgu_ln_b, sgu_w, sgu_b, w_br_a, w_br_b, w_br_c, w_out, ln1_g, ln1_b, peer_wq, peer_k1, peer_k2, peer_u, peer_v, ple_w, ple_gate_w, ln2_g, ln2_b):
    depth = w_in.shape[0]
    alpha = (2 * depth) ** 0.25
    layers = [_layer_weights(i, w_in, rpb, sgu_ln_g, sgu_ln_b, sgu_w, sgu_b, w_br_a, w_br_b, w_br_c, w_out,
                             ln1_g, ln1_b, peer_wq, peer_k1, peer_k2, peer_u, peer_v, ple_w, ple_gate_w,
                             ln2_g, ln2_b) for i in range(depth)]
    y_prompt = _trunk(x_prompt, p_prompt, ln0_g, ln0_b, layers, alpha)
    y_sample = _trunk(x_sample, p_sample, ln0_g, ln0_b, layers, alpha)
    return (y_prompt, y_sample)
```

```python
import functools

import numpy as np
import jax
import jax.numpy as jnp
from jax import lax
from jax.experimental import pallas as pl
from jax.experimental.pallas import tpu as pltpu

F32 = jnp.float32
BF16 = jnp.bfloat16

GRID_W = 64
HEAD_DIM = 64
NA_HEADS = 8
NA_WIDTH = NA_HEADS * HEAD_DIM
NA_ROWS = 8
NA_COLS = 16
SGU_GROUPS = 6
SGU_GROUP_CH = 128
SGU_WIDTH = SGU_GROUPS * SGU_GROUP_CH
SGU_CHUNK = 128
DIL_PATTERNS = ((128, 1), (512, 4), (2048, 16))
DIL_HEADS = 4
DIL_GW = DIL_HEADS * HEAD_DIM
DIL_WIDTH = len(DIL_PATTERNS) * DIL_GW
ROPE_THETA = 10000.0
N_BRANCH = 3
PEER_HEADS = 8
PEER_NKEYS = 128
PEER_DKEY = 256
PEER_TOPK = 16
LN_EPS = 1e-5
NEG_BIG = -1e30

LANES = 128
BF16_ROWS = 16
DENSE_SUB = 512
DENSE_LW = 256
DIL_QUERY_BLOCK = 256
NA_ROWS_PER_STEP = 4
ROUTE_LW = 256
VMEM_LIMIT = 48 * 1024 * 1024


def _cparams(sem):
    return pltpu.CompilerParams(dimension_semantics=sem, vmem_limit_bytes=VMEM_LIMIT)


def _ln(x, g, b):
    mu = jnp.mean(x, axis=-1, keepdims=True)
    xc = x - mu
    var = jnp.mean(xc * xc, axis=-1, keepdims=True)
    return xc * lax.rsqrt(var + LN_EPS) * g + b


def _gelu(x):
    return x * (0.5 * (1.0 + jnp.tanh(0.7978845608028654 * (x + 0.044715 * (x * x * x)))))


def _sigmoid(x):
    return 1.0 / (1.0 + jnp.exp(-x))


def _dot_nt(a, b):
    return lax.dot_general(a, b, (((1,), (1,)), ((), ())), preferred_element_type=F32)


def _ln0_kernel(x_ref, g_ref, b_ref, xf_ref, xb_ref):
    y = _ln(x_ref[...], g_ref[...], b_ref[...])
    xf_ref[...] = y
    xb_ref[...] = y.astype(BF16)


def _ln0(x, g, b, tm=512):
    T, D = x.shape
    return pl.pallas_call(
        _ln0_kernel,
        grid=(T // tm,),
        in_specs=[pl.BlockSpec((tm, D), lambda i: (i, 0)),
                  pl.BlockSpec((1, D), lambda i: (0, 0)),
                  pl.BlockSpec((1, D), lambda i: (0, 0))],
        out_specs=[pl.BlockSpec((tm, D), lambda i: (i, 0)),
                   pl.BlockSpec((tm, D), lambda i: (i, 0))],
        out_shape=[jax.ShapeDtypeStruct((T, D), F32), jax.ShapeDtypeStruct((T, D), BF16)],
        compiler_params=_cparams(("parallel",)),
        name="ln0",
    )(x, g.reshape(1, D), b.reshape(1, D))


def _matmul_kernel(x_ref, w_ref, o_ref):
    o_ref[...] = jnp.dot(x_ref[...], w_ref[...], preferred_element_type=F32).astype(o_ref.dtype)


def _inproj(xb, w, tm=1024, tn=2304):
    T, K = xb.shape
    N = w.shape[1]
    tm = min(tm, T)
    return pl.pallas_call(
        _matmul_kernel,
        grid=(T // tm, N // tn),
        in_specs=[pl.BlockSpec((tm, K), lambda i, j: (i, 0)),
                  pl.BlockSpec((K, tn), lambda i, j: (0, j))],
        out_specs=pl.BlockSpec((tm, tn), lambda i, j: (i, j)),
        out_shape=jax.ShapeDtypeStruct((T, N), BF16),
        compiler_params=_cparams(("parallel", "parallel")),
        name="inproj",
    )(xb, w)


def _matmul_residue_major_kernel(x_ref, w_ref, o_ref, acc_ref, *, dil):
    acc = jnp.dot(x_ref[...], w_ref[...], preferred_element_type=F32)
    ntile, tm, _ = acc_ref.shape
    for j in range(ntile):
        acc_ref[j] = acc[:, j * LANES:(j + 1) * LANES]
    for r in range(dil):
        for j in range(ntile):
            c0 = (r * ntile + j) * LANES
            o_ref[:, c0:c0 + LANES] = acc_ref[j, pl.ds(r, tm // dil, stride=dil), :].astype(o_ref.dtype)


def _inproj_residue_major(xb, w, dil, tm=1024):
    T, K = xb.shape
    N = w.shape[1]
    tm = min(tm, T)
    return pl.pallas_call(
        functools.partial(_matmul_residue_major_kernel, dil=dil),
        grid=(T // tm,),
        in_specs=[pl.BlockSpec((tm, K), lambda i: (i, 0)),
                  pl.BlockSpec((K, N), lambda i: (0, 0))],
        out_specs=pl.BlockSpec((tm // dil, dil * N), lambda i: (i, 0)),
        out_shape=jax.ShapeDtypeStruct((T // dil, dil * N), BF16),
        scratch_shapes=[pltpu.VMEM((N // LANES, tm, LANES), F32)],
        compiler_params=_cparams(("parallel",)),
        name=f"inproj_d{dil}",
    )(xb, w)


def _na_kernel(q_ref, k_ref, v_ref, bias_ref, o_ref, s_ref, p_ref, *, rows, rb):
    nk = NA_ROWS * GRID_W
    npair = NA_WIDTH // LANES
    first = lax.broadcasted_iota(jnp.int32, (GRID_W, LANES), 1) < HEAD_DIM
    scale = HEAD_DIM ** -0.5
    units = []
    for rr in range(rb):
        r = pl.program_id(1) * rb + rr
        rs = jnp.clip(r - NA_ROWS // 2, 0, rows - NA_ROWS)
        for p in range(npair):
            units.append((rr * npair + p, slice(rr * GRID_W, (rr + 1) * GRID_W), slice(p * LANES, (p + 1) * LANES),
                          r - rs, pl.multiple_of(rs * GRID_W, GRID_W), p))
    for u, qrows, cols, case, start, p in units:
        qp = q_ref[0, qrows, cols].astype(F32) * scale
        q2 = jnp.concatenate([jnp.where(first, qp, 0.0), jnp.where(first, 0.0, qp)], axis=0).astype(BF16)
        s_ref[u] = (_dot_nt(q2, k_ref[0, pl.ds(start, nk), cols])
                    + bias_ref[case, p * 2 * GRID_W:(p + 1) * 2 * GRID_W, :])
    for u, *_ in units:
        s = s_ref[u]
        e = jnp.exp(s - jnp.max(s, axis=-1, keepdims=True))
        p_ref[u] = (e * (1.0 / jnp.sum(e, axis=-1, keepdims=True))).astype(BF16)
    for u, qrows, cols, case, start, p in units:
        o2 = jnp.dot(p_ref[u], v_ref[0, pl.ds(start, nk), cols], preferred_element_type=F32)
        o_ref[0, qrows, cols] = jnp.where(first, o2[:GRID_W], o2[GRID_W:]).astype(o_ref.dtype)


def _na_bias_table(rpb):
    qcol = np.arange(GRID_W)
    kcol = np.arange(GRID_W)
    cstart = np.clip(qcol - NA_COLS // 2, 0, GRID_W - NA_COLS)
    ok = (kcol[None, :] >= cstart[:, None]) & (kcol[None, :] < cstart[:, None] + NA_COLS)
    dc = np.clip(kcol[None, :] - qcol[:, None], -(NA_COLS - 1), NA_COLS - 1) + (NA_COLS - 1)
    b = jnp.where(ok[None, None], rpb[:, :, dc], NEG_BIG)
    cases = []
    for c in range(NA_ROWS):
        dr = np.arange(NA_ROWS) - c + (NA_ROWS - 1)
        t = b[:, dr]
        cases.append(t.transpose(0, 2, 1, 3).reshape(NA_HEADS * GRID_W, NA_ROWS * GRID_W))
    return jnp.stack(cases, axis=0).astype(F32)


def _na_attention(h3, bias_tab, col0):
    B, S, _ = h3.shape
    rows = S // GRID_W
    cb = col0 // NA_WIDTH
    rb = NA_ROWS_PER_STEP
    return pl.pallas_call(
        functools.partial(_na_kernel, rows=rows, rb=rb),
        grid=(B, rows // rb),
        in_specs=[pl.BlockSpec((1, rb * GRID_W, NA_WIDTH), lambda b, r: (b, r, cb)),
                  pl.BlockSpec((1, S, NA_WIDTH), lambda b, r: (b, 0, cb + 1)),
                  pl.BlockSpec((1, S, NA_WIDTH), lambda b, r: (b, 0, cb + 2)),
                  pl.BlockSpec(bias_tab.shape, lambda b, r: (0, 0, 0))],
        out_specs=pl.BlockSpec((1, rb * GRID_W, NA_WIDTH), lambda b, r: (b, r, 0)),
        out_shape=jax.ShapeDtypeStruct((B, S, NA_WIDTH), BF16),
        scratch_shapes=[pltpu.VMEM((rb * NA_WIDTH // LANES, 2 * GRID_W, NA_ROWS * GRID_W), F32),
                        pltpu.VMEM((rb * NA_WIDTH // LANES, 2 * GRID_W, NA_ROWS * GRID_W), BF16)],
        compiler_params=_cparams(("parallel", "arbitrary")),
        name="na_attn",
    )(h3, h3, h3, bias_tab)


def _sgu_kernel(u_ref, v_ref, g_ref, b_ref, ws_ref, bs_ref, o_ref, *, nchunk):
    for c in range(nchunk):
        rs = slice(c * SGU_CHUNK, (c + 1) * SGU_CHUNK)
        v = _ln(_gelu(v_ref[0, rs, :].astype(F32)), g_ref[...], b_ref[...]).astype(BF16)
        u = _gelu(u_ref[0, rs, :].astype(F32))
        for g in range(SGU_GROUPS):
            cs = slice(g * SGU_GROUP_CH, (g + 1) * SGU_GROUP_CH)
            s = jnp.dot(ws_ref[g], v[:, cs], preferred_element_type=F32) + bs_ref[g]
            o_ref[0, rs, cs] = (u[:, cs] * s).astype(o_ref.dtype)


def _sgu(h3, ln_g, ln_b, ws, bs, col0, tc=512):
    B, S, _ = h3.shape
    tc = min(tc, S)
    cb = col0 // SGU_WIDTH
    bs_b = jnp.broadcast_to(bs[:, :, None], (SGU_GROUPS, SGU_CHUNK, SGU_GROUP_CH)).astype(F32)
    return pl.pallas_call(
        functools.partial(_sgu_kernel, nchunk=tc // SGU_CHUNK),
        grid=(B, S // tc),
        in_specs=[pl.BlockSpec((1, tc, SGU_WIDTH), lambda b, i: (b, i, cb)),
                  pl.BlockSpec((1, tc, SGU_WIDTH), lambda b, i: (b, i, cb + 1)),
                  pl.BlockSpec((1, SGU_WIDTH), lambda b, i: (0, 0)),
                  pl.BlockSpec((1, SGU_WIDTH), lambda b, i: (0, 0)),
                  pl.BlockSpec((SGU_GROUPS, SGU_CHUNK, SGU_CHUNK), lambda b, i: (0, 0, 0)),
                  pl.BlockSpec((SGU_GROUPS, SGU_CHUNK, SGU_GROUP_CH), lambda b, i: (0, 0, 0))],
        out_specs=pl.BlockSpec((1, tc, SGU_WIDTH), lambda b, i: (b, i, 0)),
        out_shape=jax.ShapeDtypeStruct((B, S, SGU_WIDTH), BF16),
        compiler_params=_cparams(("parallel", "parallel")),
        name="sgu",
    )(h3, h3, ln_g.reshape(1, -1), ln_b.reshape(1, -1), ws.astype(BF16), bs_b)


def _rope_tile(x, c, sa, sb):
    return x * c + pltpu.roll(x, LANES - HEAD_DIM // 2, 1) * sa + pltpu.roll(x, HEAD_DIM // 2, 1) * sb


def _dil_kernel(q_ref, k_ref, v_ref, c_ref, sa_ref, sb_ref, o_ref, lse_ref, kr_ref, s_ref, p_ref, l_ref,
                *, L, QB, KB, half, dil):
    qi = pl.program_id(2)
    ntile = DIL_GW // LANES
    rchunk = min(L, 256)

    @pl.when(qi == 0)
    def _():
        for c0 in range(0, L, rchunk):
            rs = slice(c0, c0 + rchunk)
            for t in range(ntile):
                cs = slice(t * LANES, (t + 1) * LANES)
                kr_ref[rs, cs] = _rope_tile(k_ref[0, rs, cs].astype(F32), c_ref[rs, cs], sa_ref[rs, cs],
                                            sb_ref[rs, cs]).astype(BF16)

    n0 = pl.multiple_of(qi * QB, QB)
    start = pl.multiple_of(jnp.clip(n0 - half, 0, L - KB), 16)
    qn = n0 + lax.broadcasted_iota(jnp.int32, (QB, KB), 0)
    kn = start + lax.broadcasted_iota(jnp.int32, (QB, KB), 1)
    ok = jnp.abs(kn - qn) <= half
    first = lax.broadcasted_iota(jnp.int32, (QB, LANES), 1) < HEAD_DIM
    scale = HEAD_DIM ** -0.5
    if dil == 1:
        orows = slice(None)
    else:
        orows = pl.ds(n0 * dil + pl.program_id(1), QB, stride=dil)
    for t in range(ntile):
        cs = slice(t * LANES, (t + 1) * LANES)
        qr = _rope_tile(q_ref[0, :, cs].astype(F32), c_ref[pl.ds(n0, QB), cs], sa_ref[pl.ds(n0, QB), cs],
                        sb_ref[pl.ds(n0, QB), cs]) * scale
        kt = kr_ref[pl.ds(start, KB), cs]
        for sub in range(2):
            qm = jnp.where(first, qr, 0.0) if sub == 0 else jnp.where(first, 0.0, qr)
            s_ref[2 * t + sub] = jnp.where(ok, _dot_nt(qm.astype(BF16), kt), NEG_BIG)
    for u in range(2 * ntile):
        s = s_ref[u]
        m = jnp.max(s, axis=-1, keepdims=True)
        e = jnp.exp(s - m)
        den = jnp.sum(e, axis=-1, keepdims=True)
        p_ref[u] = (e * (1.0 / den)).astype(BF16)
        l_ref[u] = jnp.broadcast_to(m + jnp.log(den), (QB, LANES))
    for t in range(ntile):
        vt = v_ref[0, pl.ds(start, KB), t * LANES:(t + 1) * LANES]
        outs = [jnp.dot(p_ref[2 * t + sub], vt, preferred_element_type=F32) for sub in range(2)]
        o_ref[0, t, orows, :] = jnp.where(first, outs[0], outs[1])
        lse_ref[0, t, orows, :] = jnp.where(first, l_ref[2 * t], l_ref[2 * t + 1])


def _rope_tables(S):
    half = HEAD_DIM // 2
    inv = ROPE_THETA ** (-jnp.arange(half, dtype=F32) / half)
    ang = jnp.arange(S).astype(F32)[:, None] * inv[None, :]
    cos, sin = jnp.cos(ang), jnp.sin(ang)
    z = jnp.zeros_like(sin)
    c = jnp.tile(jnp.concatenate([cos, cos], axis=-1), (1, DIL_HEADS))
    sa = jnp.tile(jnp.concatenate([-sin, z], axis=-1), (1, DIL_HEADS))
    sb = jnp.tile(jnp.concatenate([z, sin], axis=-1), (1, DIL_HEADS))
    return c, sa, sb


def _dil_attention(hsrc, tabs, g, B, S, col0=0):
    window, dil = DIL_PATTERNS[g]
    L = S // dil
    half = window // (2 * dil)
    QB = min(L, DIL_QUERY_BLOCK)
    KB = min(L, QB + 2 * half)
    ntile = DIL_GW // LANES
    if dil == 1:
        hv, cpb, qb = hsrc, hsrc.shape[-1] // DIL_GW, col0 // DIL_GW
        out_spec = pl.BlockSpec((1, ntile, QB, LANES), lambda b, r, i: (b, 0, i, 0))
    else:
        hv, cpb, qb = hsrc.reshape(B, L, dil * 3 * DIL_GW), 3, 0
        out_spec = pl.BlockSpec((1, ntile, S, LANES), lambda b, r, i: (b, 0, 0, 0))
    c, sa, sb = [t.reshape(L, dil * DIL_GW) for t in tabs]
    tab_spec = pl.BlockSpec((L, DIL_GW), lambda b, r, i: (0, r))
    o, lse = pl.pallas_call(
        functools.partial(_dil_kernel, L=L, QB=QB, KB=KB, half=half, dil=dil),
        grid=(B, dil, L // QB),
        in_specs=[pl.BlockSpec((1, QB, DIL_GW), lambda b, r, i: (b, i, r * cpb + qb)),
                  pl.BlockSpec((1, L, DIL_GW), lambda b, r, i: (b, 0, r * cpb + qb + 1)),
                  pl.BlockSpec((1, L, DIL_GW), lambda b, r, i: (b, 0, r * cpb + qb + 2)),
                  tab_spec, tab_spec, tab_spec],
        out_specs=[out_spec, out_spec],
        out_shape=[jax.ShapeDtypeStruct((B, ntile, S, LANES), F32)] * 2,
        scratch_shapes=[pltpu.VMEM((L, DIL_GW), BF16), pltpu.VMEM((DIL_HEADS, QB, KB), F32),
                        pltpu.VMEM((DIL_HEADS, QB, KB), BF16), pltpu.VMEM((DIL_HEADS, QB, LANES), F32)],
        compiler_params=_cparams(("parallel", "arbitrary", "arbitrary")),
        name=f"dil_attn_g{g}",
    )(hv, hv, hv, c, sa, sb)
    return o, lse


def _merge_kernel(x_ref, gate_ref, a_ref, b_ref, o0_ref, o1_ref, o2_ref, l0_ref, l1_ref, l2_ref, p_ref,
                  wa_ref, wb_ref, wc_ref, wo_ref, g1_ref, b1_ref, wpg_ref, wp_ref,
                  xb_ref, base_ref, *, alpha, D):
    a = jnp.dot(a_ref[...], wa_ref[...], preferred_element_type=F32)
    b = jnp.dot(b_ref[...], wb_ref[...], preferred_element_type=F32)
    halves = []
    for t in range(DIL_GW // LANES):
        l0, l1, l2 = l0_ref[0, t], l1_ref[0, t], l2_ref[0, t]
        m = jnp.maximum(jnp.maximum(l0, l1), l2)
        e0, e1, e2 = jnp.exp(l0 - m), jnp.exp(l1 - m), jnp.exp(l2 - m)
        halves.append((e0 * o0_ref[0, t] + e1 * o1_ref[0, t] + e2 * o2_ref[0, t]) * (1.0 / (e0 + e1 + e2)))
    cin = jnp.concatenate(halves, axis=-1)
    c = jnp.dot(cin.astype(BF16), wc_ref[...], preferred_element_type=F32)
    merged = (_sigmoid(gate_ref[:, 0:D].astype(F32)) * a
              + _sigmoid(gate_ref[:, D:2 * D].astype(F32)) * b
              + _sigmoid(gate_ref[:, 2 * D:3 * D].astype(F32)) * c)
    mix = jnp.dot(merged.astype(BF16), wo_ref[...], preferred_element_type=F32)
    x1 = _ln(alpha * x_ref[...] + mix, g1_ref[...], b1_ref[...])
    x1b = x1.astype(BF16)
    xb_ref[...] = x1b
    ple = (_sigmoid(jnp.dot(x1b, wpg_ref[...], preferred_element_type=F32))
           * jnp.dot(p_ref[...].astype(BF16), wp_ref[...], preferred_element_type=F32))
    base_ref[...] = alpha * x1 + ple


def _merge(x, h, na, sgu, dil_o, dil_l, p, wa, wb, wc, wo, g1, b1, wpg, wp, alpha, S, tm=512):
    T, D = x.shape
    tm = min(tm, S)
    per_seq = S // tm
    row = lambda w: pl.BlockSpec((tm, w), lambda i: (i, 0))
    full = lambda arr: pl.BlockSpec(arr.shape, lambda i: (0,) * arr.ndim)
    dil = pl.BlockSpec((1, DIL_GW // LANES, tm, LANES), lambda i: (i // per_seq, 0, i % per_seq, 0))
    g1, b1 = g1.reshape(1, D), b1.reshape(1, D)
    return pl.pallas_call(
        functools.partial(_merge_kernel, alpha=alpha, D=D),
        grid=(T // tm,),
        in_specs=[row(D), row(N_BRANCH * D), row(NA_WIDTH), row(SGU_WIDTH),
                  dil, dil, dil, dil, dil, dil,
                  row(p.shape[1]),
                  full(wa), full(wb), full(wc), full(wo), full(g1), full(b1), full(wpg), full(wp)],
        out_specs=[row(D), row(D)],
        out_shape=[jax.ShapeDtypeStruct((T, D), BF16), jax.ShapeDtypeStruct((T, D), F32)],
        compiler_params=_cparams(("parallel",)),
        name="merge",
    )(x, h, na, sgu, dil_o[0], dil_o[1], dil_o[2], dil_l[0], dil_l[1], dil_l[2], p,
      wa, wb, wc, wo, g1, b1, wpg, wp)


_CAND_ROWS = 16 + 7 * 8 + 8


def _cand_index():
    idx = [float(j) for j in range(16)]
    for i in range(1, 8):
        idx += [float(i * 16 + j) for j in range(8)]
    idx += [float(i * 16) for i in range(8, 16)]
    return np.broadcast_to(np.asarray(idx, np.float32)[:, None], (_CAND_ROWS, ROUTE_LW)).copy()


def _extract_top(s, order, n):
    rank = jnp.full(s.shape, 127.0, F32)
    vals = []
    for i in range(n):
        m = jnp.max(s, axis=0, keepdims=True)
        pick = jnp.min(jnp.where(s == m, order, 1e9), axis=0, keepdims=True)
        sel = order == pick
        rank = jnp.where(sel, float(i), rank)
        s = jnp.where(sel, -jnp.inf, s)
        vals.append(m)
    return rank, vals


def _batcher_pairs(n):
    pairs = []

    def merge(lo, hi, r):
        step = r * 2
        if step < hi - lo:
            merge(lo, hi, step)
            merge(lo + r, hi, step)
            pairs.extend((i, i + r) for i in range(lo + r, hi - r, step))
        else:
            pairs.append((lo, lo + r))

    def sort(lo, hi):
        if hi - lo >= 1:
            mid = lo + (hi - lo) // 2
            sort(lo, mid)
            sort(mid + 1, hi)
            merge(lo, hi, 1)

    sort(0, n - 1)
    return pairs


def _sort_desc(xs):
    xs = list(xs)
    for i, j in _batcher_pairs(PEER_TOPK):
        if j < len(xs):
            xs[i], xs[j] = jnp.maximum(xs[i], xs[j]), jnp.minimum(xs[i], xs[j])
    return xs


def _merge_sublanes(xs):
    xs = list(xs)
    n = PEER_TOPK
    for shift in (4, 2, 1):
        t = [jnp.maximum(xs[i], pltpu.roll(xs[n - 1 - i], shift, 0)) for i in range(n)]
        for d in (8, 4, 2, 1):
            for i in range(n):
                if i & d == 0:
                    t[i], t[i + d] = jnp.maximum(t[i], t[i + d]), jnp.minimum(t[i], t[i + d])
        xs = t
    return xs


def _search_bits(x, v):
    c3 = x >= v[7]
    c2 = x >= jnp.where(c3, v[3], v[11])
    c1 = x >= jnp.where(c3, jnp.where(c2, v[1], v[5]), jnp.where(c2, v[9], v[13]))
    c0 = x >= jnp.where(c3,
                        jnp.where(c2, jnp.where(c1, v[0], v[2]), jnp.where(c1, v[4], v[6])),
                        jnp.where(c2, jnp.where(c1, v[8], v[10]), jnp.where(c1, v[12], v[14])))
    return c3, c2, c1, c0


def _pick_by_rank(bits, vals):
    c3, c2, c1, c0 = bits
    l0 = [jnp.where(c0, vals[2 * k], vals[2 * k + 1]) for k in range(8)]
    l1 = [jnp.where(c1, l0[2 * k], l0[2 * k + 1]) for k in range(4)]
    l2 = [jnp.where(c2, l1[2 * k], l1[2 * k + 1]) for k in range(2)]
    return jnp.where(c3, l2[0], l2[1])


def _route_sorted(s1, s2):
    W = s1.shape[1]
    sub_rows = 8
    g1 = [s1[sub_rows * g:sub_rows * (g + 1)] for g in range(PEER_NKEYS // sub_rows)]
    g2 = [s2[sub_rows * g:sub_rows * (g + 1)] for g in range(PEER_NKEYS // sub_rows)]
    v1 = _merge_sublanes(_sort_desc(g1))
    v2 = _merge_sublanes(_sort_desc(g2))
    sub = lax.broadcasted_iota(jnp.int32, (sub_rows, W), 0)

    def pack(vs):
        out = vs[0]
        for s in range(1, sub_rows):
            out = jnp.where(sub == s, vs[s], out)
        return out

    e1r = [jnp.exp(v - v1[0]) for v in v1]
    e2r = [jnp.exp(v - v2[0]) for v in v2]
    v2lo, v2hi, v1hi = pack(v2[:8]), pack(v2[8:]), pack(v1[8:])
    e2lo, e2hi, e1hi = pack(e2r[:8]), pack(e2r[8:]), pack(e1r[8:])
    cand = [v1[0] + v2lo, v1[0] + v2hi] + [v1[i] + v2lo for i in range(1, 8)] + [v1hi + v2[0]]
    ecand = [e1r[0] * e2lo, e1r[0] * e2hi] + [e1r[i] * e2lo for i in range(1, 8)] + [e1hi * e2r[0]]
    neg = jnp.full((sub_rows, W), -jnp.inf, F32)
    tau = _merge_sublanes(_sort_desc(cand) + [neg] * (PEER_TOPK - len(cand)))[PEER_TOPK - 1]
    sel = [jnp.where(c >= tau, 1.0, 0.0) for c in cand]
    colsum = lambda x: jnp.sum(x, axis=0, keepdims=True)
    z = colsum(sum(s * e for s, e in zip(sel, ecand)))
    cnt = [colsum(sel[0] + sel[1])] + [colsum(sel[1 + i]) for i in range(1, 8)] + [sel[9][i:i + 1] for i in range(8)]
    half_minus_cnt = [0.5 - c for c in cnt]
    inv_z = 1.0 / z
    a1, nx1, e2, x2 = [], [], [], []
    n1 = jnp.zeros((sub_rows, W), F32)
    n2 = jnp.zeros((sub_rows, W), F32)
    for x in g1:
        inside = x >= v1[PEER_TOPK - 1]
        nx1.append(jnp.where(inside, _pick_by_rank(_search_bits(x, v1), half_minus_cnt), 1e9))
        a1.append(jnp.where(inside, jnp.exp(x - v1[0]), 0.0) * inv_z)
        n1 = n1 + jnp.where(inside, 1.0, 0.0)
    for x in g2:
        inside = x >= v2[PEER_TOPK - 1]
        c3, c2, c1, c0 = _search_bits(x, v2)
        rank = (jnp.where(c3, 0.0, 8.0) + jnp.where(c2, 0.0, 4.0)) + (jnp.where(c1, 0.0, 2.0) + jnp.where(c0, 0.0, 1.0))
        x2.append(jnp.where(inside, -rank, -127.0))
        e2.append(jnp.exp(x - v2[0]))
        n2 = n2 + jnp.where(inside, 1.0, 0.0)
    dup = jnp.zeros((sub_rows, W), F32)
    for i in range(PEER_TOPK - 1):
        dup = dup + jnp.where(v1[i] == v1[i + 1], 1.0, 0.0) + jnp.where(v2[i] == v2[i + 1], 1.0, 0.0)
    flag = (jnp.abs(colsum(n1) - PEER_TOPK) + jnp.abs(colsum(n2) - PEER_TOPK)
            + jnp.abs(colsum(sum(sel)) - PEER_TOPK) + colsum(dup))
    cat = lambda xs: jnp.concatenate(xs, axis=0)
    return cat(a1), cat(nx1), cat(e2), cat(x2), flag


def _row_tile(ref, h, a, ls):
    row = ref[h, a:a + 1, ls]
    return jnp.broadcast_to(row, (BF16_ROWS, row.shape[1])).astype(BF16)


def _route_chunk(s1, s2, cidx):
    key_order = lax.broadcasted_iota(jnp.int32, s1.shape, 0).astype(F32)
    rank1, v1 = _extract_top(s1, key_order, PEER_TOPK)
    rank2, v2 = _extract_top(s2, key_order, PEER_TOPK)
    v1a = jnp.concatenate(v1, axis=0)
    v2a = jnp.concatenate(v2, axis=0)
    e1r = jnp.exp(v1a - v1[0])
    e2r = jnp.exp(v2a - v2[0])

    def pairs(r1, r2, op):
        blocks = [op(r1[0:1], r2)]
        blocks += [op(r1[i:i + 1], r2[0:8]) for i in range(1, 8)]
        blocks += [op(r1[8:16], r2[0:1])]
        return jnp.concatenate(blocks, axis=0)

    cand = pairs(v1a, v2a, jnp.add)
    ecand = pairs(e1r, e2r, jnp.multiply)
    crank, _ = _extract_top(cand, cidx, PEER_TOPK)
    sel = jnp.where(crank < 100.0, 1.0, 0.0)
    z = jnp.sum(sel * ecand, axis=0, keepdims=True)
    cnt = [jnp.sum(sel[0:16], axis=0, keepdims=True)]
    cnt += [jnp.sum(sel[16 + 8 * (i - 1):16 + 8 * i], axis=0, keepdims=True) for i in range(1, 8)]
    cnt += [sel[72 + i:73 + i] for i in range(8)]
    nx1 = jnp.full(s1.shape, 1e9, F32)
    for i in range(PEER_TOPK):
        nx1 = jnp.where(rank1 == float(i), 0.5 - cnt[i], nx1)
    a1 = jnp.where(rank1 < 100.0, jnp.exp(s1 - v1[0]), 0.0) * (1.0 / z)
    e2 = jnp.exp(s2 - v2[0])
    return a1, nx1, e2, -rank2


def _route_kernel(x_ref, wq_ref, k1_ref, k2_ref, cidx_ref, a1_ref, nx1_ref, e2_ref, x2_ref, q_ref, s1_ref, s2_ref,
                  *, tb):
    hk = PEER_DKEY // 2
    h = pl.program_id(1)

    @pl.when(h == 0)
    def _():
        for hh in range(PEER_HEADS):
            q_ref[hh] = jnp.dot(x_ref[...], wq_ref[:, hh * PEER_DKEY:(hh + 1) * PEER_DKEY],
                                preferred_element_type=F32).astype(BF16)

    s1_ref[...] = _dot_nt(k1_ref[0], q_ref[h, :, :hk])
    s2_ref[...] = _dot_nt(k2_ref[0], q_ref[h, :, hk:])

    def chunk(c, carry):
        ls = pl.ds(pl.multiple_of(c * ROUTE_LW, ROUTE_LW), ROUTE_LW)

        def emit(a1, nx1, e2, x2):
            a1_ref[0, :, ls] = a1
            nx1_ref[0, :, ls] = nx1
            e2_ref[0, :, ls] = e2.astype(e2_ref.dtype)
            x2_ref[0, :, ls] = x2.astype(x2_ref.dtype)

        *fast, flag = _route_sorted(s1_ref[:, ls], s2_ref[:, ls])
        emit(*fast)

        @pl.when(jnp.max(flag) > 0.0)
        def _():
            emit(*_route_chunk(s1_ref[:, ls], s2_ref[:, ls], cidx_ref[...]))

        return carry

    lax.fori_loop(0, tb // ROUTE_LW, chunk, 0)


def _peer_route(xb, wq, k1, k2, tb):
    T, D = xb.shape
    H = PEER_HEADS
    cidx = jnp.asarray(_cand_index())
    out_spec = pl.BlockSpec((1, PEER_NKEYS, tb), lambda i, h: (h, 0, i))
    out_sds = lambda dt: jax.ShapeDtypeStruct((H, PEER_NKEYS, T), dt)
    return pl.pallas_call(
        functools.partial(_route_kernel, tb=tb),
        grid=(T // tb, H),
        in_specs=[pl.BlockSpec((tb, D), lambda i, h: (i, 0)),
                  pl.BlockSpec((D, H * PEER_DKEY), lambda i, h: (0, 0)),
                  pl.BlockSpec((1, PEER_NKEYS, PEER_DKEY // 2), lambda i, h: (h, 0, 0)),
                  pl.BlockSpec((1, PEER_NKEYS, PEER_DKEY // 2), lambda i, h: (h, 0, 0)),
                  pl.BlockSpec((_CAND_ROWS, ROUTE_LW), lambda i, h: (0, 0))],
        out_specs=[out_spec] * 4,
        out_shape=[out_sds(F32), out_sds(F32), out_sds(BF16), out_sds(BF16)],
        scratch_shapes=[pltpu.VMEM((H, tb, PEER_DKEY), BF16),
                        pltpu.VMEM((PEER_NKEYS, tb), F32), pltpu.VMEM((PEER_NKEYS, tb), F32)],
        compiler_params=_cparams(("parallel", "arbitrary")),
        name="peer_route",
    )(xb, wq, k1, k2, cidx)


def _dense_kernel(x_ref, u_ref, vt_ref, a1_ref, nx1_ref, e2_ref, x2_ref, base_ref, g_ref, b_ref,
                  yf_ref, yb_ref, acc_ref, act_ref, pw_ref, *, eb):
    e = pl.program_id(1)
    last = pl.num_programs(1) - 1
    tb = x_ref.shape[0]
    zero = jnp.zeros((), BF16)

    @pl.when(e == 0)
    def _():
        acc_ref[...] = jnp.zeros_like(acc_ref)

    for sb in range(eb // DENSE_SUB):
        rows_e = slice(sb * DENSE_SUB, (sb + 1) * DENSE_SUB)
        act_ref[rows_e, :] = _gelu(_dot_nt(u_ref[rows_e, :], x_ref[...]).astype(BF16))
    nk = PEER_NKEYS // BF16_ROWS

    for lt in range(tb // DENSE_LW):
        ls = slice(lt * DENSE_LW, (lt + 1) * DENSE_LW)
        for al in range(eb // PEER_NKEYS):
            w = [None] * nk
            for h in range(PEER_HEADS):
                nxb = _row_tile(nx1_ref, h, al, ls)
                a1b = _row_tile(a1_ref, h, al, ls)
                for k in range(nk):
                    rows = slice(k * BF16_ROWS, (k + 1) * BF16_ROWS)
                    term = jnp.where(x2_ref[h, rows, ls] >= nxb, e2_ref[h, rows, ls], zero) * a1b
                    w[k] = term if w[k] is None else w[k] + term
            for k in range(nk):
                r0 = al * PEER_NKEYS + k * BF16_ROWS
                pw_ref[r0:r0 + BF16_ROWS, ls] = w[k] * act_ref[r0:r0 + BF16_ROWS, ls]
        acc_ref[:, ls] += jnp.dot(vt_ref[...], pw_ref[:, ls], preferred_element_type=F32)

    @pl.when(e == last)
    def _():
        y = _ln(base_ref[...] + acc_ref[...].T, g_ref[...], b_ref[...])
        yf_ref[...] = y
        yb_ref[...] = y.astype(BF16)


def _peer_dense(xb, u, vt, route, base, g2, b2, tb, eb=2048):
    T, D = xb.shape
    NE = u.shape[0]
    H = PEER_HEADS
    rspec = pl.BlockSpec((H, PEER_NKEYS, tb), lambda i, e: (0, 0, i))
    aspec = pl.BlockSpec((H, eb // PEER_NKEYS, tb), lambda i, e: (0, e, i))
    return pl.pallas_call(
        functools.partial(_dense_kernel, eb=eb),
        grid=(T // tb, NE // eb),
        in_specs=[pl.BlockSpec((tb, D), lambda i, e: (i, 0)),
                  pl.BlockSpec((eb, D), lambda i, e: (e, 0)),
                  pl.BlockSpec((D, eb), lambda i, e: (0, e)),
                  aspec, aspec, rspec, rspec,
                  pl.BlockSpec((tb, D), lambda i, e: (i, 0)),
                  pl.BlockSpec((1, D), lambda i, e: (0, 0)),
                  pl.BlockSpec((1, D), lambda i, e: (0, 0))],
        out_specs=[pl.BlockSpec((tb, D), lambda i, e: (i, 0)),
                   pl.BlockSpec((tb, D), lambda i, e: (i, 0))],
        out_shape=[jax.ShapeDtypeStruct((T, D), F32), jax.ShapeDtypeStruct((T, D), BF16)],
        scratch_shapes=[pltpu.VMEM((D, tb), F32), pltpu.VMEM((eb, tb), BF16), pltpu.VMEM((eb, tb), BF16)],
        compiler_params=_cparams(("parallel", "arbitrary")),
        name="peer_dense",
    )(xb, u, vt, route[0], route[1], route[2], route[3], base, g2.reshape(1, D), b2.reshape(1, D))


def _layer_weights(i, w_in, rpb, sgu_ln_g, sgu_ln_b, sgu_w, sgu_b, w_br_a, w_br_b, w_br_c, w_out, ln1_g, ln1_b,
                   peer_wq, peer_k1, peer_k2, peer_u, peer_v, ple_w, ple_gate_w, ln2_g, ln2_b):
    D = w_in.shape[1]
    off_c = 3 * NA_WIDTH + 2 * SGU_WIDTH + 3 * DIL_WIDTH
    off_b = 3 * NA_WIDTH + 2 * SGU_WIDTH

    def group_w(g):
        return jnp.concatenate(
            [w_in[i][:, off_b + part * DIL_WIDTH + g * DIL_GW: off_b + part * DIL_WIDTH + (g + 1) * DIL_GW]
             for part in range(3)], axis=1).astype(BF16)

    w_perm = jnp.concatenate([w_in[i][:, off_c:].astype(BF16), w_in[i][:, :off_b].astype(BF16), group_w(0)], axis=1)
    return dict(
        w_in=w_perm, w_dil={g: group_w(g) for g in range(1, len(DIL_PATTERNS))},
        col_a=N_BRANCH * D, col_b=N_BRANCH * D + 3 * NA_WIDTH,
        col_c=N_BRANCH * D + 3 * NA_WIDTH + 2 * SGU_WIDTH,
        na_bias=_na_bias_table(rpb[i]),
        sgu_ln_g=sgu_ln_g[i], sgu_ln_b=sgu_ln_b[i], sgu_w=sgu_w[i], sgu_b=sgu_b[i],
        wa=w_br_a[i].astype(BF16), wb=w_br_b[i].astype(BF16), wc=w_br_c[i].astype(BF16),
        wo=w_out[i].astype(BF16), ln1_g=ln1_g[i], ln1_b=ln1_b[i],
        wq=peer_wq[i].astype(BF16), k1=peer_k1[i].astype(BF16), k2=peer_k2[i].astype(BF16),
        u=peer_u[i].astype(BF16), vt=peer_v[i].astype(BF16).T,
        wp=ple_w[i].astype(BF16), wpg=ple_gate_w[i].astype(BF16), ln2_g=ln2_g[i], ln2_b=ln2_b[i])


def _trunk(x, p, ln0_g, ln0_b, layers, alpha):
    B, S, D = x.shape
    T = B * S
    tb = min(512, T)
    tabs = _rope_tables(S)
    xf, xb = _ln0(x.reshape(T, D), ln0_g, ln0_b)
    for i, lw in enumerate(layers):
        h = _inproj(xb, lw["w_in"])
        h3 = h.reshape(B, S, -1)
        na = _na_attention(h3, lw["na_bias"], lw["col_a"]).reshape(T, NA_WIDTH)
        sg = _sgu(h3, lw["sgu_ln_g"], lw["sgu_ln_b"], lw["sgu_w"], lw["sgu_b"], lw["col_b"]).reshape(T, SGU_WIDTH)
        dil = [_dil_attention(h3, tabs, 0, B, S, lw["col_c"])]
        for g in range(1, len(DIL_PATTERNS)):
            hd = _inproj_residue_major(xb, lw["w_dil"][g], DIL_PATTERNS[g][1])
            dil.append(_dil_attention(hd, tabs, g, B, S))
        x1b, base = _merge(xf, h, na, sg, [d[0] for d in dil], [d[1] for d in dil], p[i].reshape(T, -1),
                           lw["wa"], lw["wb"], lw["wc"], lw["wo"], lw["ln1_g"], lw["ln1_b"],
                           lw["wpg"], lw["wp"], alpha, S)
        route = _peer_route(x1b, lw["wq"], lw["k1"], lw["k2"], tb)
        xf, xb = _peer_dense(x1b, lw["u"], lw["vt"], route, base, lw["ln2_g"], lw["ln2_b"], tb)
    return xf.reshape(B, S, D)


def kernel(x_prompt, x_sample, p_prompt, p_sample, ln0_g, ln0_b, w_in, rpb, sgu_ln_g, sgu_ln_b, sgu_w, sgu_b, w_br_a, w_br_b, w_br_c, w_out, ln1_g, ln1_b, peer_wq, peer_k1, peer_k2, peer_u, peer_v, ple_w, ple_gate_w, ln2_g, ln2_b):
    depth = w_in.shape[0]
    alpha = (2 * depth) ** 0.25
    layers = [_layer_weights(i, w_in, rpb, sgu_ln_g, sgu_ln_b, sgu_w, sgu_b, w_br_a, w_br_b, w_br_c, w_out,
                             ln1_g, ln1_b, peer_wq, peer_k1, peer_k2, peer_u, peer_v, ple_w, ple_gate_w,
                             ln2_g, ln2_b) for i in range(depth)]
    y_prompt = _trunk(x_prompt, p_prompt, ln0_g, ln0_b, layers, alpha)
    y_sample = _trunk(x_sample, p_sample, ln0_g, ln0_b, layers, alpha)
    return (y_prompt, y_sample)
```

```python
import functools

import numpy as np
import jax
import jax.numpy as jnp
from jax import lax
from jax.experimental import pallas as pl
from jax.experimental.pallas import tpu as pltpu

F32 = jnp.float32
BF16 = jnp.bfloat16

GRID_W = 64
HEAD_DIM = 64
NA_HEADS = 8
NA_WIDTH = NA_HEADS * HEAD_DIM
NA_ROWS = 8
NA_COLS = 16
SGU_GROUPS = 6
SGU_GROUP_CH = 128
SGU_WIDTH = SGU_GROUPS * SGU_GROUP_CH
SGU_CHUNK = 128
DIL_PATTERNS = ((128, 1), (512, 4), (2048, 16))
DIL_HEADS = 4
DIL_GW = DIL_HEADS * HEAD_DIM
DIL_WIDTH = len(DIL_PATTERNS) * DIL_GW
ROPE_THETA = 10000.0
N_BRANCH = 3
PEER_HEADS = 8
PEER_NKEYS = 128
PEER_DKEY = 256
PEER_TOPK = 16
LN_EPS = 1e-5
NEG_BIG = -1e30
LOG2E = 1.4426950408889634

LANES = 128
BF16_ROWS = 16
DENSE_SUB = 512
DENSE_LW = 256
DIL_QUERY_BLOCK = 256
NA_ROWS_PER_STEP = 4
ROUTE_LW = 512
VMEM_LIMIT = 48 * 1024 * 1024


def _cparams(sem):
    return pltpu.CompilerParams(dimension_semantics=sem, vmem_limit_bytes=VMEM_LIMIT)


def _ln(x, g, b):
    mu = jnp.mean(x, axis=-1, keepdims=True)
    xc = x - mu
    var = jnp.mean(xc * xc, axis=-1, keepdims=True)
    return xc * lax.rsqrt(var + LN_EPS) * g + b


def _gelu(x):
    return x * (0.5 * (1.0 + jnp.tanh(0.7978845608028654 * (x + 0.044715 * (x * x * x)))))


def _sigmoid(x):
    return 1.0 / (1.0 + jnp.exp(-x))


def _dot_nt(a, b):
    return lax.dot_general(a, b, (((1,), (1,)), ((), ())), preferred_element_type=F32)


def _ln0_kernel(x_ref, g_ref, b_ref, xf_ref, xb_ref):
    y = _ln(x_ref[...], g_ref[...], b_ref[...])
    xf_ref[...] = y
    xb_ref[...] = y.astype(BF16)


def _ln0(x, g, b, tm=512):
    T, D = x.shape
    return pl.pallas_call(
        _ln0_kernel,
        grid=(T // tm,),
        in_specs=[pl.BlockSpec((tm, D), lambda i: (i, 0)),
                  pl.BlockSpec((1, D), lambda i: (0, 0)),
                  pl.BlockSpec((1, D), lambda i: (0, 0))],
        out_specs=[pl.BlockSpec((tm, D), lambda i: (i, 0)),
                   pl.BlockSpec((tm, D), lambda i: (i, 0))],
        out_shape=[jax.ShapeDtypeStruct((T, D), F32), jax.ShapeDtypeStruct((T, D), BF16)],
        compiler_params=_cparams(("parallel",)),
        name="ln0",
    )(x, g.reshape(1, D), b.reshape(1, D))


def _matmul_kernel(x_ref, w_ref, o_ref):
    o_ref[...] = jnp.dot(x_ref[...], w_ref[...], preferred_element_type=F32).astype(o_ref.dtype)


def _inproj(xb, w, tm=1024, tn=2304):
    T, K = xb.shape
    N = w.shape[1]
    tm = min(tm, T)
    return pl.pallas_call(
        _matmul_kernel,
        grid=(T // tm, N // tn),
        in_specs=[pl.BlockSpec((tm, K), lambda i, j: (i, 0)),
                  pl.BlockSpec((K, tn), lambda i, j: (0, j))],
        out_specs=pl.BlockSpec((tm, tn), lambda i, j: (i, j)),
        out_shape=jax.ShapeDtypeStruct((T, N), BF16),
        compiler_params=_cparams(("parallel", "parallel")),
        name="inproj",
    )(xb, w)


def _matmul_residue_major_kernel(x_ref, w_ref, o_ref, acc_ref, *, dil):
    acc = jnp.dot(x_ref[...], w_ref[...], preferred_element_type=F32)
    ntile, tm, _ = acc_ref.shape
    for j in range(ntile):
        acc_ref[j] = acc[:, j * LANES:(j + 1) * LANES]
    for r in range(dil):
        for j in range(ntile):
            c0 = (r * ntile + j) * LANES
            o_ref[:, c0:c0 + LANES] = acc_ref[j, pl.ds(r, tm // dil, stride=dil), :].astype(o_ref.dtype)


def _inproj_residue_major(xb, w, dil, tm=1024):
    T, K = xb.shape
    N = w.shape[1]
    tm = min(tm, T)
    return pl.pallas_call(
        functools.partial(_matmul_residue_major_kernel, dil=dil),
        grid=(T // tm,),
        in_specs=[pl.BlockSpec((tm, K), lambda i: (i, 0)),
                  pl.BlockSpec((K, N), lambda i: (0, 0))],
        out_specs=pl.BlockSpec((tm // dil, dil * N), lambda i: (i, 0)),
        out_shape=jax.ShapeDtypeStruct((T // dil, dil * N), BF16),
        scratch_shapes=[pltpu.VMEM((N // LANES, tm, LANES), F32)],
        compiler_params=_cparams(("parallel",)),
        name=f"inproj_d{dil}",
    )(xb, w)


def _na_kernel(q_ref, k_ref, v_ref, bias_ref, o_ref, s_ref, p_ref, *, rows, rb):
    nk = NA_ROWS * GRID_W
    npair = NA_WIDTH // LANES
    first = lax.broadcasted_iota(jnp.int32, (GRID_W, LANES), 1) < HEAD_DIM
    scale = HEAD_DIM ** -0.5 * LOG2E
    units = []
    for rr in range(rb):
        r = pl.program_id(1) * rb + rr
        rs = jnp.clip(r - NA_ROWS // 2, 0, rows - NA_ROWS)
        for p in range(npair):
            units.append((rr * npair + p, slice(rr * GRID_W, (rr + 1) * GRID_W), slice(p * LANES, (p + 1) * LANES),
                          r - rs, pl.multiple_of(rs * GRID_W, GRID_W), p))
    for u, qrows, cols, case, start, p in units:
        qp = q_ref[0, qrows, cols].astype(F32) * scale
        q2 = jnp.concatenate([jnp.where(first, qp, 0.0), jnp.where(first, 0.0, qp)], axis=0).astype(BF16)
        s_ref[u] = (_dot_nt(q2, k_ref[0, pl.ds(start, nk), cols])
                    + bias_ref[case, p * 2 * GRID_W:(p + 1) * 2 * GRID_W, :])
    for u, *_ in units:
        s = s_ref[u]
        e = jnp.exp2(s - jnp.max(s, axis=-1, keepdims=True))
        p_ref[u] = (e * (1.0 / jnp.sum(e, axis=-1, keepdims=True))).astype(BF16)
    for u, qrows, cols, case, start, p in units:
        o2 = jnp.dot(p_ref[u], v_ref[0, pl.ds(start, nk), cols], preferred_element_type=F32)
        o_ref[0, qrows, cols] = jnp.where(first, o2[:GRID_W], o2[GRID_W:]).astype(o_ref.dtype)


def _na_bias_table(rpb):
    qcol = np.arange(GRID_W)
    kcol = np.arange(GRID_W)
    cstart = np.clip(qcol - NA_COLS // 2, 0, GRID_W - NA_COLS)
    ok = (kcol[None, :] >= cstart[:, None]) & (kcol[None, :] < cstart[:, None] + NA_COLS)
    dc = np.clip(kcol[None, :] - qcol[:, None], -(NA_COLS - 1), NA_COLS - 1) + (NA_COLS - 1)
    b = jnp.where(ok[None, None], rpb[:, :, dc] * LOG2E, NEG_BIG)
    cases = []
    for c in range(NA_ROWS):
        dr = np.arange(NA_ROWS) - c + (NA_ROWS - 1)
        t = b[:, dr]
        cases.append(t.transpose(0, 2, 1, 3).reshape(NA_HEADS * GRID_W, NA_ROWS * GRID_W))
    return jnp.stack(cases, axis=0).astype(F32)


def _na_attention(h3, bias_tab, col0):
    B, S, _ = h3.shape
    rows = S // GRID_W
    cb = col0 // NA_WIDTH
    rb = NA_ROWS_PER_STEP
    return pl.pallas_call(
        functools.partial(_na_kernel, rows=rows, rb=rb),
        grid=(B, rows // rb),
        in_specs=[pl.BlockSpec((1, rb * GRID_W, NA_WIDTH), lambda b, r: (b, r, cb)),
                  pl.BlockSpec((1, S, NA_WIDTH), lambda b, r: (b, 0, cb + 1)),
                  pl.BlockSpec((1, S, NA_WIDTH), lambda b, r: (b, 0, cb + 2)),
                  pl.BlockSpec(bias_tab.shape, lambda b, r: (0, 0, 0))],
        out_specs=pl.BlockSpec((1, rb * GRID_W, NA_WIDTH), lambda b, r: (b, r, 0)),
        out_shape=jax.ShapeDtypeStruct((B, S, NA_WIDTH), BF16),
        scratch_shapes=[pltpu.VMEM((rb * NA_WIDTH // LANES, 2 * GRID_W, NA_ROWS * GRID_W), F32),
                        pltpu.VMEM((rb * NA_WIDTH // LANES, 2 * GRID_W, NA_ROWS * GRID_W), BF16)],
        compiler_params=_cparams(("parallel", "arbitrary")),
        name="na_attn",
    )(h3, h3, h3, bias_tab)


def _sgu_kernel(u_ref, v_ref, g_ref, b_ref, ws_ref, bs_ref, o_ref, *, nchunk):
    for c in range(nchunk):
        rs = slice(c * SGU_CHUNK, (c + 1) * SGU_CHUNK)
        v = _ln(_gelu(v_ref[0, rs, :].astype(F32)), g_ref[...], b_ref[...]).astype(BF16)
        u = _gelu(u_ref[0, rs, :].astype(F32))
        for g in range(SGU_GROUPS):
            cs = slice(g * SGU_GROUP_CH, (g + 1) * SGU_GROUP_CH)
            s = jnp.dot(ws_ref[g], v[:, cs], preferred_element_type=F32) + bs_ref[g]
            o_ref[0, rs, cs] = (u[:, cs] * s).astype(o_ref.dtype)


def _sgu(h3, ln_g, ln_b, ws, bs, col0, tc=512):
    B, S, _ = h3.shape
    tc = min(tc, S)
    cb = col0 // SGU_WIDTH
    bs_b = jnp.broadcast_to(bs[:, :, None], (SGU_GROUPS, SGU_CHUNK, SGU_GROUP_CH)).astype(F32)
    return pl.pallas_call(
        functools.partial(_sgu_kernel, nchunk=tc // SGU_CHUNK),
        grid=(B, S // tc),
        in_specs=[pl.BlockSpec((1, tc, SGU_WIDTH), lambda b, i: (b, i, cb)),
                  pl.BlockSpec((1, tc, SGU_WIDTH), lambda b, i: (b, i, cb + 1)),
                  pl.BlockSpec((1, SGU_WIDTH), lambda b, i: (0, 0)),
                  pl.BlockSpec((1, SGU_WIDTH), lambda b, i: (0, 0)),
                  pl.BlockSpec((SGU_GROUPS, SGU_CHUNK, SGU_CHUNK), lambda b, i: (0, 0, 0)),
                  pl.BlockSpec((SGU_GROUPS, SGU_CHUNK, SGU_GROUP_CH), lambda b, i: (0, 0, 0))],
        out_specs=pl.BlockSpec((1, tc, SGU_WIDTH), lambda b, i: (b, i, 0)),
        out_shape=jax.ShapeDtypeStruct((B, S, SGU_WIDTH), BF16),
        compiler_params=_cparams(("parallel", "parallel")),
        name="sgu",
    )(h3, h3, ln_g.reshape(1, -1), ln_b.reshape(1, -1), ws.astype(BF16), bs_b)


def _rope_tile(x, c, sa, sb):
    return x * c + pltpu.roll(x, LANES - HEAD_DIM // 2, 1) * sa + pltpu.roll(x, HEAD_DIM // 2, 1) * sb


def _dil_kernel(q_ref, k_ref, v_ref, c_ref, sa_ref, sb_ref, o_ref, lse_ref, kr_ref, s_ref, p_ref, l_ref,
                *, L, QB, KB, half, dil):
    qi = pl.program_id(2)
    ntile = DIL_GW // LANES
    rchunk = min(L, 256)

    @pl.when(qi == 0)
    def _():
        for c0 in range(0, L, rchunk):
            rs = slice(c0, c0 + rchunk)
            for t in range(ntile):
                cs = slice(t * LANES, (t + 1) * LANES)
                kr_ref[rs, cs] = _rope_tile(k_ref[0, rs, cs].astype(F32), c_ref[rs, cs], sa_ref[rs, cs],
                                            sb_ref[rs, cs]).astype(BF16)

    n0 = pl.multiple_of(qi * QB, QB)
    start = pl.multiple_of(jnp.clip(n0 - half, 0, L - KB), 16)
    qn = n0 + lax.broadcasted_iota(jnp.int32, (QB, KB), 0)
    kn = start + lax.broadcasted_iota(jnp.int32, (QB, KB), 1)
    ok = jnp.abs(kn - qn) <= half
    first = lax.broadcasted_iota(jnp.int32, (QB, LANES), 1) < HEAD_DIM
    scale = HEAD_DIM ** -0.5 * LOG2E
    if dil == 1:
        orows = slice(None)
    else:
        orows = pl.ds(n0 * dil + pl.program_id(1), QB, stride=dil)
    for t in range(ntile):
        cs = slice(t * LANES, (t + 1) * LANES)
        qr = _rope_tile(q_ref[0, :, cs].astype(F32), c_ref[pl.ds(n0, QB), cs], sa_ref[pl.ds(n0, QB), cs],
                        sb_ref[pl.ds(n0, QB), cs]) * scale
        kt = kr_ref[pl.ds(start, KB), cs]
        for sub in range(2):
            qm = jnp.where(first, qr, 0.0) if sub == 0 else jnp.where(first, 0.0, qr)
            s_ref[2 * t + sub] = jnp.where(ok, _dot_nt(qm.astype(BF16), kt), NEG_BIG)
    for u in range(2 * ntile):
        s = s_ref[u]
        m = jnp.max(s, axis=-1, keepdims=True)
        e = jnp.exp2(s - m)
        den = jnp.sum(e, axis=-1, keepdims=True)
        p_ref[u] = (e * (1.0 / den)).astype(BF16)
        l_ref[u] = jnp.broadcast_to(m * (1.0 / LOG2E) + jnp.log(den), (QB, LANES))
    for t in range(ntile):
        vt = v_ref[0, pl.ds(start, KB), t * LANES:(t + 1) * LANES]
        outs = [jnp.dot(p_ref[2 * t + sub], vt, preferred_element_type=F32) for sub in range(2)]
        o_ref[0, t, orows, :] = jnp.where(first, outs[0], outs[1])
        lse_ref[0, t, orows, :] = jnp.where(first, l_ref[2 * t], l_ref[2 * t + 1])


def _rope_tables(S):
    half = HEAD_DIM // 2
    inv = ROPE_THETA ** (-jnp.arange(half, dtype=F32) / half)
    ang = jnp.arange(S).astype(F32)[:, None] * inv[None, :]
    cos, sin = jnp.cos(ang), jnp.sin(ang)
    z = jnp.zeros_like(sin)
    c = jnp.tile(jnp.concatenate([cos, cos], axis=-1), (1, DIL_HEADS))
    sa = jnp.tile(jnp.concatenate([-sin, z], axis=-1), (1, DIL_HEADS))
    sb = jnp.tile(jnp.concatenate([z, sin], axis=-1), (1, DIL_HEADS))
    return c, sa, sb


def _dil_attention(hsrc, tabs, g, B, S, col0=0):
    window, dil = DIL_PATTERNS[g]
    L = S // dil
    half = window // (2 * dil)
    QB = min(L, DIL_QUERY_BLOCK)
    KB = min(L, QB + 2 * half)
    ntile = DIL_GW // LANES
    if dil == 1:
        hv, cpb, qb = hsrc, hsrc.shape[-1] // DIL_GW, col0 // DIL_GW
        out_spec = pl.BlockSpec((1, ntile, QB, LANES), lambda b, r, i: (b, 0, i, 0))
    else:
        hv, cpb, qb = hsrc.reshape(B, L, dil * 3 * DIL_GW), 3, 0
        out_spec = pl.BlockSpec((1, ntile, S, LANES), lambda b, r, i: (b, 0, 0, 0))
    c, sa, sb = [t.reshape(L, dil * DIL_GW) for t in tabs]
    tab_spec = pl.BlockSpec((L, DIL_GW), lambda b, r, i: (0, r))
    o, lse = pl.pallas_call(
        functools.partial(_dil_kernel, L=L, QB=QB, KB=KB, half=half, dil=dil),
        grid=(B, dil, L // QB),
        in_specs=[pl.BlockSpec((1, QB, DIL_GW), lambda b, r, i: (b, i, r * cpb + qb)),
                  pl.BlockSpec((1, L, DIL_GW), lambda b, r, i: (b, 0, r * cpb + qb + 1)),
                  pl.BlockSpec((1, L, DIL_GW), lambda b, r, i: (b, 0, r * cpb + qb + 2)),
                  tab_spec, tab_spec, tab_spec],
        out_specs=[out_spec, out_spec],
        out_shape=[jax.ShapeDtypeStruct((B, ntile, S, LANES), F32)] * 2,
        scratch_shapes=[pltpu.VMEM((L, DIL_GW), BF16), pltpu.VMEM((DIL_HEADS, QB, KB), F32),
                        pltpu.VMEM((DIL_HEADS, QB, KB), BF16), pltpu.VMEM((DIL_HEADS, QB, LANES), F32)],
        compiler_params=_cparams(("parallel", "arbitrary", "arbitrary")),
        name=f"dil_attn_g{g}",
    )(hv, hv, hv, c, sa, sb)
    return o, lse


def _merge_kernel(x_ref, gate_ref, a_ref, b_ref, o0_ref, o1_ref, o2_ref, l0_ref, l1_ref, l2_ref, p_ref,
                  wa_ref, wb_ref, wc_ref, wo_ref, g1_ref, b1_ref, wpg_ref, wp_ref,
                  xb_ref, base_ref, *, alpha, D):
    a = jnp.dot(a_ref[...], wa_ref[...], preferred_element_type=F32)
    b = jnp.dot(b_ref[...], wb_ref[...], preferred_element_type=F32)
    halves = []
    for t in range(DIL_GW // LANES):
        l0, l1, l2 = l0_ref[0, t], l1_ref[0, t], l2_ref[0, t]
        m = jnp.maximum(jnp.maximum(l0, l1), l2)
        e0, e1, e2 = jnp.exp(l0 - m), jnp.exp(l1 - m), jnp.exp(l2 - m)
        halves.append((e0 * o0_ref[0, t] + e1 * o1_ref[0, t] + e2 * o2_ref[0, t]) * (1.0 / (e0 + e1 + e2)))
    cin = jnp.concatenate(halves, axis=-1)
    c = jnp.dot(cin.astype(BF16), wc_ref[...], preferred_element_type=F32)
    merged = (_sigmoid(gate_ref[:, 0:D].astype(F32)) * a
              + _sigmoid(gate_ref[:, D:2 * D].astype(F32)) * b
              + _sigmoid(gate_ref[:, 2 * D:3 * D].astype(F32)) * c)
    mix = jnp.dot(merged.astype(BF16), wo_ref[...], preferred_element_type=F32)
    x1 = _ln(alpha * x_ref[...] + mix, g1_ref[...], b1_ref[...])
    x1b = x1.astype(BF16)
    xb_ref[...] = x1b
    ple = (_sigmoid(jnp.dot(x1b, wpg_ref[...], preferred_element_type=F32))
           * jnp.dot(p_ref[...].astype(BF16), wp_ref[...], preferred_element_type=F32))
    base_ref[...] = alpha * x1 + ple


def _merge(x, h, na, sgu, dil_o, dil_l, p, wa, wb, wc, wo, g1, b1, wpg, wp, alpha, S, tm=512):
    T, D = x.shape
    tm = min(tm, S)
    per_seq = S // tm
    row = lambda w: pl.BlockSpec((tm, w), lambda i: (i, 0))
    full = lambda arr: pl.BlockSpec(arr.shape, lambda i: (0,) * arr.ndim)
    dil = pl.BlockSpec((1, DIL_GW // LANES, tm, LANES), lambda i: (i // per_seq, 0, i % per_seq, 0))
    g1, b1 = g1.reshape(1, D), b1.reshape(1, D)
    return pl.pallas_call(
        functools.partial(_merge_kernel, alpha=alpha, D=D),
        grid=(T // tm,),
        in_specs=[row(D), row(N_BRANCH * D), row(NA_WIDTH), row(SGU_WIDTH),
                  dil, dil, dil, dil, dil, dil,
                  row(p.shape[1]),
                  full(wa), full(wb), full(wc), full(wo), full(g1), full(b1), full(wpg), full(wp)],
        out_specs=[row(D), row(D)],
        out_shape=[jax.ShapeDtypeStruct((T, D), BF16), jax.ShapeDtypeStruct((T, D), F32)],
        compiler_params=_cparams(("parallel",)),
        name="merge",
    )(x, h, na, sgu, dil_o[0], dil_o[1], dil_o[2], dil_l[0], dil_l[1], dil_l[2], p,
      wa, wb, wc, wo, g1, b1, wpg, wp)


_CAND_ROWS = 16 + 7 * 8 + 8


def _cand_index():
    idx = [float(j) for j in range(16)]
    for i in range(1, 8):
        idx += [float(i * 16 + j) for j in range(8)]
    idx += [float(i * 16) for i in range(8, 16)]
    return np.broadcast_to(np.asarray(idx, np.float32)[:, None], (_CAND_ROWS, ROUTE_LW)).copy()


def _extract_top(s, order, n):
    rank = jnp.full(s.shape, 127.0, F32)
    vals = []
    for i in range(n):
        m = jnp.max(s, axis=0, keepdims=True)
        pick = jnp.min(jnp.where(s == m, order, 1e9), axis=0, keepdims=True)
        sel = order == pick
        rank = jnp.where(sel, float(i), rank)
        s = jnp.where(sel, -jnp.inf, s)
        vals.append(m)
    return rank, vals


def _batcher_pairs(n):
    pairs = []

    def merge(lo, hi, r):
        step = r * 2
        if step < hi - lo:
            merge(lo, hi, step)
            merge(lo + r, hi, step)
            pairs.extend((i, i + r) for i in range(lo + r, hi - r, step))
        else:
            pairs.append((lo, lo + r))

    def sort(lo, hi):
        if hi - lo >= 1:
            mid = lo + (hi - lo) // 2
            sort(lo, mid)
            sort(mid + 1, hi)
            merge(lo, hi, 1)

    sort(0, n - 1)
    return pairs


def _sort_desc(xs):
    xs = list(xs)
    for i, j in _batcher_pairs(PEER_TOPK):
        if j < len(xs):
            xs[i], xs[j] = jnp.maximum(xs[i], xs[j]), jnp.minimum(xs[i], xs[j])
    return xs


def _merge_sublanes(xs):
    xs = list(xs)
    n = PEER_TOPK
    for shift in (4, 2, 1):
        t = [jnp.maximum(xs[i], pltpu.roll(xs[n - 1 - i], shift, 0)) for i in range(n)]
        for d in (8, 4, 2, 1):
            for i in range(n):
                if i & d == 0:
                    t[i], t[i + d] = jnp.maximum(t[i], t[i + d]), jnp.minimum(t[i], t[i + d])
        xs = t
    return xs


def _search_bits(x, v):
    c3 = x >= v[7]
    c2 = x >= jnp.where(c3, v[3], v[11])
    c1 = x >= jnp.where(c3, jnp.where(c2, v[1], v[5]), jnp.where(c2, v[9], v[13]))
    c0 = x >= jnp.where(c3,
                        jnp.where(c2, jnp.where(c1, v[0], v[2]), jnp.where(c1, v[4], v[6])),
                        jnp.where(c2, jnp.where(c1, v[8], v[10]), jnp.where(c1, v[12], v[14])))
    return c3, c2, c1, c0


def _pick_by_rank(bits, vals):
    c3, c2, c1, c0 = bits
    l0 = [jnp.where(c0, vals[2 * k], vals[2 * k + 1]) for k in range(8)]
    l1 = [jnp.where(c1, l0[2 * k], l0[2 * k + 1]) for k in range(4)]
    l2 = [jnp.where(c2, l1[2 * k], l1[2 * k + 1]) for k in range(2)]
    return jnp.where(c3, l2[0], l2[1])


def _route_sorted(s1, s2):
    W = s1.shape[1]
    sub_rows = 8
    g1 = [s1[sub_rows * g:sub_rows * (g + 1)] for g in range(PEER_NKEYS // sub_rows)]
    g2 = [s2[sub_rows * g:sub_rows * (g + 1)] for g in range(PEER_NKEYS // sub_rows)]
    v1 = _merge_sublanes(_sort_desc(g1))
    v2 = _merge_sublanes(_sort_desc(g2))
    sub = lax.broadcasted_iota(jnp.int32, (sub_rows, W), 0)

    def pack(vs):
        out = vs[0]
        for s in range(1, sub_rows):
            out = jnp.where(sub == s, vs[s], out)
        return out

    e1r = [jnp.exp(v - v1[0]) for v in v1]
    e2r = [jnp.exp(v - v2[0]) for v in v2]
    v2lo, v2hi, v1hi = pack(v2[:8]), pack(v2[8:]), pack(v1[8:])
    e2lo, e2hi, e1hi = pack(e2r[:8]), pack(e2r[8:]), pack(e1r[8:])
    cand = [v1[0] + v2lo, v1[0] + v2hi] + [v1[i] + v2lo for i in range(1, 8)] + [v1hi + v2[0]]
    ecand = [e1r[0] * e2lo, e1r[0] * e2hi] + [e1r[i] * e2lo for i in range(1, 8)] + [e1hi * e2r[0]]
    neg = jnp.full((sub_rows, W), -jnp.inf, F32)
    tau = _merge_sublanes(_sort_desc(cand) + [neg] * (PEER_TOPK - len(cand)))[PEER_TOPK - 1]
    sel = [jnp.where(c >= tau, 1.0, 0.0) for c in cand]
    colsum = lambda x: jnp.sum(x, axis=0, keepdims=True)
    z = colsum(sum(s * e for s, e in zip(sel, ecand)))
    cnt = [colsum(sel[0] + sel[1])] + [colsum(sel[1 + i]) for i in range(1, 8)] + [sel[9][i:i + 1] for i in range(8)]
    half_minus_cnt = [0.5 - c for c in cnt]
    inv_z = 1.0 / z
    a1, nx1, e2, x2 = [], [], [], []
    n1 = jnp.zeros((sub_rows, W), F32)
    n2 = jnp.zeros((sub_rows, W), F32)
    for x in g1:
        inside = x >= v1[PEER_TOPK - 1]
        nx1.append(jnp.where(inside, _pick_by_rank(_search_bits(x, v1), half_minus_cnt), 1e9))
        a1.append(jnp.where(inside, jnp.exp(x - v1[0]), 0.0) * inv_z)
        n1 = n1 + jnp.where(inside, 1.0, 0.0)
    for x in g2:
        inside = x >= v2[PEER_TOPK - 1]
        c3, c2, c1, c0 = _search_bits(x, v2)
        rank = (jnp.where(c3, 0.0, 8.0) + jnp.where(c2, 0.0, 4.0)) + (jnp.where(c1, 0.0, 2.0) + jnp.where(c0, 0.0, 1.0))
        x2.append(jnp.where(inside, -rank, -127.0))
        e2.append(jnp.exp(x - v2[0]))
        n2 = n2 + jnp.where(inside, 1.0, 0.0)
    dup = jnp.zeros((sub_rows, W), F32)
    for i in range(PEER_TOPK - 1):
        dup = dup + jnp.where(v1[i] == v1[i + 1], 1.0, 0.0) + jnp.where(v2[i] == v2[i + 1], 1.0, 0.0)
    flag = (jnp.abs(colsum(n1) - PEER_TOPK) + jnp.abs(colsum(n2) - PEER_TOPK)
            + jnp.abs(colsum(sum(sel)) - PEER_TOPK) + colsum(dup))
    cat = lambda xs: jnp.concatenate(xs, axis=0)
    return cat(a1), cat(nx1), cat(e2), cat(x2), flag


def _row_tile(ref, h, a, ls):
    row = ref[h, a:a + 1, ls]
    return jnp.broadcast_to(row, (BF16_ROWS, row.shape[1])).astype(BF16)


def _route_chunk(s1, s2, cidx):
    key_order = lax.broadcasted_iota(jnp.int32, s1.shape, 0).astype(F32)
    rank1, v1 = _extract_top(s1, key_order, PEER_TOPK)
    rank2, v2 = _extract_top(s2, key_order, PEER_TOPK)
    v1a = jnp.concatenate(v1, axis=0)
    v2a = jnp.concatenate(v2, axis=0)
    e1r = jnp.exp(v1a - v1[0])
    e2r = jnp.exp(v2a - v2[0])

    def pairs(r1, r2, op):
        blocks = [op(r1[0:1], r2)]
        blocks += [op(r1[i:i + 1], r2[0:8]) for i in range(1, 8)]
        blocks += [op(r1[8:16], r2[0:1])]
        return jnp.concatenate(blocks, axis=0)

    cand = pairs(v1a, v2a, jnp.add)
    ecand = pairs(e1r, e2r, jnp.multiply)
    crank, _ = _extract_top(cand, cidx, PEER_TOPK)
    sel = jnp.where(crank < 100.0, 1.0, 0.0)
    z = jnp.sum(sel * ecand, axis=0, keepdims=True)
    cnt = [jnp.sum(sel[0:16], axis=0, keepdims=True)]
    cnt += [jnp.sum(sel[16 + 8 * (i - 1):16 + 8 * i], axis=0, keepdims=True) for i in range(1, 8)]
    cnt += [sel[72 + i:73 + i] for i in range(8)]
    nx1 = jnp.full(s1.shape, 1e9, F32)
    for i in range(PEER_TOPK):
        nx1 = jnp.where(rank1 == float(i), 0.5 - cnt[i], nx1)
    a1 = jnp.where(rank1 < 100.0, jnp.exp(s1 - v1[0]), 0.0) * (1.0 / z)
    e2 = jnp.exp(s2 - v2[0])
    return a1, nx1, e2, -rank2


def _route_kernel(x_ref, wq_ref, k1_ref, k2_ref, cidx_ref, a1_ref, nx1_ref, e2_ref, x2_ref, q_ref, s1_ref, s2_ref,
                  *, tb):
    hk = PEER_DKEY // 2
    h = pl.program_id(1)

    @pl.when(h == 0)
    def _():
        for hh in range(PEER_HEADS):
            q_ref[hh] = jnp.dot(x_ref[...], wq_ref[:, hh * PEER_DKEY:(hh + 1) * PEER_DKEY],
                                preferred_element_type=F32).astype(BF16)

    s1_ref[...] = _dot_nt(k1_ref[0], q_ref[h, :, :hk])
    s2_ref[...] = _dot_nt(k2_ref[0], q_ref[h, :, hk:])

    def chunk(c, carry):
        ls = pl.ds(pl.multiple_of(c * ROUTE_LW, ROUTE_LW), ROUTE_LW)

        def emit(a1, nx1, e2, x2):
            a1_ref[0, :, ls] = a1
            nx1_ref[0, :, ls] = nx1
            e2_ref[0, :, ls] = e2.astype(e2_ref.dtype)
            x2_ref[0, :, ls] = x2.astype(x2_ref.dtype)

        *fast, flag = _route_sorted(s1_ref[:, ls], s2_ref[:, ls])
        emit(*fast)

        @pl.when(jnp.max(flag) > 0.0)
        def _():
            emit(*_route_chunk(s1_ref[:, ls], s2_ref[:, ls], cidx_ref[...]))

        return carry

    lax.fori_loop(0, tb // ROUTE_LW, chunk, 0)


def _peer_route(xb, wq, k1, k2, tb):
    T, D = xb.shape
    H = PEER_HEADS
    cidx = jnp.asarray(_cand_index())
    out_spec = pl.BlockSpec((1, PEER_NKEYS, tb), lambda i, h: (h, 0, i))
    out_sds = lambda dt: jax.ShapeDtypeStruct((H, PEER_NKEYS, T), dt)
    return pl.pallas_call(
        functools.partial(_route_kernel, tb=tb),
        grid=(T // tb, H),
        in_specs=[pl.BlockSpec((tb, D), lambda i, h: (i, 0)),
                  pl.BlockSpec((D, H * PEER_DKEY), lambda i, h: (0, 0)),
                  pl.BlockSpec((1, PEER_NKEYS, PEER_DKEY // 2), lambda i, h: (h, 0, 0)),
                  pl.BlockSpec((1, PEER_NKEYS, PEER_DKEY // 2), lambda i, h: (h, 0, 0)),
                  pl.BlockSpec((_CAND_ROWS, ROUTE_LW), lambda i, h: (0, 0))],
        out_specs=[out_spec] * 4,
        out_shape=[out_sds(F32), out_sds(F32), out_sds(BF16), out_sds(BF16)],
        scratch_shapes=[pltpu.VMEM((H, tb, PEER_DKEY), BF16),
                        pltpu.VMEM((PEER_NKEYS, tb), F32), pltpu.VMEM((PEER_NKEYS, tb), F32)],
        compiler_params=_cparams(("parallel", "arbitrary")),
        name="peer_route",
    )(xb, wq, k1, k2, cidx)


def _dense_kernel(x_ref, u_ref, vt_ref, a1_ref, nx1_ref, e2_ref, x2_ref, base_ref, g_ref, b_ref,
                  yf_ref, yb_ref, acc_ref, act_ref, pw_ref, *, eb):
    e = pl.program_id(1)
    last = pl.num_programs(1) - 1
    tb = x_ref.shape[0]
    zero = jnp.zeros((), BF16)

    @pl.when(e == 0)
    def _():
        acc_ref[...] = jnp.zeros_like(acc_ref)

    for sb in range(eb // DENSE_SUB):
        rows_e = slice(sb * DENSE_SUB, (sb + 1) * DENSE_SUB)
        act_ref[rows_e, :] = _gelu(_dot_nt(u_ref[rows_e, :], x_ref[...]).astype(BF16))
    nk = PEER_NKEYS // BF16_ROWS

    for lt in range(tb // DENSE_LW):
        ls = slice(lt * DENSE_LW, (lt + 1) * DENSE_LW)
        for al in range(eb // PEER_NKEYS):
            w = [None] * nk
            for h in range(PEER_HEADS):
                nxb = _row_tile(nx1_ref, h, al, ls)
                a1b = _row_tile(a1_ref, h, al, ls)
                for k in range(nk):
                    rows = slice(k * BF16_ROWS, (k + 1) * BF16_ROWS)
                    term = jnp.where(x2_ref[h, rows, ls] >= nxb, e2_ref[h, rows, ls], zero) * a1b
                    w[k] = term if w[k] is None else w[k] + term
            for k in range(nk):
                r0 = al * PEER_NKEYS + k * BF16_ROWS
                pw_ref[r0:r0 + BF16_ROWS, ls] = w[k] * act_ref[r0:r0 + BF16_ROWS, ls]
        acc_ref[:, ls] += jnp.dot(vt_ref[...], pw_ref[:, ls], preferred_element_type=F32)

    @pl.when(e == last)
    def _():
        y = _ln(base_ref[...] + acc_ref[...].T, g_ref[...], b_ref[...])
        yf_ref[...] = y
        yb_ref[...] = y.astype(BF16)


def _peer_dense(xb, u, vt, route, base, g2, b2, tb, eb=2048):
    T, D = xb.shape
    NE = u.shape[0]
    H = PEER_HEADS
    rspec = pl.BlockSpec((H, PEER_NKEYS, tb), lambda i, e: (0, 0, i))
    aspec = pl.BlockSpec((H, eb // PEER_NKEYS, tb), lambda i, e: (0, e, i))
    return pl.pallas_call(
        functools.partial(_dense_kernel, eb=eb),
        grid=(T // tb, NE // eb),
        in_specs=[pl.BlockSpec((tb, D), lambda i, e: (i, 0)),
                  pl.BlockSpec((eb, D), lambda i, e: (e, 0)),
                  pl.BlockSpec((D, eb), lambda i, e: (0, e)),
                  aspec, aspec, rspec, rspec,
                  pl.BlockSpec((tb, D), lambda i, e: (i, 0)),
                  pl.BlockSpec((1, D), lambda i, e: (0, 0)),
                  pl.BlockSpec((1, D), lambda i, e: (0, 0))],
        out_specs=[pl.BlockSpec((tb, D), lambda i, e: (i, 0)),
                   pl.BlockSpec((tb, D), lambda i, e: (i, 0))],
        out_shape=[jax.ShapeDtypeStruct((T, D), F32), jax.ShapeDtypeStruct((T, D), BF16)],
        scratch_shapes=[pltpu.VMEM((D, tb), F32), pltpu.VMEM((eb, tb), BF16), pltpu.VMEM((eb, tb), BF16)],
        compiler_params=_cparams(("parallel", "arbitrary")),
        name="peer_dense",
    )(xb, u, vt, route[0], route[1], route[2], route[3], base, g2.reshape(1, D), b2.reshape(1, D))


def _layer_weights(i, w_in, rpb, sgu_ln_g, sgu_ln_b, sgu_w, sgu_b, w_br_a, w_br_b, w_br_c, w_out, ln1_g, ln1_b,
                   peer_wq, peer_k1, peer_k2, peer_u, peer_v, ple_w, ple_gate_w, ln2_g, ln2_b):
    D = w_in.shape[1]
    off_c = 3 * NA_WIDTH + 2 * SGU_WIDTH + 3 * DIL_WIDTH
    off_b = 3 * NA_WIDTH + 2 * SGU_WIDTH

    def group_w(g):
        return jnp.concatenate(
            [w_in[i][:, off_b + part * DIL_WIDTH + g * DIL_GW: off_b + part * DIL_WIDTH + (g + 1) * DIL_GW]
             for part in range(3)], axis=1).astype(BF16)

    w_perm = jnp.concatenate([w_in[i][:, off_c:].astype(BF16), w_in[i][:, :off_b].astype(BF16), group_w(0)], axis=1)
    return dict(
        w_in=w_perm, w_dil={g: group_w(g) for g in range(1, len(DIL_PATTERNS))},
        col_a=N_BRANCH * D, col_b=N_BRANCH * D + 3 * NA_WIDTH,
        col_c=N_BRANCH * D + 3 * NA_WIDTH + 2 * SGU_WIDTH,
        na_bias=_na_bias_table(rpb[i]),
        sgu_ln_g=sgu_ln_g[i], sgu_ln_b=sgu_ln_b[i], sgu_w=sgu_w[i], sgu_b=sgu_b[i],
        wa=w_br_a[i].astype(BF16), wb=w_br_b[i].astype(BF16), wc=w_br_c[i].astype(BF16),
        wo=w_out[i].astype(BF16), ln1_g=ln1_g[i], ln1_b=ln1_b[i],
        wq=peer_wq[i].astype(BF16), k1=peer_k1[i].astype(BF16), k2=peer_k2[i].astype(BF16),
        u=peer_u[i].astype(BF16), vt=peer_v[i].astype(BF16).T,
        wp=ple_w[i].astype(BF16), wpg=ple_gate_w[i].astype(BF16), ln2_g=ln2_g[i], ln2_b=ln2_b[i])


def _trunk(x, p, ln0_g, ln0_b, layers, alpha):
    B, S, D = x.shape
    T = B * S
    tb = min(512, T)
    tabs = _rope_tables(S)
    xf, xb = _ln0(x.reshape(T, D), ln0_g, ln0_b)
    for i, lw in enumerate(layers):
        h = _inproj(xb, lw["w_in"])
        h3 = h.reshape(B, S, -1)
        na = _na_attention(h3, lw["na_bias"], lw["col_a"]).reshape(T, NA_WIDTH)
        sg = _sgu(h3, lw["sgu_ln_g"], lw["sgu_ln_b"], lw["sgu_w"], lw["sgu_b"], lw["col_b"]).reshape(T, SGU_WIDTH)
        dil = [_dil_attention(h3, tabs, 0, B, S, lw["col_c"])]
        for g in range(1, len(DIL_PATTERNS)):
            hd = _inproj_residue_major(xb, lw["w_dil"][g], DIL_PATTERNS[g][1])
            dil.append(_dil_attention(hd, tabs, g, B, S))
        x1b, base = _merge(xf, h, na, sg, [d[0] for d in dil], [d[1] for d in dil], p[i].reshape(T, -1),
                           lw["wa"], lw["wb"], lw["wc"], lw["wo"], lw["ln1_g"], lw["ln1_b"],
                           lw["wpg"], lw["wp"], alpha, S)
        route = _peer_route(x1b, lw["wq"], lw["k1"], lw["k2"], tb)
        xf, xb = _peer_dense(x1b, lw["u"], lw["vt"], route, base, lw["ln2_g"], lw["ln2_b"], tb)
    return xf.reshape(B, S, D)


def kernel(x_prompt, x_sample, p_prompt, p_sample, ln0_g, ln0_b, w_in, rpb, sgu_ln_g, sgu_ln_b, sgu_w, sgu_b, w_br_a, w_br_b, w_br_c, w_out, ln1_g, ln1_b, peer_wq, peer_k1, peer_k2, peer_u, peer_v, ple_w, ple_gate_w, ln2_g, ln2_b):
    depth = w_in.shape[0]
    alpha = (2 * depth) ** 0.25
    layers = [_layer_weights(i, w_in, rpb, sgu_ln_g, sgu_ln_b, sgu_w, sgu_b, w_br_a, w_br_b, w_br_c, w_out,
                             ln1_g, ln1_b, peer_wq, peer_k1, peer_k2, peer_u, peer_v, ple_w, ple_gate_w,
                             ln2_g, ln2_b) for i in range(depth)]
    y_prompt = _trunk(x_prompt, p_prompt, ln0_g, ln0_b, layers, alpha)
    y_sample = _trunk(x_sample, p_sample, ln0_g, ln0_b, layers, alpha)
    return (y_prompt, y_sample)
```

```python
import functools

import numpy as np
import jax
import jax.numpy as jnp
from jax import lax
from jax.experimental import pallas as pl
from jax.experimental.pallas import tpu as pltpu

F32 = jnp.float32
BF16 = jnp.bfloat16

GRID_W = 64
HEAD_DIM = 64
NA_HEADS = 8
NA_WIDTH = NA_HEADS * HEAD_DIM
NA_ROWS = 8
NA_COLS = 16
SGU_GROUPS = 6
SGU_GROUP_CH = 128
SGU_WIDTH = SGU_GROUPS * SGU_GROUP_CH
SGU_CHUNK = 128
DIL_PATTERNS = ((128, 1), (512, 4), (2048, 16))
DIL_HEADS = 4
DIL_GW = DIL_HEADS * HEAD_DIM
DIL_WIDTH = len(DIL_PATTERNS) * DIL_GW
ROPE_THETA = 10000.0
N_BRANCH = 3
PEER_HEADS = 8
PEER_NKEYS = 128
PEER_DKEY = 256
PEER_TOPK = 16
LN_EPS = 1e-5
NEG_BIG = -1e30
LOG2E = 1.4426950408889634

LANES = 128
BF16_ROWS = 16
DENSE_SUB = 512
DENSE_LW = 256
DIL_QUERY_BLOCK = 256
NA_ROWS_PER_STEP = 4
ROUTE_TB = 1024
ROUTE_LW = 512
VMEM_LIMIT = 48 * 1024 * 1024


def _cparams(sem):
    return pltpu.CompilerParams(dimension_semantics=sem, vmem_limit_bytes=VMEM_LIMIT)


def _ln(x, g, b):
    mu = jnp.mean(x, axis=-1, keepdims=True)
    xc = x - mu
    var = jnp.mean(xc * xc, axis=-1, keepdims=True)
    return xc * lax.rsqrt(var + LN_EPS) * g + b


def _gelu(x):
    return x * (0.5 * (1.0 + jnp.tanh(0.7978845608028654 * (x + 0.044715 * (x * x * x)))))


def _sigmoid(x):
    return 1.0 / (1.0 + jnp.exp(-x))


def _dot_nt(a, b):
    return lax.dot_general(a, b, (((1,), (1,)), ((), ())), preferred_element_type=F32)


def _ln0_kernel(x_ref, g_ref, b_ref, xf_ref, xb_ref):
    y = _ln(x_ref[...], g_ref[...], b_ref[...])
    xf_ref[...] = y
    xb_ref[...] = y.astype(BF16)


def _ln0(x, g, b, tm=512):
    T, D = x.shape
    return pl.pallas_call(
        _ln0_kernel,
        grid=(T // tm,),
        in_specs=[pl.BlockSpec((tm, D), lambda i: (i, 0)),
                  pl.BlockSpec((1, D), lambda i: (0, 0)),
                  pl.BlockSpec((1, D), lambda i: (0, 0))],
        out_specs=[pl.BlockSpec((tm, D), lambda i: (i, 0)),
                   pl.BlockSpec((tm, D), lambda i: (i, 0))],
        out_shape=[jax.ShapeDtypeStruct((T, D), F32), jax.ShapeDtypeStruct((T, D), BF16)],
        compiler_params=_cparams(("parallel",)),
        name="ln0",
    )(x, g.reshape(1, D), b.reshape(1, D))


def _matmul_kernel(x_ref, w_ref, o_ref):
    o_ref[...] = jnp.dot(x_ref[...], w_ref[...], preferred_element_type=F32).astype(o_ref.dtype)


def _inproj(xb, w, tm=1024, tn=2304):
    T, K = xb.shape
    N = w.shape[1]
    tm = min(tm, T)
    return pl.pallas_call(
        _matmul_kernel,
        grid=(T // tm, N // tn),
        in_specs=[pl.BlockSpec((tm, K), lambda i, j: (i, 0)),
                  pl.BlockSpec((K, tn), lambda i, j: (0, j))],
        out_specs=pl.BlockSpec((tm, tn), lambda i, j: (i, j)),
        out_shape=jax.ShapeDtypeStruct((T, N), BF16),
        compiler_params=_cparams(("parallel", "parallel")),
        name="inproj",
    )(xb, w)


def _matmul_residue_major_kernel(x_ref, w_ref, o_ref, acc_ref, *, dil):
    acc = jnp.dot(x_ref[...], w_ref[...], preferred_element_type=F32)
    ntile, tm, _ = acc_ref.shape
    for j in range(ntile):
        acc_ref[j] = acc[:, j * LANES:(j + 1) * LANES]
    for r in range(dil):
        for j in range(ntile):
            c0 = (r * ntile + j) * LANES
            o_ref[:, c0:c0 + LANES] = acc_ref[j, pl.ds(r, tm // dil, stride=dil), :].astype(o_ref.dtype)


def _inproj_residue_major(xb, w, dil, tm=1024):
    T, K = xb.shape
    N = w.shape[1]
    tm = min(tm, T)
    return pl.pallas_call(
        functools.partial(_matmul_residue_major_kernel, dil=dil),
        grid=(T // tm,),
        in_specs=[pl.BlockSpec((tm, K), lambda i: (i, 0)),
                  pl.BlockSpec((K, N), lambda i: (0, 0))],
        out_specs=pl.BlockSpec((tm // dil, dil * N), lambda i: (i, 0)),
        out_shape=jax.ShapeDtypeStruct((T // dil, dil * N), BF16),
        scratch_shapes=[pltpu.VMEM((N // LANES, tm, LANES), F32)],
        compiler_params=_cparams(("parallel",)),
        name=f"inproj_d{dil}",
    )(xb, w)


def _na_kernel(q_ref, k_ref, v_ref, bias_ref, o_ref, s_ref, p_ref, *, rows, rb):
    nk = NA_ROWS * GRID_W
    npair = NA_WIDTH // LANES
    first = lax.broadcasted_iota(jnp.int32, (GRID_W, LANES), 1) < HEAD_DIM
    scale = HEAD_DIM ** -0.5 * LOG2E
    units = []
    for rr in range(rb):
        r = pl.program_id(1) * rb + rr
        rs = jnp.clip(r - NA_ROWS // 2, 0, rows - NA_ROWS)
        for p in range(npair):
            units.append((rr * npair + p, slice(rr * GRID_W, (rr + 1) * GRID_W), slice(p * LANES, (p + 1) * LANES),
                          r - rs, pl.multiple_of(rs * GRID_W, GRID_W), p))
    for u, qrows, cols, case, start, p in units:
        qp = q_ref[0, qrows, cols].astype(F32) * scale
        q2 = jnp.concatenate([jnp.where(first, qp, 0.0), jnp.where(first, 0.0, qp)], axis=0).astype(BF16)
        s_ref[u] = (_dot_nt(q2, k_ref[0, pl.ds(start, nk), cols])
                    + bias_ref[case, p * 2 * GRID_W:(p + 1) * 2 * GRID_W, :])
    for u, *_ in units:
        s = s_ref[u]
        e = jnp.exp2(s - jnp.max(s, axis=-1, keepdims=True))
        p_ref[u] = (e * (1.0 / jnp.sum(e, axis=-1, keepdims=True))).astype(BF16)
    for u, qrows, cols, case, start, p in units:
        o2 = jnp.dot(p_ref[u], v_ref[0, pl.ds(start, nk), cols], preferred_element_type=F32)
        o_ref[0, qrows, cols] = jnp.where(first, o2[:GRID_W], o2[GRID_W:]).astype(o_ref.dtype)


def _na_bias_table(rpb):
    qcol = np.arange(GRID_W)
    kcol = np.arange(GRID_W)
    cstart = np.clip(qcol - NA_COLS // 2, 0, GRID_W - NA_COLS)
    ok = (kcol[None, :] >= cstart[:, None]) & (kcol[None, :] < cstart[:, None] + NA_COLS)
    dc = np.clip(kcol[None, :] - qcol[:, None], -(NA_COLS - 1), NA_COLS - 1) + (NA_COLS - 1)
    b = jnp.where(ok[None, None], rpb[:, :, dc] * LOG2E, NEG_BIG)
    cases = []
    for c in range(NA_ROWS):
        dr = np.arange(NA_ROWS) - c + (NA_ROWS - 1)
        t = b[:, dr]
        cases.append(t.transpose(0, 2, 1, 3).reshape(NA_HEADS * GRID_W, NA_ROWS * GRID_W))
    return jnp.stack(cases, axis=0).astype(F32)


def _na_attention(h3, bias_tab, col0):
    B, S, _ = h3.shape
    rows = S // GRID_W
    cb = col0 // NA_WIDTH
    rb = NA_ROWS_PER_STEP
    return pl.pallas_call(
        functools.partial(_na_kernel, rows=rows, rb=rb),
        grid=(B, rows // rb),
        in_specs=[pl.BlockSpec((1, rb * GRID_W, NA_WIDTH), lambda b, r: (b, r, cb)),
                  pl.BlockSpec((1, S, NA_WIDTH), lambda b, r: (b, 0, cb + 1)),
                  pl.BlockSpec((1, S, NA_WIDTH), lambda b, r: (b, 0, cb + 2)),
                  pl.BlockSpec(bias_tab.shape, lambda b, r: (0, 0, 0))],
        out_specs=pl.BlockSpec((1, rb * GRID_W, NA_WIDTH), lambda b, r: (b, r, 0)),
        out_shape=jax.ShapeDtypeStruct((B, S, NA_WIDTH), BF16),
        scratch_shapes=[pltpu.VMEM((rb * NA_WIDTH // LANES, 2 * GRID_W, NA_ROWS * GRID_W), F32),
                        pltpu.VMEM((rb * NA_WIDTH // LANES, 2 * GRID_W, NA_ROWS * GRID_W), BF16)],
        compiler_params=_cparams(("parallel", "arbitrary")),
        name="na_attn",
    )(h3, h3, h3, bias_tab)


def _sgu_kernel(u_ref, v_ref, g_ref, b_ref, ws_ref, bs_ref, o_ref, *, nchunk):
    for c in range(nchunk):
        rs = slice(c * SGU_CHUNK, (c + 1) * SGU_CHUNK)
        v = _ln(_gelu(v_ref[0, rs, :].astype(F32)), g_ref[...], b_ref[...]).astype(BF16)
        u = _gelu(u_ref[0, rs, :].astype(F32))
        for g in range(SGU_GROUPS):
            cs = slice(g * SGU_GROUP_CH, (g + 1) * SGU_GROUP_CH)
            s = jnp.dot(ws_ref[g], v[:, cs], preferred_element_type=F32) + bs_ref[g]
            o_ref[0, rs, cs] = (u[:, cs] * s).astype(o_ref.dtype)


def _sgu(h3, ln_g, ln_b, ws, bs, col0, tc=512):
    B, S, _ = h3.shape
    tc = min(tc, S)
    cb = col0 // SGU_WIDTH
    bs_b = jnp.broadcast_to(bs[:, :, None], (SGU_GROUPS, SGU_CHUNK, SGU_GROUP_CH)).astype(F32)
    return pl.pallas_call(
        functools.partial(_sgu_kernel, nchunk=tc // SGU_CHUNK),
        grid=(B, S // tc),
        in_specs=[pl.BlockSpec((1, tc, SGU_WIDTH), lambda b, i: (b, i, cb)),
                  pl.BlockSpec((1, tc, SGU_WIDTH), lambda b, i: (b, i, cb + 1)),
                  pl.BlockSpec((1, SGU_WIDTH), lambda b, i: (0, 0)),
                  pl.BlockSpec((1, SGU_WIDTH), lambda b, i: (0, 0)),
                  pl.BlockSpec((SGU_GROUPS, SGU_CHUNK, SGU_CHUNK), lambda b, i: (0, 0, 0)),
                  pl.BlockSpec((SGU_GROUPS, SGU_CHUNK, SGU_GROUP_CH), lambda b, i: (0, 0, 0))],
        out_specs=pl.BlockSpec((1, tc, SGU_WIDTH), lambda b, i: (b, i, 0)),
        out_shape=jax.ShapeDtypeStruct((B, S, SGU_WIDTH), BF16),
        compiler_params=_cparams(("parallel", "parallel")),
        name="sgu",
    )(h3, h3, ln_g.reshape(1, -1), ln_b.reshape(1, -1), ws.astype(BF16), bs_b)


def _rope_tile(x, c, sa, sb):
    return x * c + pltpu.roll(x, LANES - HEAD_DIM // 2, 1) * sa + pltpu.roll(x, HEAD_DIM // 2, 1) * sb


def _dil_kernel(q_ref, k_ref, v_ref, c_ref, sa_ref, sb_ref, o_ref, lse_ref, kr_ref, s_ref, p_ref, l_ref,
                *, L, QB, KB, half, dil):
    qi = pl.program_id(2)
    ntile = DIL_GW // LANES
    rchunk = min(L, 256)

    @pl.when(qi == 0)
    def _():
        for c0 in range(0, L, rchunk):
            rs = slice(c0, c0 + rchunk)
            for t in range(ntile):
                cs = slice(t * LANES, (t + 1) * LANES)
                kr_ref[rs, cs] = _rope_tile(k_ref[0, rs, cs].astype(F32), c_ref[rs, cs], sa_ref[rs, cs],
                                            sb_ref[rs, cs]).astype(BF16)

    n0 = pl.multiple_of(qi * QB, QB)
    start = pl.multiple_of(jnp.clip(n0 - half, 0, L - KB), 16)
    qn = n0 + lax.broadcasted_iota(jnp.int32, (QB, KB), 0)
    kn = start + lax.broadcasted_iota(jnp.int32, (QB, KB), 1)
    ok = jnp.abs(kn - qn) <= half
    first = lax.broadcasted_iota(jnp.int32, (QB, LANES), 1) < HEAD_DIM
    scale = HEAD_DIM ** -0.5 * LOG2E
    if dil == 1:
        orows = slice(None)
    else:
        orows = pl.ds(n0 * dil + pl.program_id(1), QB, stride=dil)
    for t in range(ntile):
        cs = slice(t * LANES, (t + 1) * LANES)
        qr = _rope_tile(q_ref[0, :, cs].astype(F32), c_ref[pl.ds(n0, QB), cs], sa_ref[pl.ds(n0, QB), cs],
                        sb_ref[pl.ds(n0, QB), cs]) * scale
        kt = kr_ref[pl.ds(start, KB), cs]
        for sub in range(2):
            qm = jnp.where(first, qr, 0.0) if sub == 0 else jnp.where(first, 0.0, qr)
            s_ref[2 * t + sub] = jnp.where(ok, _dot_nt(qm.astype(BF16), kt), NEG_BIG)
    for u in range(2 * ntile):
        s = s_ref[u]
        m = jnp.max(s, axis=-1, keepdims=True)
        e = jnp.exp2(s - m)
        den = jnp.sum(e, axis=-1, keepdims=True)
        p_ref[u] = (e * (1.0 / den)).astype(BF16)
        l_ref[u] = jnp.broadcast_to(m * (1.0 / LOG2E) + jnp.log(den), (QB, LANES))
    for t in range(ntile):
        vt = v_ref[0, pl.ds(start, KB), t * LANES:(t + 1) * LANES]
        outs = [jnp.dot(p_ref[2 * t + sub], vt, preferred_element_type=F32) for sub in range(2)]
        o_ref[0, t, orows, :] = jnp.where(first, outs[0], outs[1])
        lse_ref[0, t, orows, :] = jnp.where(first, l_ref[2 * t], l_ref[2 * t + 1])


def _rope_tables(S):
    half = HEAD_DIM // 2
    inv = ROPE_THETA ** (-jnp.arange(half, dtype=F32) / half)
    ang = jnp.arange(S).astype(F32)[:, None] * inv[None, :]
    cos, sin = jnp.cos(ang), jnp.sin(ang)
    z = jnp.zeros_like(sin)
    c = jnp.tile(jnp.concatenate([cos, cos], axis=-1), (1, DIL_HEADS))
    sa = jnp.tile(jnp.concatenate([-sin, z], axis=-1), (1, DIL_HEADS))
    sb = jnp.tile(jnp.concatenate([z, sin], axis=-1), (1, DIL_HEADS))
    return c, sa, sb


def _dil_attention(hsrc, tabs, g, B, S, col0=0):
    window, dil = DIL_PATTERNS[g]
    L = S // dil
    half = window // (2 * dil)
    QB = min(L, DIL_QUERY_BLOCK)
    KB = min(L, QB + 2 * half)
    ntile = DIL_GW // LANES
    if dil == 1:
        hv, cpb, qb = hsrc, hsrc.shape[-1] // DIL_GW, col0 // DIL_GW
        out_spec = pl.BlockSpec((1, ntile, QB, LANES), lambda b, r, i: (b, 0, i, 0))
    else:
        hv, cpb, qb = hsrc.reshape(B, L, dil * 3 * DIL_GW), 3, 0
        out_spec = pl.BlockSpec((1, ntile, S, LANES), lambda b, r, i: (b, 0, 0, 0))
    c, sa, sb = [t.reshape(L, dil * DIL_GW) for t in tabs]
    tab_spec = pl.BlockSpec((L, DIL_GW), lambda b, r, i: (0, r))
    o, lse = pl.pallas_call(
        functools.partial(_dil_kernel, L=L, QB=QB, KB=KB, half=half, dil=dil),
        grid=(B, dil, L // QB),
        in_specs=[pl.BlockSpec((1, QB, DIL_GW), lambda b, r, i: (b, i, r * cpb + qb)),
                  pl.BlockSpec((1, L, DIL_GW), lambda b, r, i: (b, 0, r * cpb + qb + 1)),
                  pl.BlockSpec((1, L, DIL_GW), lambda b, r, i: (b, 0, r * cpb + qb + 2)),
                  tab_spec, tab_spec, tab_spec],
        out_specs=[out_spec, out_spec],
        out_shape=[jax.ShapeDtypeStruct((B, ntile, S, LANES), F32)] * 2,
        scratch_shapes=[pltpu.VMEM((L, DIL_GW), BF16), pltpu.VMEM((DIL_HEADS, QB, KB), F32),
                        pltpu.VMEM((DIL_HEADS, QB, KB), BF16), pltpu.VMEM((DIL_HEADS, QB, LANES), F32)],
        compiler_params=_cparams(("parallel", "arbitrary", "arbitrary")),
        name=f"dil_attn_g{g}",
    )(hv, hv, hv, c, sa, sb)
    return o, lse


def _merge_kernel(x_ref, gate_ref, a_ref, b_ref, o0_ref, o1_ref, o2_ref, l0_ref, l1_ref, l2_ref, p_ref,
                  wa_ref, wb_ref, wc_ref, wo_ref, g1_ref, b1_ref, wpg_ref, wp_ref,
                  xb_ref, base_ref, *, alpha, D):
    a = jnp.dot(a_ref[...], wa_ref[...], preferred_element_type=F32)
    b = jnp.dot(b_ref[...], wb_ref[...], preferred_element_type=F32)
    halves = []
    for t in range(DIL_GW // LANES):
        l0, l1, l2 = l0_ref[0, t], l1_ref[0, t], l2_ref[0, t]
        m = jnp.maximum(jnp.maximum(l0, l1), l2)
        e0, e1, e2 = jnp.exp(l0 - m), jnp.exp(l1 - m), jnp.exp(l2 - m)
        halves.append((e0 * o0_ref[0, t] + e1 * o1_ref[0, t] + e2 * o2_ref[0, t]) * (1.0 / (e0 + e1 + e2)))
    cin = jnp.concatenate(halves, axis=-1)
    c = jnp.dot(cin.astype(BF16), wc_ref[...], preferred_element_type=F32)
    merged = (_sigmoid(gate_ref[:, 0:D].astype(F32)) * a
              + _sigmoid(gate_ref[:, D:2 * D].astype(F32)) * b
              + _sigmoid(gate_ref[:, 2 * D:3 * D].astype(F32)) * c)
    mix = jnp.dot(merged.astype(BF16), wo_ref[...], preferred_element_type=F32)
    x1 = _ln(alpha * x_ref[...] + mix, g1_ref[...], b1_ref[...])
    x1b = x1.astype(BF16)
    xb_ref[...] = x1b
    ple = (_sigmoid(jnp.dot(x1b, wpg_ref[...], preferred_element_type=F32))
           * jnp.dot(p_ref[...].astype(BF16), wp_ref[...], preferred_element_type=F32))
    base_ref[...] = alpha * x1 + ple


def _merge(x, h, na, sgu, dil_o, dil_l, p, wa, wb, wc, wo, g1, b1, wpg, wp, alpha, S, tm=512):
    T, D = x.shape
    tm = min(tm, S)
    per_seq = S // tm
    row = lambda w: pl.BlockSpec((tm, w), lambda i: (i, 0))
    full = lambda arr: pl.BlockSpec(arr.shape, lambda i: (0,) * arr.ndim)
    dil = pl.BlockSpec((1, DIL_GW // LANES, tm, LANES), lambda i: (i // per_seq, 0, i % per_seq, 0))
    g1, b1 = g1.reshape(1, D), b1.reshape(1, D)
    return pl.pallas_call(
        functools.partial(_merge_kernel, alpha=alpha, D=D),
        grid=(T // tm,),
        in_specs=[row(D), row(N_BRANCH * D), row(NA_WIDTH), row(SGU_WIDTH),
                  dil, dil, dil, dil, dil, dil,
                  row(p.shape[1]),
                  full(wa), full(wb), full(wc), full(wo), full(g1), full(b1), full(wpg), full(wp)],
        out_specs=[row(D), row(D)],
        out_shape=[jax.ShapeDtypeStruct((T, D), BF16), jax.ShapeDtypeStruct((T, D), F32)],
        compiler_params=_cparams(("parallel",)),
        name="merge",
    )(x, h, na, sgu, dil_o[0], dil_o[1], dil_o[2], dil_l[0], dil_l[1], dil_l[2], p,
      wa, wb, wc, wo, g1, b1, wpg, wp)


_CAND_ROWS = 16 + 7 * 8 + 8


def _cand_index():
    idx = [float(j) for j in range(16)]
    for i in range(1, 8):
        idx += [float(i * 16 + j) for j in range(8)]
    idx += [float(i * 16) for i in range(8, 16)]
    return np.broadcast_to(np.asarray(idx, np.float32)[:, None], (_CAND_ROWS, ROUTE_LW)).copy()


def _extract_top(s, order, n):
    rank = jnp.full(s.shape, 127.0, F32)
    vals = []
    for i in range(n):
        m = jnp.max(s, axis=0, keepdims=True)
        pick = jnp.min(jnp.where(s == m, order, 1e9), axis=0, keepdims=True)
        sel = order == pick
        rank = jnp.where(sel, float(i), rank)
        s = jnp.where(sel, -jnp.inf, s)
        vals.append(m)
    return rank, vals


def _batcher_pairs(n):
    pairs = []

    def merge(lo, hi, r):
        step = r * 2
        if step < hi - lo:
            merge(lo, hi, step)
            merge(lo + r, hi, step)
            pairs.extend((i, i + r) for i in range(lo + r, hi - r, step))
        else:
            pairs.append((lo, lo + r))

    def sort(lo, hi):
        if hi - lo >= 1:
            mid = lo + (hi - lo) // 2
            sort(lo, mid)
            sort(mid + 1, hi)
            merge(lo, hi, 1)

    sort(0, n - 1)
    return pairs


def _sort_desc(xs):
    xs = list(xs)
    for i, j in _batcher_pairs(PEER_TOPK):
        if j < len(xs):
            xs[i], xs[j] = jnp.maximum(xs[i], xs[j]), jnp.minimum(xs[i], xs[j])
    return xs


def _merge_sublanes(xs):
    xs = list(xs)
    n = PEER_TOPK
    for shift in (4, 2, 1):
        t = [jnp.maximum(xs[i], pltpu.roll(xs[n - 1 - i], shift, 0)) for i in range(n)]
        for d in (8, 4, 2, 1):
            for i in range(n):
                if i & d == 0:
                    t[i], t[i + d] = jnp.maximum(t[i], t[i + d]), jnp.minimum(t[i], t[i + d])
        xs = t
    return xs


def _search_bits(x, v):
    c3 = x >= v[7]
    c2 = x >= jnp.where(c3, v[3], v[11])
    c1 = x >= jnp.where(c3, jnp.where(c2, v[1], v[5]), jnp.where(c2, v[9], v[13]))
    c0 = x >= jnp.where(c3,
                        jnp.where(c2, jnp.where(c1, v[0], v[2]), jnp.where(c1, v[4], v[6])),
                        jnp.where(c2, jnp.where(c1, v[8], v[10]), jnp.where(c1, v[12], v[14])))
    return c3, c2, c1, c0


def _pick_by_rank(bits, vals):
    c3, c2, c1, c0 = bits
    l0 = [jnp.where(c0, vals[2 * k], vals[2 * k + 1]) for k in range(8)]
    l1 = [jnp.where(c1, l0[2 * k], l0[2 * k + 1]) for k in range(4)]
    l2 = [jnp.where(c2, l1[2 * k], l1[2 * k + 1]) for k in range(2)]
    return jnp.where(c3, l2[0], l2[1])


def _route_sorted(s1, s2):
    W = s1.shape[1]
    sub_rows = 8
    g1 = [s1[sub_rows * g:sub_rows * (g + 1)] for g in range(PEER_NKEYS // sub_rows)]
    g2 = [s2[sub_rows * g:sub_rows * (g + 1)] for g in range(PEER_NKEYS // sub_rows)]
    v1 = _merge_sublanes(_sort_desc(g1))
    v2 = _merge_sublanes(_sort_desc(g2))
    sub = lax.broadcasted_iota(jnp.int32, (sub_rows, W), 0)

    def pack(vs):
        out = vs[0]
        for s in range(1, sub_rows):
            out = jnp.where(sub == s, vs[s], out)
        return out

    e1r = [jnp.exp(v - v1[0]) for v in v1]
    e2r = [jnp.exp(v - v2[0]) for v in v2]
    v2lo, v2hi, v1hi = pack(v2[:8]), pack(v2[8:]), pack(v1[8:])
    e2lo, e2hi, e1hi = pack(e2r[:8]), pack(e2r[8:]), pack(e1r[8:])
    cand = [v1[0] + v2lo, v1[0] + v2hi] + [v1[i] + v2lo for i in range(1, 8)] + [v1hi + v2[0]]
    ecand = [e1r[0] * e2lo, e1r[0] * e2hi] + [e1r[i] * e2lo for i in range(1, 8)] + [e1hi * e2r[0]]
    neg = jnp.full((sub_rows, W), -jnp.inf, F32)
    tau = _merge_sublanes(_sort_desc(cand) + [neg] * (PEER_TOPK - len(cand)))[PEER_TOPK - 1]
    sel = [jnp.where(c >= tau, 1.0, 0.0) for c in cand]
    colsum = lambda x: jnp.sum(x, axis=0, keepdims=True)
    z = colsum(sum(s * e for s, e in zip(sel, ecand)))
    cnt = [colsum(sel[0] + sel[1])] + [colsum(sel[1 + i]) for i in range(1, 8)] + [sel[9][i:i + 1] for i in range(8)]
    half_minus_cnt = [0.5 - c for c in cnt]
    inv_z = 1.0 / z
    a1, nx1, e2, x2 = [], [], [], []
    n1 = jnp.zeros((sub_rows, W), F32)
    n2 = jnp.zeros((sub_rows, W), F32)
    for x in g1:
        inside = x >= v1[PEER_TOPK - 1]
        nx1.append(jnp.where(inside, _pick_by_rank(_search_bits(x, v1), half_minus_cnt), 1e9))
        a1.append(jnp.where(inside, jnp.exp(x - v1[0]), 0.0) * inv_z)
        n1 = n1 + jnp.where(inside, 1.0, 0.0)
    for x in g2:
        inside = x >= v2[PEER_TOPK - 1]
        c3, c2, c1, c0 = _search_bits(x, v2)
        rank = (jnp.where(c3, 0.0, 8.0) + jnp.where(c2, 0.0, 4.0)) + (jnp.where(c1, 0.0, 2.0) + jnp.where(c0, 0.0, 1.0))
        x2.append(jnp.where(inside, -rank, -127.0))
        e2.append(jnp.exp(x - v2[0]))
        n2 = n2 + jnp.where(inside, 1.0, 0.0)
    dup = jnp.zeros((sub_rows, W), F32)
    for i in range(PEER_TOPK - 1):
        dup = dup + jnp.where(v1[i] == v1[i + 1], 1.0, 0.0) + jnp.where(v2[i] == v2[i + 1], 1.0, 0.0)
    flag = (jnp.abs(colsum(n1) - PEER_TOPK) + jnp.abs(colsum(n2) - PEER_TOPK)
            + jnp.abs(colsum(sum(sel)) - PEER_TOPK) + colsum(dup))
    cat = lambda xs: jnp.concatenate(xs, axis=0)
    return cat(a1), cat(nx1), cat(e2), cat(x2), flag


def _row_tile(ref, h, a, ls):
    row = ref[h, a:a + 1, ls]
    return jnp.broadcast_to(row, (BF16_ROWS, row.shape[1])).astype(BF16)


def _route_chunk(s1, s2, cidx):
    key_order = lax.broadcasted_iota(jnp.int32, s1.shape, 0).astype(F32)
    rank1, v1 = _extract_top(s1, key_order, PEER_TOPK)
    rank2, v2 = _extract_top(s2, key_order, PEER_TOPK)
    v1a = jnp.concatenate(v1, axis=0)
    v2a = jnp.concatenate(v2, axis=0)
    e1r = jnp.exp(v1a - v1[0])
    e2r = jnp.exp(v2a - v2[0])

    def pairs(r1, r2, op):
        blocks = [op(r1[0:1], r2)]
        blocks += [op(r1[i:i + 1], r2[0:8]) for i in range(1, 8)]
        blocks += [op(r1[8:16], r2[0:1])]
        return jnp.concatenate(blocks, axis=0)

    cand = pairs(v1a, v2a, jnp.add)
    ecand = pairs(e1r, e2r, jnp.multiply)
    crank, _ = _extract_top(cand, cidx, PEER_TOPK)
    sel = jnp.where(crank < 100.0, 1.0, 0.0)
    z = jnp.sum(sel * ecand, axis=0, keepdims=True)
    cnt = [jnp.sum(sel[0:16], axis=0, keepdims=True)]
    cnt += [jnp.sum(sel[16 + 8 * (i - 1):16 + 8 * i], axis=0, keepdims=True) for i in range(1, 8)]
    cnt += [sel[72 + i:73 + i] for i in range(8)]
    nx1 = jnp.full(s1.shape, 1e9, F32)
    for i in range(PEER_TOPK):
        nx1 = jnp.where(rank1 == float(i), 0.5 - cnt[i], nx1)
    a1 = jnp.where(rank1 < 100.0, jnp.exp(s1 - v1[0]), 0.0) * (1.0 / z)
    e2 = jnp.exp(s2 - v2[0])
    return a1, nx1, e2, -rank2


def _route_kernel(x_ref, wq_ref, k1_ref, k2_ref, cidx_ref, a1_ref, nx1_ref, e2_ref, x2_ref, q_ref, s1_ref, s2_ref,
                  *, tb):
    hk = PEER_DKEY // 2
    h = pl.program_id(1)

    @pl.when(h == 0)
    def _():
        for hh in range(PEER_HEADS):
            q_ref[hh] = jnp.dot(x_ref[...], wq_ref[:, hh * PEER_DKEY:(hh + 1) * PEER_DKEY],
                                preferred_element_type=F32).astype(BF16)

    s1_ref[...] = _dot_nt(k1_ref[0], q_ref[h, :, :hk])
    s2_ref[...] = _dot_nt(k2_ref[0], q_ref[h, :, hk:])

    def chunk(c, carry):
        ls = pl.ds(pl.multiple_of(c * ROUTE_LW, ROUTE_LW), ROUTE_LW)

        def emit(a1, nx1, e2, x2):
            a1_ref[0, :, ls] = a1
            nx1_ref[0, :, ls] = nx1
            e2_ref[0, :, ls] = e2.astype(e2_ref.dtype)
            x2_ref[0, :, ls] = x2.astype(x2_ref.dtype)

        *fast, flag = _route_sorted(s1_ref[:, ls], s2_ref[:, ls])
        emit(*fast)

        @pl.when(jnp.max(flag) > 0.0)
        def _():
            emit(*_route_chunk(s1_ref[:, ls], s2_ref[:, ls], cidx_ref[...]))

        return carry

    lax.fori_loop(0, tb // ROUTE_LW, chunk, 0)


def _peer_route(xb, wq, k1, k2, tb):
    T, D = xb.shape
    H = PEER_HEADS
    cidx = jnp.asarray(_cand_index())
    out_spec = pl.BlockSpec((1, PEER_NKEYS, tb), lambda i, h: (h, 0, i))
    out_sds = lambda dt: jax.ShapeDtypeStruct((H, PEER_NKEYS, T), dt)
    return pl.pallas_call(
        functools.partial(_route_kernel, tb=tb),
        grid=(T // tb, H),
        in_specs=[pl.BlockSpec((tb, D), lambda i, h: (i, 0)),
                  pl.BlockSpec((D, H * PEER_DKEY), lambda i, h: (0, 0)),
                  pl.BlockSpec((1, PEER_NKEYS, PEER_DKEY // 2), lambda i, h: (h, 0, 0)),
                  pl.BlockSpec((1, PEER_NKEYS, PEER_DKEY // 2), lambda i, h: (h, 0, 0)),
                  pl.BlockSpec((_CAND_ROWS, ROUTE_LW), lambda i, h: (0, 0))],
        out_specs=[out_spec] * 4,
        out_shape=[out_sds(F32), out_sds(F32), out_sds(BF16), out_sds(BF16)],
        scratch_shapes=[pltpu.VMEM((H, tb, PEER_DKEY), BF16),
                        pltpu.VMEM((PEER_NKEYS, tb), F32), pltpu.VMEM((PEER_NKEYS, tb), F32)],
        compiler_params=_cparams(("parallel", "arbitrary")),
        name="peer_route",
    )(xb, wq, k1, k2, cidx)


def _dense_kernel(x_ref, u_ref, vt_ref, a1_ref, nx1_ref, e2_ref, x2_ref, base_ref, g_ref, b_ref,
                  yf_ref, yb_ref, acc_ref, act_ref, pw_ref, *, eb):
    e = pl.program_id(1)
    last = pl.num_programs(1) - 1
    tb = x_ref.shape[0]
    zero = jnp.zeros((), BF16)

    @pl.when(e == 0)
    def _():
        acc_ref[...] = jnp.zeros_like(acc_ref)

    for sb in range(eb // DENSE_SUB):
        rows_e = slice(sb * DENSE_SUB, (sb + 1) * DENSE_SUB)
        act_ref[rows_e, :] = _gelu(_dot_nt(u_ref[rows_e, :], x_ref[...]).astype(BF16))
    nk = PEER_NKEYS // BF16_ROWS

    for lt in range(tb // DENSE_LW):
        ls = slice(lt * DENSE_LW, (lt + 1) * DENSE_LW)
        for al in range(eb // PEER_NKEYS):
            w = [None] * nk
            for h in range(PEER_HEADS):
                nxb = _row_tile(nx1_ref, h, al, ls)
                a1b = _row_tile(a1_ref, h, al, ls)
                for k in range(nk):
                    rows = slice(k * BF16_ROWS, (k + 1) * BF16_ROWS)
                    term = jnp.where(x2_ref[h, rows, ls] >= nxb, e2_ref[h, rows, ls], zero) * a1b
                    w[k] = term if w[k] is None else w[k] + term
            for k in range(nk):
                r0 = al * PEER_NKEYS + k * BF16_ROWS
                pw_ref[r0:r0 + BF16_ROWS, ls] = w[k] * act_ref[r0:r0 + BF16_ROWS, ls]
        acc_ref[:, ls] += jnp.dot(vt_ref[...], pw_ref[:, ls], preferred_element_type=F32)

    @pl.when(e == last)
    def _():
        y = _ln(base_ref[...] + acc_ref[...].T, g_ref[...], b_ref[...])
        yf_ref[...] = y
        yb_ref[...] = y.astype(BF16)


def _peer_dense(xb, u, vt, route, base, g2, b2, tb, eb=2048):
    T, D = xb.shape
    NE = u.shape[0]
    H = PEER_HEADS
    rspec = pl.BlockSpec((H, PEER_NKEYS, tb), lambda i, e: (0, 0, i))
    aspec = pl.BlockSpec((H, eb // PEER_NKEYS, tb), lambda i, e: (0, e, i))
    return pl.pallas_call(
        functools.partial(_dense_kernel, eb=eb),
        grid=(T // tb, NE // eb),
        in_specs=[pl.BlockSpec((tb, D), lambda i, e: (i, 0)),
                  pl.BlockSpec((eb, D), lambda i, e: (e, 0)),
                  pl.BlockSpec((D, eb), lambda i, e: (0, e)),
                  aspec, aspec, rspec, rspec,
                  pl.BlockSpec((tb, D), lambda i, e: (i, 0)),
                  pl.BlockSpec((1, D), lambda i, e: (0, 0)),
                  pl.BlockSpec((1, D), lambda i, e: (0, 0))],
        out_specs=[pl.BlockSpec((tb, D), lambda i, e: (i, 0)),
                   pl.BlockSpec((tb, D), lambda i, e: (i, 0))],
        out_shape=[jax.ShapeDtypeStruct((T, D), F32), jax.ShapeDtypeStruct((T, D), BF16)],
        scratch_shapes=[pltpu.VMEM((D, tb), F32), pltpu.VMEM((eb, tb), BF16), pltpu.VMEM((eb, tb), BF16)],
        compiler_params=_cparams(("parallel", "arbitrary")),
        name="peer_dense",
    )(xb, u, vt, route[0], route[1], route[2], route[3], base, g2.reshape(1, D), b2.reshape(1, D))


def _layer_weights(i, w_in, rpb, sgu_ln_g, sgu_ln_b, sgu_w, sgu_b, w_br_a, w_br_b, w_br_c, w_out, ln1_g, ln1_b,
                   peer_wq, peer_k1, peer_k2, peer_u, peer_v, ple_w, ple_gate_w, ln2_g, ln2_b):
    D = w_in.shape[1]
    off_c = 3 * NA_WIDTH + 2 * SGU_WIDTH + 3 * DIL_WIDTH
    off_b = 3 * NA_WIDTH + 2 * SGU_WIDTH

    def group_w(g):
        return jnp.concatenate(
            [w_in[i][:, off_b + part * DIL_WIDTH + g * DIL_GW: off_b + part * DIL_WIDTH + (g + 1) * DIL_GW]
             for part in range(3)], axis=1).astype(BF16)

    w_perm = jnp.concatenate([w_in[i][:, off_c:].astype(BF16), w_in[i][:, :off_b].astype(BF16), group_w(0)], axis=1)
    return dict(
        w_in=w_perm, w_dil={g: group_w(g) for g in range(1, len(DIL_PATTERNS))},
        col_a=N_BRANCH * D, col_b=N_BRANCH * D + 3 * NA_WIDTH,
        col_c=N_BRANCH * D + 3 * NA_WIDTH + 2 * SGU_WIDTH,
        na_bias=_na_bias_table(rpb[i]),
        sgu_ln_g=sgu_ln_g[i], sgu_ln_b=sgu_ln_b[i], sgu_w=sgu_w[i], sgu_b=sgu_b[i],
        wa=w_br_a[i].astype(BF16), wb=w_br_b[i].astype(BF16), wc=w_br_c[i].astype(BF16),
        wo=w_out[i].astype(BF16), ln1_g=ln1_g[i], ln1_b=ln1_b[i],
        wq=peer_wq[i].astype(BF16), k1=peer_k1[i].astype(BF16), k2=peer_k2[i].astype(BF16),
        u=peer_u[i].astype(BF16), vt=peer_v[i].astype(BF16).T,
        wp=ple_w[i].astype(BF16), wpg=ple_gate_w[i].astype(BF16), ln2_g=ln2_g[i], ln2_b=ln2_b[i])


def _trunk(x, p, ln0_g, ln0_b, layers, alpha):
    B, S, D = x.shape
    T = B * S
    tb = min(512, T)
    tabs = _rope_tables(S)
    xf, xb = _ln0(x.reshape(T, D), ln0_g, ln0_b)
    for i, lw in enumerate(layers):
        h = _inproj(xb, lw["w_in"])
        h3 = h.reshape(B, S, -1)
        na = _na_attention(h3, lw["na_bias"], lw["col_a"]).reshape(T, NA_WIDTH)
        sg = _sgu(h3, lw["sgu_ln_g"], lw["sgu_ln_b"], lw["sgu_w"], lw["sgu_b"], lw["col_b"]).reshape(T, SGU_WIDTH)
        dil = [_dil_attention(h3, tabs, 0, B, S, lw["col_c"])]
        for g in range(1, len(DIL_PATTERNS)):
            hd = _inproj_residue_major(xb, lw["w_dil"][g], DIL_PATTERNS[g][1])
            dil.append(_dil_attention(hd, tabs, g, B, S))
        x1b, base = _merge(xf, h, na, sg, [d[0] for d in dil], [d[1] for d in dil], p[i].reshape(T, -1),
                           lw["wa"], lw["wb"], lw["wc"], lw["wo"], lw["ln1_g"], lw["ln1_b"],
                           lw["wpg"], lw["wp"], alpha, S)
        route = _peer_route(x1b, lw["wq"], lw["k1"], lw["k2"], min(ROUTE_TB, T))
        xf, xb = _peer_dense(x1b, lw["u"], lw["vt"], route, base, lw["ln2_g"], lw["ln2_b"], tb)
    return xf.reshape(B, S, D)


def kernel(x_prompt, x_sample, p_prompt, p_sample, ln0_g, ln0_b, w_in, rpb, sgu_ln_g, sgu_ln_b, sgu_w, sgu_b, w_br_a, w_br_b, w_br_c, w_out, ln1_g, ln1_b, peer_wq, peer_k1, peer_k2, peer_u, peer_v, ple_w, ple_gate_w, ln2_g, ln2_b):
    depth = w_in.shape[0]
    alpha = (2 * depth) ** 0.25
    layers = [_layer_weights(i, w_in, rpb, sgu_ln_g, sgu_ln_b, sgu_w, sgu_b, w_br_a, w_br_b, w_br_c, w_out,
                             ln1_g, ln1_b, peer_wq, peer_k1, peer_k2, peer_u, peer_v, ple_w, ple_gate_w,
                             ln2_g, ln2_b) for i in range(depth)]
    y_prompt = _trunk(x_prompt, p_prompt, ln0_g, ln0_b, layers, alpha)
    y_sample = _trunk(x_sample, p_sample, ln0_g, ln0_b, layers, alpha)
    return (y_prompt, y_sample)
```

```python
import functools

import numpy as np
import jax
import jax.numpy as jnp
from jax import lax
from jax.experimental import pallas as pl
from jax.experimental.pallas import tpu as pltpu

F32 = jnp.float32
BF16 = jnp.bfloat16

GRID_W = 64
HEAD_DIM = 64
NA_HEADS = 8
NA_WIDTH = NA_HEADS * HEAD_DIM
NA_ROWS = 8
NA_COLS = 16
SGU_GROUPS = 6
SGU_GROUP_CH = 128
SGU_WIDTH = SGU_GROUPS * SGU_GROUP_CH
SGU_CHUNK = 128
DIL_PATTERNS = ((128, 1), (512, 4), (2048, 16))
DIL_HEADS = 4
DIL_GW = DIL_HEADS * HEAD_DIM
DIL_WIDTH = len(DIL_PATTERNS) * DIL_GW
ROPE_THETA = 10000.0
N_BRANCH = 3
PEER_HEADS = 8
PEER_NKEYS = 128
PEER_DKEY = 256
PEER_TOPK = 16
LN_EPS = 1e-5
NEG_BIG = -1e30
LOG2E = 1.4426950408889634

LANES = 128
BF16_ROWS = 16
DENSE_SUB = 512
DENSE_LW = 256
DIL_QUERY_BLOCK = 256
NA_ROWS_PER_STEP = 4
ROUTE_TB = 1024
ROUTE_LW = 512
VMEM_LIMIT = 48 * 1024 * 1024


def _cparams(sem):
    return pltpu.CompilerParams(dimension_semantics=sem, vmem_limit_bytes=VMEM_LIMIT)


def _ln(x, g, b):
    mu = jnp.mean(x, axis=-1, keepdims=True)
    xc = x - mu
    var = jnp.mean(xc * xc, axis=-1, keepdims=True)
    return xc * lax.rsqrt(var + LN_EPS) * g + b


def _gelu(x):
    return x * (0.5 * (1.0 + jnp.tanh(0.7978845608028654 * (x + 0.044715 * (x * x * x)))))


def _sigmoid(x):
    return 1.0 / (1.0 + jnp.exp(-x))


def _dot_nt(a, b):
    return lax.dot_general(a, b, (((1,), (1,)), ((), ())), preferred_element_type=F32)


def _ln0_kernel(x_ref, g_ref, b_ref, xf_ref, xb_ref):
    y = _ln(x_ref[...], g_ref[...], b_ref[...])
    xf_ref[...] = y
    xb_ref[...] = y.astype(BF16)


def _ln0(x, g, b, tm=512):
    T, D = x.shape
    return pl.pallas_call(
        _ln0_kernel,
        grid=(T // tm,),
        in_specs=[pl.BlockSpec((tm, D), lambda i: (i, 0)),
                  pl.BlockSpec((1, D), lambda i: (0, 0)),
                  pl.BlockSpec((1, D), lambda i: (0, 0))],
        out_specs=[pl.BlockSpec((tm, D), lambda i: (i, 0)),
                   pl.BlockSpec((tm, D), lambda i: (i, 0))],
        out_shape=[jax.ShapeDtypeStruct((T, D), F32), jax.ShapeDtypeStruct((T, D), BF16)],
        compiler_params=_cparams(("parallel",)),
        name="ln0",
    )(x, g.reshape(1, D), b.reshape(1, D))


def _matmul_kernel(x_ref, w_ref, o_ref):
    o_ref[...] = jnp.dot(x_ref[...], w_ref[...], preferred_element_type=F32).astype(o_ref.dtype)


def _inproj(xb, w, tm=1024, tn=2304):
    T, K = xb.shape
    N = w.shape[1]
    tm = min(tm, T)
    return pl.pallas_call(
        _matmul_kernel,
        grid=(T // tm, N // tn),
        in_specs=[pl.BlockSpec((tm, K), lambda i, j: (i, 0)),
                  pl.BlockSpec((K, tn), lambda i, j: (0, j))],
        out_specs=pl.BlockSpec((tm, tn), lambda i, j: (i, j)),
        out_shape=jax.ShapeDtypeStruct((T, N), BF16),
        compiler_params=_cparams(("parallel", "parallel")),
        name="inproj",
    )(xb, w)


def _matmul_residue_major_kernel(x_ref, w_ref, o_ref, acc_ref, *, dil):
    acc = jnp.dot(x_ref[...], w_ref[...], preferred_element_type=F32)
    ntile, tm, _ = acc_ref.shape
    for j in range(ntile):
        acc_ref[j] = acc[:, j * LANES:(j + 1) * LANES]
    for r in range(dil):
        for j in range(ntile):
            c0 = (r * ntile + j) * LANES
            o_ref[:, c0:c0 + LANES] = acc_ref[j, pl.ds(r, tm // dil, stride=dil), :].astype(o_ref.dtype)


def _inproj_residue_major(xb, w, dil, tm=1024):
    T, K = xb.shape
    N = w.shape[1]
    tm = min(tm, T)
    return pl.pallas_call(
        functools.partial(_matmul_residue_major_kernel, dil=dil),
        grid=(T // tm,),
        in_specs=[pl.BlockSpec((tm, K), lambda i: (i, 0)),
                  pl.BlockSpec((K, N), lambda i: (0, 0))],
        out_specs=pl.BlockSpec((tm // dil, dil * N), lambda i: (i, 0)),
        out_shape=jax.ShapeDtypeStruct((T // dil, dil * N), BF16),
        scratch_shapes=[pltpu.VMEM((N // LANES, tm, LANES), F32)],
        compiler_params=_cparams(("parallel",)),
        name=f"inproj_d{dil}",
    )(xb, w)


def _na_kernel(q_ref, k_ref, v_ref, bias_ref, o_ref, s_ref, p_ref, *, rows, rb):
    nk = NA_ROWS * GRID_W
    npair = NA_WIDTH // LANES
    first = lax.broadcasted_iota(jnp.int32, (GRID_W, LANES), 1) < HEAD_DIM
    scale = HEAD_DIM ** -0.5 * LOG2E
    units = []
    for rr in range(rb):
        r = pl.program_id(1) * rb + rr
        rs = jnp.clip(r - NA_ROWS // 2, 0, rows - NA_ROWS)
        for p in range(npair):
            units.append((rr * npair + p, slice(rr * GRID_W, (rr + 1) * GRID_W), slice(p * LANES, (p + 1) * LANES),
                          r - rs, pl.multiple_of(rs * GRID_W, GRID_W), p))
    for u, qrows, cols, case, start, p in units:
        qp = q_ref[0, qrows, cols].astype(F32) * scale
        q2 = jnp.concatenate([jnp.where(first, qp, 0.0), jnp.where(first, 0.0, qp)], axis=0).astype(BF16)
        s_ref[u] = (_dot_nt(q2, k_ref[0, pl.ds(start, nk), cols])
                    + bias_ref[case, p * 2 * GRID_W:(p + 1) * 2 * GRID_W, :])
    for u, *_ in units:
        s = s_ref[u]
        e = jnp.exp2(s - jnp.max(s, axis=-1, keepdims=True))
        p_ref[u] = (e * (1.0 / jnp.sum(e, axis=-1, keepdims=True))).astype(BF16)
    for u, qrows, cols, case, start, p in units:
        o2 = jnp.dot(p_ref[u], v_ref[0, pl.ds(start, nk), cols], preferred_element_type=F32)
        o_ref[0, qrows, cols] = jnp.where(first, o2[:GRID_W], o2[GRID_W:]).astype(o_ref.dtype)


def _na_bias_table(rpb):
    qcol = np.arange(GRID_W)
    kcol = np.arange(GRID_W)
    cstart = np.clip(qcol - NA_COLS // 2, 0, GRID_W - NA_COLS)
    ok = (kcol[None, :] >= cstart[:, None]) & (kcol[None, :] < cstart[:, None] + NA_COLS)
    dc = np.clip(kcol[None, :] - qcol[:, None], -(NA_COLS - 1), NA_COLS - 1) + (NA_COLS - 1)
    b = jnp.where(ok[None, None], rpb[:, :, dc] * LOG2E, NEG_BIG)
    cases = []
    for c in range(NA_ROWS):
        dr = np.arange(NA_ROWS) - c + (NA_ROWS - 1)
        t = b[:, dr]
        cases.append(t.transpose(0, 2, 1, 3).reshape(NA_HEADS * GRID_W, NA_ROWS * GRID_W))
    return jnp.stack(cases, axis=0).astype(F32)


def _na_attention(h3, bias_tab, col0):
    B, S, _ = h3.shape
    rows = S // GRID_W
    cb = col0 // NA_WIDTH
    rb = NA_ROWS_PER_STEP
    return pl.pallas_call(
        functools.partial(_na_kernel, rows=rows, rb=rb),
        grid=(B, rows // rb),
        in_specs=[pl.BlockSpec((1, rb * GRID_W, NA_WIDTH), lambda b, r: (b, r, cb)),
                  pl.BlockSpec((1, S, NA_WIDTH), lambda b, r: (b, 0, cb + 1)),
                  pl.BlockSpec((1, S, NA_WIDTH), lambda b, r: (b, 0, cb + 2)),
                  pl.BlockSpec(bias_tab.shape, lambda b, r: (0, 0, 0))],
        out_specs=pl.BlockSpec((1, rb * GRID_W, NA_WIDTH), lambda b, r: (b, r, 0)),
        out_shape=jax.ShapeDtypeStruct((B, S, NA_WIDTH), BF16),
        scratch_shapes=[pltpu.VMEM((rb * NA_WIDTH // LANES, 2 * GRID_W, NA_ROWS * GRID_W), F32),
                        pltpu.VMEM((rb * NA_WIDTH // LANES, 2 * GRID_W, NA_ROWS * GRID_W), BF16)],
        compiler_params=_cparams(("parallel", "arbitrary")),
        name="na_attn",
    )(h3, h3, h3, bias_tab)


def _sgu_kernel(u_ref, v_ref, g_ref, b_ref, ws_ref, bs_ref, o_ref, *, nchunk):
    for c in range(nchunk):
        rs = slice(c * SGU_CHUNK, (c + 1) * SGU_CHUNK)
        v = _ln(_gelu(v_ref[0, rs, :].astype(F32)), g_ref[...], b_ref[...]).astype(BF16)
        u = _gelu(u_ref[0, rs, :].astype(F32))
        for g in range(SGU_GROUPS):
            cs = slice(g * SGU_GROUP_CH, (g + 1) * SGU_GROUP_CH)
            s = jnp.dot(ws_ref[g], v[:, cs], preferred_element_type=F32) + bs_ref[g]
            o_ref[0, rs, cs] = (u[:, cs] * s).astype(o_ref.dtype)


def _sgu(h3, ln_g, ln_b, ws, bs, col0, tc=512):
    B, S, _ = h3.shape
    tc = min(tc, S)
    cb = col0 // SGU_WIDTH
    bs_b = jnp.broadcast_to(bs[:, :, None], (SGU_GROUPS, SGU_CHUNK, SGU_GROUP_CH)).astype(F32)
    return pl.pallas_call(
        functools.partial(_sgu_kernel, nchunk=tc // SGU_CHUNK),
        grid=(B, S // tc),
        in_specs=[pl.BlockSpec((1, tc, SGU_WIDTH), lambda b, i: (b, i, cb)),
                  pl.BlockSpec((1, tc, SGU_WIDTH), lambda b, i: (b, i, cb + 1)),
                  pl.BlockSpec((1, SGU_WIDTH), lambda b, i: (0, 0)),
                  pl.BlockSpec((1, SGU_WIDTH), lambda b, i: (0, 0)),
                  pl.BlockSpec((SGU_GROUPS, SGU_CHUNK, SGU_CHUNK), lambda b, i: (0, 0, 0)),
                  pl.BlockSpec((SGU_GROUPS, SGU_CHUNK, SGU_GROUP_CH), lambda b, i: (0, 0, 0))],
        out_specs=pl.BlockSpec((1, tc, SGU_WIDTH), lambda b, i: (b, i, 0)),
        out_shape=jax.ShapeDtypeStruct((B, S, SGU_WIDTH), BF16),
        compiler_params=_cparams(("parallel", "parallel")),
        name="sgu",
    )(h3, h3, ln_g.reshape(1, -1), ln_b.reshape(1, -1), ws.astype(BF16), bs_b)


def _rope_tile(x, c, sa, sb):
    return x * c + pltpu.roll(x, LANES - HEAD_DIM // 2, 1) * sa + pltpu.roll(x, HEAD_DIM // 2, 1) * sb


def _dil_kernel(q_ref, k_ref, v_ref, c_ref, sa_ref, sb_ref, o_ref, lse_ref, kr_ref, s_ref, p_ref, l_ref,
                *, L, QB, KB, half, dil):
    qi = pl.program_id(2)
    ntile = DIL_GW // LANES
    rchunk = min(L, 256)

    @pl.when(qi == 0)
    def _():
        for c0 in range(0, L, rchunk):
            rs = slice(c0, c0 + rchunk)
            for t in range(ntile):
                cs = slice(t * LANES, (t + 1) * LANES)
                kr_ref[rs, cs] = _rope_tile(k_ref[0, rs, cs].astype(F32), c_ref[rs, cs], sa_ref[rs, cs],
                                            sb_ref[rs, cs]).astype(BF16)

    n0 = pl.multiple_of(qi * QB, QB)
    start = pl.multiple_of(jnp.clip(n0 - half, 0, L - KB), 16)
    qn = n0 + lax.broadcasted_iota(jnp.int32, (QB, KB), 0)
    kn = start + lax.broadcasted_iota(jnp.int32, (QB, KB), 1)
    ok = jnp.abs(kn - qn) <= half
    first = lax.broadcasted_iota(jnp.int32, (QB, LANES), 1) < HEAD_DIM
    scale = HEAD_DIM ** -0.5 * LOG2E
    if dil == 1:
        orows = slice(None)
    else:
        orows = pl.ds(n0 * dil + pl.program_id(1), QB, stride=dil)
    for t in range(ntile):
        cs = slice(t * LANES, (t + 1) * LANES)
        qr = _rope_tile(q_ref[0, :, cs].astype(F32), c_ref[pl.ds(n0, QB), cs], sa_ref[pl.ds(n0, QB), cs],
                        sb_ref[pl.ds(n0, QB), cs]) * scale
        kt = kr_ref[pl.ds(start, KB), cs]
        for sub in range(2):
            qm = jnp.where(first, qr, 0.0) if sub == 0 else jnp.where(first, 0.0, qr)
            s_ref[2 * t + sub] = jnp.where(ok, _dot_nt(qm.astype(BF16), kt), NEG_BIG)
    for u in range(2 * ntile):
        s = s_ref[u]
        m = jnp.max(s, axis=-1, keepdims=True)
        e = jnp.exp2(s - m)
        den = jnp.sum(e, axis=-1, keepdims=True)
        p_ref[u] = (e * (1.0 / den)).astype(BF16)
        l_ref[u] = jnp.broadcast_to(m * (1.0 / LOG2E) + jnp.log(den), (QB, LANES))
    for t in range(ntile):
        vt = v_ref[0, pl.ds(start, KB), t * LANES:(t + 1) * LANES]
        outs = [jnp.dot(p_ref[2 * t + sub], vt, preferred_element_type=F32) for sub in range(2)]
        o_ref[0, t, orows, :] = jnp.where(first, outs[0], outs[1])
        lse_ref[0, t, orows, :] = jnp.where(first, l_ref[2 * t], l_ref[2 * t + 1])


def _rope_tables(S):
    half = HEAD_DIM // 2
    inv = ROPE_THETA ** (-jnp.arange(half, dtype=F32) / half)
    ang = jnp.arange(S).astype(F32)[:, None] * inv[None, :]
    cos, sin = jnp.cos(ang), jnp.sin(ang)
    z = jnp.zeros_like(sin)
    c = jnp.tile(jnp.concatenate([cos, cos], axis=-1), (1, DIL_HEADS))
    sa = jnp.tile(jnp.concatenate([-sin, z], axis=-1), (1, DIL_HEADS))
    sb = jnp.tile(jnp.concatenate([z, sin], axis=-1), (1, DIL_HEADS))
    return c, sa, sb


def _dil_attention(hsrc, tabs, g, B, S, col0=0):
    window, dil = DIL_PATTERNS[g]
    L = S // dil
    half = window // (2 * dil)
    QB = min(L, DIL_QUERY_BLOCK)
    KB = min(L, QB + 2 * half)
    ntile = DIL_GW // LANES
    if dil == 1:
        hv, cpb, qb = hsrc, hsrc.shape[-1] // DIL_GW, col0 // DIL_GW
        out_spec = pl.BlockSpec((1, ntile, QB, LANES), lambda b, r, i: (b, 0, i, 0))
    else:
        hv, cpb, qb = hsrc.reshape(B, L, dil * 3 * DIL_GW), 3, 0
        out_spec = pl.BlockSpec((1, ntile, S, LANES), lambda b, r, i: (b, 0, 0, 0))
    c, sa, sb = [t.reshape(L, dil * DIL_GW) for t in tabs]
    tab_spec = pl.BlockSpec((L, DIL_GW), lambda b, r, i: (0, r))
    o, lse = pl.pallas_call(
        functools.partial(_dil_kernel, L=L, QB=QB, KB=KB, half=half, dil=dil),
        grid=(B, dil, L // QB),
        in_specs=[pl.BlockSpec((1, QB, DIL_GW), lambda b, r, i: (b, i, r * cpb + qb)),
                  pl.BlockSpec((1, L, DIL_GW), lambda b, r, i: (b, 0, r * cpb + qb + 1)),
                  pl.BlockSpec((1, L, DIL_GW), lambda b, r, i: (b, 0, r * cpb + qb + 2)),
                  tab_spec, tab_spec, tab_spec],
        out_specs=[out_spec, out_spec],
        out_shape=[jax.ShapeDtypeStruct((B, ntile, S, LANES), F32)] * 2,
        scratch_shapes=[pltpu.VMEM((L, DIL_GW), BF16), pltpu.VMEM((DIL_HEADS, QB, KB), F32),
                        pltpu.VMEM((DIL_HEADS, QB, KB), BF16), pltpu.VMEM((DIL_HEADS, QB, LANES), F32)],
        compiler_params=_cparams(("parallel", "arbitrary", "arbitrary")),
        name=f"dil_attn_g{g}",
    )(hv, hv, hv, c, sa, sb)
    return o, lse


def _merge_kernel(x_ref, gate_ref, a_ref, b_ref, o0_ref, o1_ref, o2_ref, l0_ref, l1_ref, l2_ref, p_ref,
                  wa_ref, wb_ref, wc_ref, wo_ref, g1_ref, b1_ref, wpg_ref, wp_ref,
                  xb_ref, base_ref, *, alpha, D):
    a = jnp.dot(a_ref[...], wa_ref[...], preferred_element_type=F32)
    b = jnp.dot(b_ref[...], wb_ref[...], preferred_element_type=F32)
    halves = []
    for t in range(DIL_GW // LANES):
        l0, l1, l2 = l0_ref[0, t], l1_ref[0, t], l2_ref[0, t]
        m = jnp.maximum(jnp.maximum(l0, l1), l2)
        e0, e1, e2 = jnp.exp(l0 - m), jnp.exp(l1 - m), jnp.exp(l2 - m)
        halves.append((e0 * o0_ref[0, t] + e1 * o1_ref[0, t] + e2 * o2_ref[0, t]) * (1.0 / (e0 + e1 + e2)))
    cin = jnp.concatenate(halves, axis=-1)
    c = jnp.dot(cin.astype(BF16), wc_ref[...], preferred_element_type=F32)
    merged = (_sigmoid(gate_ref[:, 0:D].astype(F32)) * a
              + _sigmoid(gate_ref[:, D:2 * D].astype(F32)) * b
              + _sigmoid(gate_ref[:, 2 * D:3 * D].astype(F32)) * c)
    mix = jnp.dot(merged.astype(BF16), wo_ref[...], preferred_element_type=F32)
    x1 = _ln(alpha * x_ref[...] + mix, g1_ref[...], b1_ref[...])
    x1b = x1.astype(BF16)
    xb_ref[...] = x1b
    ple = (_sigmoid(jnp.dot(x1b, wpg_ref[...], preferred_element_type=F32))
           * jnp.dot(p_ref[...].astype(BF16), wp_ref[...], preferred_element_type=F32))
    base_ref[...] = alpha * x1 + ple


def _merge(x, h, na, sgu, dil_o, dil_l, p, wa, wb, wc, wo, g1, b1, wpg, wp, alpha, S, tm=512):
    T, D = x.shape
    tm = min(tm, S)
    per_seq = S // tm
    row = lambda w: pl.BlockSpec((tm, w), lambda i: (i, 0))
    full = lambda arr: pl.BlockSpec(arr.shape, lambda i: (0,) * arr.ndim)
    dil = pl.BlockSpec((1, DIL_GW // LANES, tm, LANES), lambda i: (i // per_seq, 0, i % per_seq, 0))
    g1, b1 = g1.reshape(1, D), b1.reshape(1, D)
    return pl.pallas_call(
        functools.partial(_merge_kernel, alpha=alpha, D=D),
        grid=(T // tm,),
        in_specs=[row(D), row(N_BRANCH * D), row(NA_WIDTH), row(SGU_WIDTH),
                  dil, dil, dil, dil, dil, dil,
                  row(p.shape[1]),
                  full(wa), full(wb), full(wc), full(wo), full(g1), full(b1), full(wpg), full(wp)],
        out_specs=[row(D), row(D)],
        out_shape=[jax.ShapeDtypeStruct((T, D), BF16), jax.ShapeDtypeStruct((T, D), F32)],
        compiler_params=_cparams(("parallel",)),
        name="merge",
    )(x, h, na, sgu, dil_o[0], dil_o[1], dil_o[2], dil_l[0], dil_l[1], dil_l[2], p,
      wa, wb, wc, wo, g1, b1, wpg, wp)


_CAND_ROWS = 16 + 7 * 8 + 8


def _cand_index():
    idx = [float(j) for j in range(16)]
    for i in range(1, 8):
        idx += [float(i * 16 + j) for j in range(8)]
    idx += [float(i * 16) for i in range(8, 16)]
    return np.broadcast_to(np.asarray(idx, np.float32)[:, None], (_CAND_ROWS, ROUTE_LW)).copy()


def _extract_top(s, order, n):
    rank = jnp.full(s.shape, 127.0, F32)
    vals = []
    for i in range(n):
        m = jnp.max(s, axis=0, keepdims=True)
        pick = jnp.min(jnp.where(s == m, order, 1e9), axis=0, keepdims=True)
        sel = order == pick
        rank = jnp.where(sel, float(i), rank)
        s = jnp.where(sel, -jnp.inf, s)
        vals.append(m)
    return rank, vals


def _batcher_pairs(n):
    pairs = []

    def merge(lo, hi, r):
        step = r * 2
        if step < hi - lo:
            merge(lo, hi, step)
            merge(lo + r, hi, step)
            pairs.extend((i, i + r) for i in range(lo + r, hi - r, step))
        else:
            pairs.append((lo, lo + r))

    def sort(lo, hi):
        if hi - lo >= 1:
            mid = lo + (hi - lo) // 2
            sort(lo, mid)
            sort(mid + 1, hi)
            merge(lo, hi, 1)

    sort(0, n - 1)
    return pairs


def _sort_desc(xs):
    xs = list(xs)
    for i, j in _batcher_pairs(PEER_TOPK):
        if j < len(xs):
            xs[i], xs[j] = jnp.maximum(xs[i], xs[j]), jnp.minimum(xs[i], xs[j])
    return xs


def _merge_sublanes(xs):
    xs = list(xs)
    n = PEER_TOPK
    for shift in (4, 2, 1):
        t = [jnp.maximum(xs[i], pltpu.roll(xs[n - 1 - i], shift, 0)) for i in range(n)]
        for d in (8, 4, 2, 1):
            for i in range(n):
                if i & d == 0:
                    t[i], t[i + d] = jnp.maximum(t[i], t[i + d]), jnp.minimum(t[i], t[i + d])
        xs = t
    return xs


def _search_bits(x, v):
    c3 = x >= v[7]
    c2 = x >= jnp.where(c3, v[3], v[11])
    c1 = x >= jnp.where(c3, jnp.where(c2, v[1], v[5]), jnp.where(c2, v[9], v[13]))
    c0 = x >= jnp.where(c3,
                        jnp.where(c2, jnp.where(c1, v[0], v[2]), jnp.where(c1, v[4], v[6])),
                        jnp.where(c2, jnp.where(c1, v[8], v[10]), jnp.where(c1, v[12], v[14])))
    return c3, c2, c1, c0


def _pick_by_rank(bits, vals):
    c3, c2, c1, c0 = bits
    l0 = [jnp.where(c0, vals[2 * k], vals[2 * k + 1]) for k in range(8)]
    l1 = [jnp.where(c1, l0[2 * k], l0[2 * k + 1]) for k in range(4)]
    l2 = [jnp.where(c2, l1[2 * k], l1[2 * k + 1]) for k in range(2)]
    return jnp.where(c3, l2[0], l2[1])


def _route_sorted(s1, s2):
    W = s1.shape[1]
    sub_rows = 8
    g1 = [s1[sub_rows * g:sub_rows * (g + 1)] for g in range(PEER_NKEYS // sub_rows)]
    g2 = [s2[sub_rows * g:sub_rows * (g + 1)] for g in range(PEER_NKEYS // sub_rows)]
    v1 = _merge_sublanes(_sort_desc(g1))
    v2 = _merge_sublanes(_sort_desc(g2))
    sub = lax.broadcasted_iota(jnp.int32, (sub_rows, W), 0)

    def pack(vs):
        out = vs[0]
        for s in range(1, sub_rows):
            out = jnp.where(sub == s, vs[s], out)
        return out

    e1r = [jnp.exp(v - v1[0]) for v in v1]
    e2r = [jnp.exp(v - v2[0]) for v in v2]
    v2lo, v2hi, v1hi = pack(v2[:8]), pack(v2[8:]), pack(v1[8:])
    e2lo, e2hi, e1hi = pack(e2r[:8]), pack(e2r[8:]), pack(e1r[8:])
    cand = [v1[0] + v2lo, v1[0] + v2hi] + [v1[i] + v2lo for i in range(1, 8)] + [v1hi + v2[0]]
    ecand = [e1r[0] * e2lo, e1r[0] * e2hi] + [e1r[i] * e2lo for i in range(1, 8)] + [e1hi * e2r[0]]
    neg = jnp.full((sub_rows, W), -jnp.inf, F32)
    tau = _merge_sublanes(_sort_desc(cand) + [neg] * (PEER_TOPK - len(cand)))[PEER_TOPK - 1]
    sel = [jnp.where(c >= tau, 1.0, 0.0) for c in cand]
    colsum = lambda x: jnp.sum(x, axis=0, keepdims=True)
    z = colsum(sum(s * e for s, e in zip(sel, ecand)))
    cnt = [colsum(sel[0] + sel[1])] + [colsum(sel[1 + i]) for i in range(1, 8)] + [sel[9][i:i + 1] for i in range(8)]
    half_minus_cnt = [0.5 - c for c in cnt]
    inv_z = 1.0 / z
    a1, nx1, e2, x2 = [], [], [], []
    n1 = jnp.zeros((sub_rows, W), F32)
    n2 = jnp.zeros((sub_rows, W), F32)
    for x in g1:
        inside = x >= v1[PEER_TOPK - 1]
        nx1.append(jnp.where(inside, _pick_by_rank(_search_bits(x, v1), half_minus_cnt), 1e9))
        a1.append(jnp.where(inside, jnp.exp(x - v1[0]), 0.0) * inv_z)
        n1 = n1 + jnp.where(inside, 1.0, 0.0)
    for x in g2:
        inside = x >= v2[PEER_TOPK - 1]
        c3, c2, c1, c0 = _search_bits(x, v2)
        rank = (jnp.where(c3, 0.0, 8.0) + jnp.where(c2, 0.0, 4.0)) + (jnp.where(c1, 0.0, 2.0) + jnp.where(c0, 0.0, 1.0))
        x2.append(jnp.where(inside, -rank, -127.0))
        e2.append(jnp.exp(x - v2[0]))
        n2 = n2 + jnp.where(inside, 1.0, 0.0)
    dup = jnp.zeros((sub_rows, W), F32)
    for i in range(PEER_TOPK - 1):
        dup = dup + jnp.where(v1[i] == v1[i + 1], 1.0, 0.0) + jnp.where(v2[i] == v2[i + 1], 1.0, 0.0)
    flag = (jnp.abs(colsum(n1) - PEER_TOPK) + jnp.abs(colsum(n2) - PEER_TOPK)
            + jnp.abs(colsum(sum(sel)) - PEER_TOPK) + colsum(dup))
    cat = lambda xs: jnp.concatenate(xs, axis=0)
    return cat(a1), cat(nx1), cat(e2), cat(x2), flag


def _row_tile(ref, h, a, ls):
    row = ref[h, a:a + 1, ls]
    return jnp.broadcast_to(row, (BF16_ROWS, row.shape[1])).astype(BF16)


def _route_chunk(s1, s2, cidx):
    key_order = lax.broadcasted_iota(jnp.int32, s1.shape, 0).astype(F32)
    rank1, v1 = _extract_top(s1, key_order, PEER_TOPK)
    rank2, v2 = _extract_top(s2, key_order, PEER_TOPK)
    v1a = jnp.concatenate(v1, axis=0)
    v2a = jnp.concatenate(v2, axis=0)
    e1r = jnp.exp(v1a - v1[0])
    e2r = jnp.exp(v2a - v2[0])

    def pairs(r1, r2, op):
        blocks = [op(r1[0:1], r2)]
        blocks += [op(r1[i:i + 1], r2[0:8]) for i in range(1, 8)]
        blocks += [op(r1[8:16], r2[0:1])]
        return jnp.concatenate(blocks, axis=0)

    cand = pairs(v1a, v2a, jnp.add)
    ecand = pairs(e1r, e2r, jnp.multiply)
    crank, _ = _extract_top(cand, cidx, PEER_TOPK)
    sel = jnp.where(crank < 100.0, 1.0, 0.0)
    z = jnp.sum(sel * ecand, axis=0, keepdims=True)
    cnt = [jnp.sum(sel[0:16], axis=0, keepdims=True)]
    cnt += [jnp.sum(sel[16 + 8 * (i - 1):16 + 8 * i], axis=0, keepdims=True) for i in range(1, 8)]
    cnt += [sel[72 + i:73 + i] for i in range(8)]
    nx1 = jnp.full(s1.shape, 1e9, F32)
    for i in range(PEER_TOPK):
        nx1 = jnp.where(rank1 == float(i), 0.5 - cnt[i], nx1)
    a1 = jnp.where(rank1 < 100.0, jnp.exp(s1 - v1[0]), 0.0) * (1.0 / z)
    e2 = jnp.exp(s2 - v2[0])
    return a1, nx1, e2, -rank2


def _route_kernel(x_ref, wq_ref, k1_ref, k2_ref, cidx_ref, a1_ref, nx1_ref, e2_ref, x2_ref, q_ref, s1_ref, s2_ref,
                  *, tb):
    hk = PEER_DKEY // 2
    h = pl.program_id(1)

    @pl.when(h == 0)
    def _():
        for hh in range(PEER_HEADS):
            q_ref[hh] = jnp.dot(x_ref[...], wq_ref[:, hh * PEER_DKEY:(hh + 1) * PEER_DKEY],
                                preferred_element_type=F32).astype(BF16)

    s1_ref[...] = _dot_nt(k1_ref[0], q_ref[h, :, :hk])
    s2_ref[...] = _dot_nt(k2_ref[0], q_ref[h, :, hk:])

    def chunk(c, carry):
        ls = pl.ds(pl.multiple_of(c * ROUTE_LW, ROUTE_LW), ROUTE_LW)

        def emit(a1, nx1, e2, x2):
            a1_ref[0, :, ls] = a1
            nx1_ref[0, :, ls] = nx1
            e2_ref[0, :, ls] = e2.astype(e2_ref.dtype)
            x2_ref[0, :, ls] = x2.astype(x2_ref.dtype)

        *fast, flag = _route_sorted(s1_ref[:, ls], s2_ref[:, ls])
        emit(*fast)

        @pl.when(jnp.max(flag) > 0.0)
        def _():
            emit(*_route_chunk(s1_ref[:, ls], s2_ref[:, ls], cidx_ref[...]))

        return carry

    lax.fori_loop(0, tb // ROUTE_LW, chunk, 0)


def _peer_route(xb, wq, k1, k2, tb):
    T, D = xb.shape
    H = PEER_HEADS
    cidx = jnp.asarray(_cand_index())
    out_spec = pl.BlockSpec((1, PEER_NKEYS, tb), lambda i, h: (h, 0, i))
    out_sds = lambda dt: jax.ShapeDtypeStruct((H, PEER_NKEYS, T), dt)
    return pl.pallas_call(
        functools.partial(_route_kernel, tb=tb),
        grid=(T // tb, H),
        in_specs=[pl.BlockSpec((tb, D), lambda i, h: (i, 0)),
                  pl.BlockSpec((D, H * PEER_DKEY), lambda i, h: (0, 0)),
                  pl.BlockSpec((1, PEER_NKEYS, PEER_DKEY // 2), lambda i, h: (h, 0, 0)),
                  pl.BlockSpec((1, PEER_NKEYS, PEER_DKEY // 2), lambda i, h: (h, 0, 0)),
                  pl.BlockSpec((_CAND_ROWS, ROUTE_LW), lambda i, h: (0, 0))],
        out_specs=[out_spec] * 4,
        out_shape=[out_sds(F32), out_sds(F32), out_sds(BF16), out_sds(BF16)],
        scratch_shapes=[pltpu.VMEM((H, tb, PEER_DKEY), BF16),
                        pltpu.VMEM((PEER_NKEYS, tb), F32), pltpu.VMEM((PEER_NKEYS, tb), F32)],
        compiler_params=_cparams(("parallel", "arbitrary")),
        name="peer_route",
    )(xb, wq, k1, k2, cidx)


def _dense_kernel(x_ref, u_ref, vt_ref, a1_ref, nx1_ref, e2_ref, x2_ref, base_ref, g_ref, b_ref,
                  yf_ref, yb_ref, acc_ref, act_ref, pw_ref, *, eb):
    e = pl.program_id(1)
    last = pl.num_programs(1) - 1
    tb = x_ref.shape[0]
    zero = jnp.zeros((), BF16)

    @pl.when(e == 0)
    def _():
        acc_ref[...] = jnp.zeros_like(acc_ref)

    for sb in range(eb // DENSE_SUB):
        rows_e = slice(sb * DENSE_SUB, (sb + 1) * DENSE_SUB)
        act_ref[rows_e, :] = _gelu(_dot_nt(u_ref[rows_e, :], x_ref[...]).astype(BF16))
    nk = PEER_NKEYS // BF16_ROWS

    for lt in range(tb // DENSE_LW):
        ls = slice(lt * DENSE_LW, (lt + 1) * DENSE_LW)
        for al in range(eb // PEER_NKEYS):
            w = [None] * nk
            for h in range(PEER_HEADS):
                nxb = _row_tile(nx1_ref, h, al, ls)
                a1b = _row_tile(a1_ref, h, al, ls)
                for k in range(nk):
                    rows = slice(k * BF16_ROWS, (k + 1) * BF16_ROWS)
                    term = jnp.where(x2_ref[h, rows, ls] >= nxb, e2_ref[h, rows, ls], zero) * a1b
                    w[k] = term if w[k] is None else w[k] + term
            for k in range(nk):
                r0 = al * PEER_NKEYS + k * BF16_ROWS
                pw_ref[r0:r0 + BF16_ROWS, ls] = w[k] * act_ref[r0:r0 + BF16_ROWS, ls]
        acc_ref[ls, :] += lax.dot_general(pw_ref[:, ls], vt_ref[...], (((0,), (0,)), ((), ())),
                                          preferred_element_type=F32)

    @pl.when(e == last)
    def _():
        y = _ln(base_ref[...] + acc_ref[...], g_ref[...], b_ref[...])
        yf_ref[...] = y
        yb_ref[...] = y.astype(BF16)


def _peer_dense(xb, u, vt, route, base, g2, b2, tb, eb=2048):
    T, D = xb.shape
    NE = u.shape[0]
    H = PEER_HEADS
    rspec = pl.BlockSpec((H, PEER_NKEYS, tb), lambda i, e: (0, 0, i))
    aspec = pl.BlockSpec((H, eb // PEER_NKEYS, tb), lambda i, e: (0, e, i))
    return pl.pallas_call(
        functools.partial(_dense_kernel, eb=eb),
        grid=(T // tb, NE // eb),
        in_specs=[pl.BlockSpec((tb, D), lambda i, e: (i, 0)),
                  pl.BlockSpec((eb, D), lambda i, e: (e, 0)),
                  pl.BlockSpec((eb, D), lambda i, e: (e, 0)),
                  aspec, aspec, rspec, rspec,
                  pl.BlockSpec((tb, D), lambda i, e: (i, 0)),
                  pl.BlockSpec((1, D), lambda i, e: (0, 0)),
                  pl.BlockSpec((1, D), lambda i, e: (0, 0))],
        out_specs=[pl.BlockSpec((tb, D), lambda i, e: (i, 0)),
                   pl.BlockSpec((tb, D), lambda i, e: (i, 0))],
        out_shape=[jax.ShapeDtypeStruct((T, D), F32), jax.ShapeDtypeStruct((T, D), BF16)],
        scratch_shapes=[pltpu.VMEM((tb, D), F32), pltpu.VMEM((eb, tb), BF16), pltpu.VMEM((eb, tb), BF16)],
        compiler_params=_cparams(("parallel", "arbitrary")),
        name="peer_dense",
    )(xb, u, vt, route[0], route[1], route[2], route[3], base, g2.reshape(1, D), b2.reshape(1, D))


def _layer_weights(i, w_in, rpb, sgu_ln_g, sgu_ln_b, sgu_w, sgu_b, w_br_a, w_br_b, w_br_c, w_out, ln1_g, ln1_b,
                   peer_wq, peer_k1, peer_k2, peer_u, peer_v, ple_w, ple_gate_w, ln2_g, ln2_b):
    D = w_in.shape[1]
    off_c = 3 * NA_WIDTH + 2 * SGU_WIDTH + 3 * DIL_WIDTH
    off_b = 3 * NA_WIDTH + 2 * SGU_WIDTH

    def group_w(g):
        return jnp.concatenate(
            [w_in[i][:, off_b + part * DIL_WIDTH + g * DIL_GW: off_b + part * DIL_WIDTH + (g + 1) * DIL_GW]
             for part in range(3)], axis=1).astype(BF16)

    w_perm = jnp.concatenate([w_in[i][:, off_c:].astype(BF16), w_in[i][:, :off_b].astype(BF16), group_w(0)], axis=1)
    return dict(
        w_in=w_perm, w_dil={g: group_w(g) for g in range(1, len(DIL_PATTERNS))},
        col_a=N_BRANCH * D, col_b=N_BRANCH * D + 3 * NA_WIDTH,
        col_c=N_BRANCH * D + 3 * NA_WIDTH + 2 * SGU_WIDTH,
        na_bias=_na_bias_table(rpb[i]),
        sgu_ln_g=sgu_ln_g[i], sgu_ln_b=sgu_ln_b[i], sgu_w=sgu_w[i], sgu_b=sgu_b[i],
        wa=w_br_a[i].astype(BF16), wb=w_br_b[i].astype(BF16), wc=w_br_c[i].astype(BF16),
        wo=w_out[i].astype(BF16), ln1_g=ln1_g[i], ln1_b=ln1_b[i],
        wq=peer_wq[i].astype(BF16), k1=peer_k1[i].astype(BF16), k2=peer_k2[i].astype(BF16),
        u=peer_u[i].astype(BF16), vt=peer_v[i].astype(BF16),
        wp=ple_w[i].astype(BF16), wpg=ple_gate_w[i].astype(BF16), ln2_g=ln2_g[i], ln2_b=ln2_b[i])


def _trunk(x, p, ln0_g, ln0_b, layers, alpha):
    B, S, D = x.shape
    T = B * S
    tb = min(512, T)
    tabs = _rope_tables(S)
    xf, xb = _ln0(x.reshape(T, D), ln0_g, ln0_b)
    for i, lw in enumerate(layers):
        h = _inproj(xb, lw["w_in"])
        h3 = h.reshape(B, S, -1)
        na = _na_attention(h3, lw["na_bias"], lw["col_a"]).reshape(T, NA_WIDTH)
        sg = _sgu(h3, lw["sgu_ln_g"], lw["sgu_ln_b"], lw["sgu_w"], lw["sgu_b"], lw["col_b"]).reshape(T, SGU_WIDTH)
        dil = [_dil_attention(h3, tabs, 0, B, S, lw["col_c"])]
        for g in range(1, len(DIL_PATTERNS)):
            hd = _inproj_residue_major(xb, lw["w_dil"][g], DIL_PATTERNS[g][1])
            dil.append(_dil_attention(hd, tabs, g, B, S))
        x1b, base = _merge(xf, h, na, sg, [d[0] for d in dil], [d[1] for d in dil], p[i].reshape(T, -1),
                           lw["wa"], lw["wb"], lw["wc"], lw["wo"], lw["ln1_g"], lw["ln1_b"],
                           lw["wpg"], lw["wp"], alpha, S)
        route = _peer_route(x1b, lw["wq"], lw["k1"], lw["k2"], min(ROUTE_TB, T))
        xf, xb = _peer_dense(x1b, lw["u"], lw["vt"], route, base, lw["ln2_g"], lw["ln2_b"], tb)
    return xf.reshape(B, S, D)


def kernel(x_prompt, x_sample, p_prompt, p_sample, ln0_g, ln0_b, w_in, rpb, sgu_ln_g, sgu_ln_b, sgu_w, sgu_b, w_br_a, w_br_b, w_br_c, w_out, ln1_g, ln1_b, peer_wq, peer_k1, peer_k2, peer_u, peer_v, ple_w, ple_gate_w, ln2_g, ln2_b):
    depth = w_in.shape[0]
    alpha = (2 * depth) ** 0.25
    layers = [_layer_weights(i, w_in, rpb, sgu_ln_g, sgu_ln_b, sgu_w, sgu_b, w_br_a, w_br_b, w_br_c, w_out,
                             ln1_g, ln1_b, peer_wq, peer_k1, peer_k2, peer_u, peer_v, ple_w, ple_gate_w,
                             ln2_g, ln2_b) for i in range(depth)]
    y_prompt = _trunk(x_prompt, p_prompt, ln0_g, ln0_b, layers, alpha)
    y_sample = _trunk(x_sample, p_sample, ln0_g, ln0_b, layers, alpha)
    return (y_prompt, y_sample)
```
